```python
import math
import jax
import jax.numpy as jnp
from jax import lax
import numpy as np

D_MODEL = 1024
BATCH = 8
SEQ = 2048
DEPTH = 2
DEC_BATCH = 32
DEC_SEQ = 8
PAST_LEN = 8192
PAGE_SIZE = 128

HEAD_DIM = 64
BRANCH_W = D_MODEL // 2
N_BRANCH = 4
N_HEADS = BRANCH_W // HEAD_DIM
S5_GROUP = 16
S5_GROUPS = BRANCH_W // S5_GROUP
S5_STATE = 64
RWKV_DECAY_LORA = 64
RWKV_ICLR_LORA = 64
RWKV_GATE_LORA = 128
RWKV_IN = 3 * BRANCH_W + RWKV_DECAY_LORA + RWKV_ICLR_LORA + RWKV_GATE_LORA
MOBA_BLOCK = 256
MOBA_TOPK = 3
MOBA_Q_CHUNK = 16
Q_BLOCK = 128
D_FF = ((8 * D_MODEL // 3 + 127) // 128) * 128
CONV_W = 3
EPS = 1e-6
RWKV_GN_EPS = 64e-5

OFF_S5 = 0
OFF_RWKV = OFF_S5 + BRANCH_W
OFF_SB = OFF_RWKV + RWKV_IN
OFF_MOBA = OFF_SB + 3 * BRANCH_W
OFF_GATE = OFF_MOBA + 3 * BRANCH_W
N_IN = OFF_GATE + N_BRANCH * D_MODEL

kernel_name = 'hybrid_s5_rwkv7_stickbreak_moba_decoder_step'


def split_heads(u):
    return u.reshape(u.shape[:-1] + (N_HEADS, HEAD_DIM))


def rms_norm(x, g):
    xf = x.astype(jnp.float32)
    y = xf * lax.rsqrt(jnp.mean(xf * xf, axis=-1, keepdims=True) + EPS)
    return (y * g.astype(jnp.float32)).astype(x.dtype)


def _complex_linear_combine(e1, e2):
    a1r, a1i, b1r, b1i = e1
    a2r, a2i, b2r, b2i = e2
    return (a2r * a1r - a2i * a1i, a2r * a1i + a2i * a1r,
            a2r * b1r - a2i * b1i + b2r, a2r * b1i + a2i * b1r + b2i)


def s5_mixer(u, s_re0, s_im0, a_re, a_im, log_dt, b_re, b_im, c_re, c_im, d_skip, w_glu):
    f32 = jnp.float32
    bsz, t, _ = u.shape
    uf = u.astype(f32)
    ug = uf.reshape(bsz, t, S5_GROUPS, S5_GROUP)
    lr = a_re.astype(f32)
    li = a_im.astype(f32)
    dt = jnp.exp(log_dt.astype(f32))[:, None]
    mag = jnp.exp(lr * dt)
    ab_re = mag * jnp.cos(li * dt)
    ab_im = mag * jnp.sin(li * dt)
    den = lr * lr + li * li
    zr = ((ab_re - 1.0) * lr + ab_im * li) / den
    zi = (ab_im * lr - (ab_re - 1.0) * li) / den
    br = b_re.astype(f32)
    bi = b_im.astype(f32)
    bb_re = zr[..., None] * br - zi[..., None] * bi
    bb_im = zr[..., None] * bi + zi[..., None] * br
    x_re = jnp.einsum('btgc,gpc->btgp', ug, bb_re)
    x_im = jnp.einsum('btgc,gpc->btgp', ug, bb_im)
    s0r = s_re0.astype(f32)
    s0i = s_im0.astype(f32)
    x_re = x_re.at[:, 0].add(ab_re * s0r - ab_im * s0i)
    x_im = x_im.at[:, 0].add(ab_re * s0i + ab_im * s0r)
    a_re_t = jnp.broadcast_to(ab_re, x_re.shape)
    a_im_t = jnp.broadcast_to(ab_im, x_im.shape)
    _, _, s_re, s_im = lax.associative_scan(_complex_linear_combine, (a_re_t, a_im_t, x_re, x_im), axis=1)
    y = jnp.einsum('btgp,gcp->btgc', s_re, c_re.astype(f32)) - jnp.einsum('btgp,gcp->btgc', s_im, c_im.astype(f32))
    y = y.reshape(bsz, t, BRANCH_W) + d_skip.astype(f32) * uf
    y = jax.nn.gelu(y)
    y = y * jax.nn.sigmoid(y @ w_glu.astype(f32))
    return y.astype(u.dtype), s_re[:, -1].astype(u.dtype), s_im[:, -1].astype(u.dtype)


def _rwkv_step(state, inp):
    r_t, w_t, k_t, v_t, a_t, b_t = inp
    sa = jnp.einsum('bhij,bhj->bhi', state, a_t)
    state = state * w_t[:, :, None, :] + sa[..., None] * b_t[:, :, None, :] + v_t[..., None] * k_t[:, :, None, :]
    y_t = jnp.einsum('bhij,bhj->bhi', state, r_t)
    return state, y_t


def rwkv_mixer(u, shift0, wkv0, mu, w0, w_up, a0, a_up, g_up, k_k, k_a, r_k, ln_w, ln_b):
    f32 = jnp.float32
    bsz, t, _ = u.shape
    W = BRANCH_W
    uf = u.astype(f32)
    prev = jnp.concatenate([shift0.astype(f32)[:, None], uf[:, :-1]], axis=1)
    xs = uf + mu.astype(f32) * (prev - uf)
    r = xs[..., :W]
    k = xs[..., W:2 * W]
    v = xs[..., 2 * W:3 * W]
    o1 = 3 * W + RWKV_DECAY_LORA
    o2 = o1 + RWKV_ICLR_LORA
    w = -jax.nn.softplus(-(w0 + jnp.tanh(xs[..., 3 * W:o1]) @ w_up)) - 0.5
    decay = jnp.exp(-jnp.exp(w))
    a = jax.nn.sigmoid(a0 + xs[..., o1:o2] @ a_up)
    g = jax.nn.sigmoid(xs[..., o2:]) @ g_up
    kk = split_heads(k * k_k)
    kk = kk / jnp.maximum(jnp.linalg.norm(kk, axis=-1, keepdims=True), 1e-12)
    k = k * (1.0 + (a - 1.0) * k_a)
    rh, kh, vh, ah, dh = split_heads(r), split_heads(k), split_heads(v), split_heads(a), split_heads(decay)
    seq = (jnp.moveaxis(rh, 1, 0), jnp.moveaxis(dh, 1, 0), jnp.moveaxis(kh, 1, 0),
           jnp.moveaxis(vh, 1, 0), jnp.moveaxis(-kk, 1, 0), jnp.moveaxis(kk * ah, 1, 0))
    s_fin, ys = lax.scan(_rwkv_step, wkv0.astype(f32), seq)
    y = jnp.moveaxis(ys, 0, 1)
    mean = jnp.mean(y, axis=-1, keepdims=True)
    var = jnp.mean(jnp.square(y - mean), axis=-1, keepdims=True)
    yn = ((y - mean) * lax.rsqrt(var + RWKV_GN_EPS)).reshape(bsz, t, W) * ln_w + ln_b
    bonus = (jnp.sum(rh * kh * r_k, axis=-1, keepdims=True) * vh).reshape(bsz, t, W)
    out = (yn + bonus) * g
    return out.astype(u.dtype), s_fin.astype(u.dtype), u[:, -1]


def qkv_heads(u, g_q, g_k):
    q, k, v = jnp.split(u, 3, axis=-1)
    return rms_norm(split_heads(q), g_q), rms_norm(split_heads(k), g_k), split_heads(v)


def stick_breaking_attend(q, k, v, q_off):
    bsz, tq, _, d = q.shape
    lk = k.shape[1]
    qb = Q_BLOCK if tq % Q_BLOCK == 0 else tq
    scale = d ** -0.5
    outs = []
    for i in range(tq // qb):
        t0 = q_off + i * qb
        kend = min(lk, t0 + qb)
        q_i = q[:, i * qb:(i + 1) * qb].astype(jnp.float32)
        k_i = k[:, :kend].astype(jnp.float32)
        v_i = v[:, :kend].astype(jnp.float32)
        z = jnp.einsum('bqhd,bshd->bhqs', q_i, k_i) * scale
        mask = jnp.arange(kend)[None, :] < (t0 + jnp.arange(qb))[:, None]
        log_1m = jnp.where(mask, jax.nn.log_sigmoid(-z), 0.0)
        rest = lax.cumsum(log_1m, axis=3, reverse=True) - log_1m
        w = jnp.where(mask, jnp.exp(jax.nn.log_sigmoid(z) + rest), 0.0)
        outs.append(jnp.einsum('bhqs,bshd->bqhd', w, v_i))
    return jnp.concatenate(outs, axis=1).astype(q.dtype)


def moba_attend(q, k, v, q_off):
    f32 = jnp.float32
    bsz, tq, nh, d = q.shape
    lk = k.shape[1]
    nb = -(-lk // MOBA_BLOCK)
    pad = nb * MOBA_BLOCK - lk
    kp = jnp.pad(k, ((0, 0), (0, pad), (0, 0), (0, 0)))
    vp = jnp.pad(v, ((0, 0), (0, pad), (0, 0), (0, 0)))
    kb = kp.reshape(bsz, nb, MOBA_BLOCK, nh, d).transpose(0, 3, 1, 2, 4)
    vb = vp.reshape(bsz, nb, MOBA_BLOCK, nh, d).transpose(0, 3, 1, 2, 4)
    scale = d ** -0.5
    kmean = jnp.mean(kb.astype(f32), axis=3)
    qf = q.astype(f32).transpose(0, 2, 1, 3)
    own = (q_off + jnp.arange(tq)) // MOBA_BLOCK
    gate = jnp.einsum('bhqd,bhnd->bhqn', qf, kmean)
    gate = jnp.where(jnp.arange(nb)[None, :] < own[:, None], gate, -jnp.inf)
    ksel = min(MOBA_TOPK, nb)
    _, sel = lax.top_k(gate, ksel)
    sel_ok = jnp.arange(ksel)[None, :] < own[:, None]
    base = (jnp.arange(bsz)[:, None] * nh + jnp.arange(nh)[None, :]) * nb
    idx = base[:, :, None, None] + sel
    kflat = kb.reshape(bsz * nh * nb, MOBA_BLOCK, d)
    vflat = vb.reshape(bsz * nh * nb, MOBA_BLOCK, d)
    qc = math.gcd(tq, MOBA_Q_CHUNK)
    nc = tq // qc

    def chunk(args):
        c, q_c, idx_c, ok_c = args
        ks = kflat[idx_c].astype(f32)
        vs = vflat[idx_c].astype(f32)
        t0 = q_off + c * qc
        ob = t0 // MOBA_BLOCK
        k_own = lax.dynamic_index_in_dim(kb, ob, axis=2, keepdims=False).astype(f32)
        v_own = lax.dynamic_index_in_dim(vb, ob, axis=2, keepdims=False).astype(f32)
        s_sel = jnp.einsum('bhqd,bhqnkd->bhqnk', q_c, ks) * scale
        s_sel = jnp.where(ok_c[None, None, :, :, None], s_sel, -jnp.inf)
        s_own = jnp.einsum('bhqd,bhkd->bhqk', q_c, k_own) * scale
        kpos = ob * MOBA_BLOCK + jnp.arange(MOBA_BLOCK)
        tpos = t0 + jnp.arange(qc)
        s_own = jnp.where(kpos[None, :] <= tpos[:, None], s_own, -jnp.inf)
        s = jnp.concatenate([s_sel.reshape(bsz, nh, qc, ksel * MOBA_BLOCK), s_own], axis=-1)
        p = jax.nn.softmax(s, axis=-1)
        p_sel = p[..., :ksel * MOBA_BLOCK].reshape(bsz, nh, qc, ksel, MOBA_BLOCK)
        p_own = p[..., ksel * MOBA_BLOCK:]
        return jnp.einsum('bhqnk,bhqnkd->bhqd', p_sel, vs) + jnp.einsum('bhqk,bhkd->bhqd', p_own, v_own)

    xs = (jnp.arange(nc),
          qf.reshape(bsz, nh, nc, qc, d).transpose(2, 0, 1, 3, 4),
          idx.reshape(bsz, nh, nc, qc, ksel).transpose(2, 0, 1, 3, 4),
          sel_ok.reshape(nc, qc, ksel))
    out = lax.map(chunk, xs)
    out = out.transpose(1, 0, 3, 2, 4).reshape(bsz, tq, nh, d)
    return out.astype(q.dtype)


def conv_ffn(h, conv0, w_up, conv_w, conv_b, w_down):
    up = h @ w_up
    t = up.shape[1]
    ext = jnp.concatenate([conv0.astype(up.dtype), up], axis=1)
    c = conv_b + conv_w[CONV_W - 1] * ext[:, CONV_W - 1:CONV_W - 1 + t]
    for j in range(CONV_W - 1):
        c = c + conv_w[j] * ext[:, j:j + t]
    a, b = jnp.split(c, 2, axis=-1)
    return (jax.nn.silu(a) * b) @ w_down, ext[:, t:]


def gather_pages(pool, page_table):
    g = pool[page_table]
    return g.reshape(page_table.shape[0], page_table.shape[1] * pool.shape[1], N_HEADS, HEAD_DIM)


def trunk_layer(x, past_len, s5_re0, s5_im0, shift0, wkv0, conv0, past_kv, p):
    bsz, t, _ = x.shape
    h = rms_norm(x, p['norm1_g'])
    u = h @ p['w_in']
    gates = jax.nn.sigmoid(u[..., OFF_GATE:].reshape(bsz, t, N_BRANCH, D_MODEL))
    y_s5, s5_re, s5_im = s5_mixer(u[..., OFF_S5:OFF_RWKV], s5_re0, s5_im0, p['s5_a_re'], p['s5_a_im'],
                                  p['s5_log_dt'], p['s5_b_re'], p['s5_b_im'], p['s5_c_re'], p['s5_c_im'],
                                  p['s5_d'], p['s5_w_glu'])
    y_rw, wkv, shift = rwkv_mixer(u[..., OFF_RWKV:OFF_SB], shift0, wkv0, p['rwkv_mu'], p['rwkv_w0'],
                                  p['rwkv_w_up'], p['rwkv_a0'], p['rwkv_a_up'], p['rwkv_g_up'],
                                  p['rwkv_k_k'], p['rwkv_k_a'], p['rwkv_r_k'], p['rwkv_ln_w'], p['rwkv_ln_b'])
    q_sb, k_sb, v_sb = qkv_heads(u[..., OFF_SB:OFF_MOBA], p['sb_q_g'], p['sb_k_g'])
    q_mb, k_mb, v_mb = qkv_heads(u[..., OFF_MOBA:OFF_GATE], p['moba_q_g'], p['moba_k_g'])
    if past_kv is None:
        ks_all, vs_all, km_all, vm_all = k_sb, v_sb, k_mb, v_mb
    else:
        pk_sb, pv_sb, pk_mb, pv_mb = past_kv
        ks_all = jnp.concatenate([pk_sb.astype(k_sb.dtype), k_sb], axis=1)
        vs_all = jnp.concatenate([pv_sb.astype(v_sb.dtype), v_sb], axis=1)
        km_all = jnp.concatenate([pk_mb.astype(k_mb.dtype), k_mb], axis=1)
        vm_all = jnp.concatenate([pv_mb.astype(v_mb.dtype), v_mb], axis=1)
    y_sb = stick_breaking_attend(q_sb, ks_all, vs_all, past_len).reshape(bsz, t, BRANCH_W)
    y_mb = moba_attend(q_mb, km_all, vm_all, past_len).reshape(bsz, t, BRANCH_W)
    ys = jnp.stack([y_s5, y_rw, y_sb, y_mb], axis=2)
    merged = jnp.sum(gates * jnp.einsum('btnw,nwd->btnd', ys, p['w_branch']), axis=2)
    x = x + merged @ p['w_out']
    f, conv = conv_ffn(rms_norm(x, p['norm2_g']), conv0, p['ffn_w_up'], p['ffn_conv_w'],
                       p['ffn_conv_b'], p['ffn_w_down'])
    x = x + f
    return x, (s5_re, s5_im, wkv, shift, conv, k_sb, v_sb, k_mb, v_mb)


def setup_inputs(seed: int = 0) -> dict:
    key = jax.random.key(seed)
    keys = iter(jax.random.split(key, 64))
    f32 = jnp.float32
    n_pages = PAST_LEN // PAGE_SIZE
    n_pool = (DEC_BATCH * n_pages * 5) // 4

    def nrm(shape, scale):
        return scale * jax.random.normal(next(keys), shape, f32)

    def uni(shape, lo, hi):
        return jax.random.uniform(next(keys), shape, f32, minval=lo, maxval=hi)

    cache_shape = (DEPTH, n_pool, PAGE_SIZE, N_HEADS, HEAD_DIM)
    inp = {}
    inp['x_prompt'] = nrm((BATCH, SEQ, D_MODEL), 1.0)
    inp['x_sample'] = nrm((DEC_BATCH, DEC_SEQ, D_MODEL), 1.0)
    inp['state_s5_re'] = nrm((DEPTH, DEC_BATCH, S5_GROUPS, S5_STATE), 0.5)
    inp['state_s5_im'] = nrm((DEPTH, DEC_BATCH, S5_GROUPS, S5_STATE), 0.5)
    inp['state_rwkv_wkv'] = nrm((DEPTH, DEC_BATCH, N_HEADS, HEAD_DIM, HEAD_DIM), 0.3)
    inp['state_rwkv_shift'] = nrm((DEPTH, DEC_BATCH, RWKV_IN), 1.0)
    inp['state_ffn_conv'] = nrm((DEPTH, DEC_BATCH, CONV_W - 1, 2 * D_FF), 1.0)
    inp['cache_sb_k'] = nrm(cache_shape, 1.0)
    inp['cache_sb_v'] = nrm(cache_shape, 1.0)
    inp['cache_moba_k'] = nrm(cache_shape, 1.0)
    inp['cache_moba_v'] = nrm(cache_shape, 1.0)
    perm = jax.random.permutation(next(keys), n_pool)
    inp['page_table'] = perm[:DEC_BATCH * n_pages].reshape(DEC_BATCH, n_pages).astype(jnp.int32)
    inp['norm1_g'] = 1.0 + nrm((DEPTH, D_MODEL), 0.02)
    inp['w_in'] = nrm((DEPTH, D_MODEL, N_IN), D_MODEL ** -0.5)
    inp['s5_a_re'] = -0.5 + nrm((DEPTH, S5_GROUPS, S5_STATE), 0.02)
    inp['s5_a_im'] = jnp.pi * jnp.arange(S5_STATE, dtype=f32) + nrm((DEPTH, S5_GROUPS, S5_STATE), 0.02)
    inp['s5_log_dt'] = uni((DEPTH, S5_GROUPS), math.log(1e-3), math.log(1e-1))
    inp['s5_b_re'] = nrm((DEPTH, S5_GROUPS, S5_STATE, S5_GROUP), (2 * S5_GROUP) ** -0.5)
    inp['s5_b_im'] = nrm((DEPTH, S5_GROUPS, S5_STATE, S5_GROUP), (2 * S5_GROUP) ** -0.5)
    inp['s5_c_re'] = nrm((DEPTH, S5_GROUPS, S5_GROUP, S5_STATE), S5_STATE ** -0.5)
    inp['s5_c_im'] = nrm((DEPTH, S5_GROUPS, S5_GROUP, S5_STATE), S5_STATE ** -0.5)
    inp['s5_d'] = nrm((DEPTH, BRANCH_W), 1.0)
    inp['s5_w_glu'] = nrm((DEPTH, BRANCH_W, BRANCH_W), BRANCH_W ** -0.5)
    inp['rwkv_mu'] = uni((DEPTH, RWKV_IN), 0.0, 1.0)
    inp['rwkv_w0'] = uni((DEPTH, BRANCH_W), -6.0, -1.0)
    inp['rwkv_w_up'] = nrm((DEPTH, RWKV_DECAY_LORA, BRANCH_W), 0.1)
    inp['rwkv_a0'] = nrm((DEPTH, BRANCH_W), 0.1)
    inp['rwkv_a_up'] = nrm((DEPTH, RWKV_ICLR_LORA, BRANCH_W), 0.1)
    inp['rwkv_g_up'] = nrm((DEPTH, RWKV_GATE_LORA, BRANCH_W), RWKV_GATE_LORA ** -0.5)
    inp['rwkv_k_k'] = 0.85 + nrm((DEPTH, BRANCH_W), 0.02)
    inp['rwkv_k_a'] = 1.0 + nrm((DEPTH, BRANCH_W), 0.02)
    inp['rwkv_r_k'] = nrm((DEPTH, N_HEADS, HEAD_DIM), 0.1)
    inp['rwkv_ln_w'] = 1.0 + nrm((DEPTH, BRANCH_W), 0.02)
    inp['rwkv_ln_b'] = nrm((DEPTH, BRANCH_W), 0.01)
    inp['sb_q_g'] = 1.0 + nrm((DEPTH, HEAD_DIM), 0.02)
    inp['sb_k_g'] = 1.0 + nrm((DEPTH, HEAD_DIM), 0.02)
    inp['moba_q_g'] = 1.0 + nrm((DEPTH, HEAD_DIM), 0.02)
    inp['moba_k_g'] = 1.0 + nrm((DEPTH, HEAD_DIM), 0.02)
    inp['w_branch'] = nrm((DEPTH, N_BRANCH, BRANCH_W, D_MODEL), BRANCH_W ** -0.5)
    inp['w_out'] = nrm((DEPTH, D_MODEL, D_MODEL), D_MODEL ** -0.5)
    inp['norm2_g'] = 1.0 + nrm((DEPTH, D_MODEL), 0.02)
    inp['ffn_w_up'] = nrm((DEPTH, D_MODEL, 2 * D_FF), D_MODEL ** -0.5)
    inp['ffn_conv_w'] = nrm((DEPTH, CONV_W, 2 * D_FF), 0.5)
    inp['ffn_conv_b'] = nrm((DEPTH, 2 * D_FF), 0.01)
    inp['ffn_w_down'] = nrm((DEPTH, D_FF, D_MODEL), D_FF ** -0.5)
    return inp


def reference(x_prompt, x_sample, state_s5_re, state_s5_im, state_rwkv_wkv, state_rwkv_shift,
              state_ffn_conv, cache_sb_k, cache_sb_v, cache_moba_k, cache_moba_v, page_table,
              norm1_g, w_in, s5_a_re, s5_a_im, s5_log_dt, s5_b_re, s5_b_im, s5_c_re, s5_c_im, s5_d,
              s5_w_glu, rwkv_mu, rwkv_w0, rwkv_w_up, rwkv_a0, rwkv_a_up, rwkv_g_up, rwkv_k_k, rwkv_k_a,
              rwkv_r_k, rwkv_ln_w, rwkv_ln_b, sb_q_g, sb_k_g, moba_q_g, moba_k_g, w_branch, w_out,
              norm2_g, ffn_w_up, ffn_conv_w, ffn_conv_b, ffn_w_down):
    bp = x_prompt.shape[0]
    dt = x_prompt.dtype
    past_len = page_table.shape[1] * cache_sb_k.shape[2]
    yp = x_prompt
    ys = x_sample
    states_p = []
    states_s = []
    for l in range(DEPTH):
        p = {'norm1_g': norm1_g[l], 'w_in': w_in[l], 's5_a_re': s5_a_re[l], 's5_a_im': s5_a_im[l],
             's5_log_dt': s5_log_dt[l], 's5_b_re': s5_b_re[l], 's5_b_im': s5_b_im[l],
             's5_c_re': s5_c_re[l], 's5_c_im': s5_c_im[l], 's5_d': s5_d[l], 's5_w_glu': s5_w_glu[l],
             'rwkv_mu': rwkv_mu[l], 'rwkv_w0': rwkv_w0[l], 'rwkv_w_up': rwkv_w_up[l],
             'rwkv_a0': rwkv_a0[l], 'rwkv_a_up': rwkv_a_up[l], 'rwkv_g_up': rwkv_g_up[l],
             'rwkv_k_k': rwkv_k_k[l], 'rwkv_k_a': rwkv_k_a[l], 'rwkv_r_k': rwkv_r_k[l],
             'rwkv_ln_w': rwkv_ln_w[l], 'rwkv_ln_b': rwkv_ln_b[l], 'sb_q_g': sb_q_g[l],
             'sb_k_g': sb_k_g[l], 'moba_q_g': moba_q_g[l], 'moba_k_g': moba_k_g[l],
             'w_branch': w_branch[l], 'w_out': w_out[l], 'norm2_g': norm2_g[l],
             'ffn_w_up': ffn_w_up[l], 'ffn_conv_w': ffn_conv_w[l], 'ffn_conv_b': ffn_conv_b[l],
             'ffn_w_down': ffn_w_down[l]}
        yp, st_p = trunk_layer(
            yp, 0,
            jnp.zeros((bp, S5_GROUPS, S5_STATE), dt), jnp.zeros((bp, S5_GROUPS, S5_STATE), dt),
            jnp.zeros((bp, RWKV_IN), dt), jnp.zeros((bp, N_HEADS, HEAD_DIM, HEAD_DIM), dt),
            jnp.zeros((bp, CONV_W - 1, 2 * D_FF), dt), None, p)
        past_kv = (gather_pages(cache_sb_k[l], page_table), gather_pages(cache_sb_v[l], page_table),
                   gather_pages(cache_moba_k[l], page_table), gather_pages(cache_moba_v[l], page_table))
        ys, st_s = trunk_layer(ys, past_len, state_s5_re[l], state_s5_im[l], state_rwkv_shift[l],
                               state_rwkv_wkv[l], state_ffn_conv[l], past_kv, p)
        states_p.append(st_p)
        states_s.append(st_s)
    (s5r_p, s5i_p, wkv_p, sh_p, cv_p, sbk_p, sbv_p, mbk_p, mbv_p) = [jnp.stack(z, axis=0) for z in zip(*states_p)]
    (s5r_s, s5i_s, wkv_s, sh_s, cv_s, sbk_s, sbv_s, mbk_s, mbv_s) = [jnp.stack(z, axis=0) for z in zip(*states_s)]
    return (yp, ys, s5r_p, s5r_s, s5i_p, s5i_s, wkv_p, wkv_s, sh_p, sh_s, cv_p, cv_s,
            sbk_p, sbk_s, sbv_p, sbv_s, mbk_p, mbk_s, mbv_p, mbv_s)
```

```python
import functools
import math

import jax
import jax.numpy as jnp
from jax import lax
from jax.experimental import pallas as pl
from jax.experimental.pallas import tpu as pltpu

F32 = jnp.float32
BF16 = jnp.bfloat16

HEAD_DIM = 64
S5_GROUP = 16
S5_STATE = 64
RWKV_DECAY_LORA = 64
RWKV_ICLR_LORA = 64
RWKV_GATE_LORA = 128
N_BRANCH = 4
MOBA_BLOCK = 256
MOBA_TOPK = 3
CONV_W = 3
EPS = 1e-6
RWKV_GN_EPS = 64e-5
RWKV_MAX_CHUNK = 64

V7X_VMEM_LIMIT_BYTES = 56 * 1024 * 1024
SUBLANES = 8


def _cparams(*sem):
    return pltpu.CompilerParams(dimension_semantics=sem, vmem_limit_bytes=V7X_VMEM_LIMIT_BYTES)


def _split_bf16(x):
    hi = x.astype(BF16)
    lo = (x - hi.astype(F32)).astype(BF16)
    return hi, lo


_NN = (((1,), (0,)), ((), ()))
_NT = (((1,), (1,)), ((), ()))


def _dg(a, b, dims):
    return lax.dot_general(a, b, dims, preferred_element_type=F32)


def _mm(a, b, passes=1, dims=_NN):
    if passes == 1:
        return _dg(a.astype(BF16), b.astype(BF16), dims)
    ah, al = _split_bf16(a)
    if b.dtype == BF16:
        return _dg(ah, b, dims) + _dg(al, b, dims)
    bh, bl = _split_bf16(b)
    if passes == 2:
        return _dg(ah, bh, dims) + _dg(al, bh, dims)
    return _dg(ah, bh, dims) + (_dg(al, bh, dims) + _dg(ah, bl, dims))


def _head_block_matrix(width, value):
    r = jnp.arange(width) // HEAD_DIM
    return jnp.where(r[:, None] == r[None, :], value, 0.0).astype(BF16)


def _rms(x, g):
    return x * lax.rsqrt(jnp.mean(x * x, axis=-1, keepdims=True) + EPS) * g


def _proj_plain_kernel(x_ref, g_ref, w_ref, *out_refs, widths):
    h = _rms(x_ref[...], g_ref[...]).astype(BF16)
    u = _dg(h, w_ref[...], _NN)
    off = 0
    for o_ref, n in zip(out_refs, widths):
        o_ref[...] = u[:, off:off + n]
        off += n


def proj_plain(x2d, g, w, widths, tm):
    m, d = x2d.shape
    n = w.shape[1]
    assert sum(widths) == n and m % tm == 0
    return pl.pallas_call(
        functools.partial(_proj_plain_kernel, widths=widths),
        grid=(m // tm,),
        in_specs=[pl.BlockSpec((tm, d), lambda i: (i, 0)),
                  pl.BlockSpec((1, d), lambda i: (0, 0)),
                  pl.BlockSpec((d, n), lambda i: (0, 0))],
        out_specs=[pl.BlockSpec((tm, k), lambda i: (i, 0)) for k in widths],
        out_shape=[jax.ShapeDtypeStruct((m, k), F32) for k in widths],
        compiler_params=_cparams("parallel"),
        name="proj_plain",
    )(x2d, g.reshape(1, d), w)


def _head_rms(x, hm, g):
    ms = _mm(x * x, hm, passes=2)
    return x * lax.rsqrt(ms + EPS) * g


def _proj_qkv_kernel(x_ref, g_ref, w_ref, hm_ref, gq_ref, gk_ref, q_ref, k_ref, v_ref, *, width, q_scale):
    h = _rms(x_ref[...], g_ref[...]).astype(BF16)
    u = _dg(h, w_ref[...], _NN)
    hm = hm_ref[...]
    q_ref[...] = _head_rms(u[:, :width], hm, gq_ref[...]) * q_scale
    k_ref[...] = _head_rms(u[:, width:2 * width], hm, gk_ref[...])
    v_ref[...] = u[:, 2 * width:]


def proj_qkv(x2d, g, w, g_q, g_k, tm):
    m, d = x2d.shape
    width = w.shape[1] // 3
    n_heads = width // HEAD_DIM
    hm = _head_block_matrix(width, 1.0 / HEAD_DIM)
    row = lambda i: (i, 0)
    fixed = lambda i: (0, 0)
    return pl.pallas_call(
        functools.partial(_proj_qkv_kernel, width=width, q_scale=HEAD_DIM ** -0.5),
        grid=(m // tm,),
        in_specs=[pl.BlockSpec((tm, d), row), pl.BlockSpec((1, d), fixed),
                  pl.BlockSpec((d, 3 * width), fixed), pl.BlockSpec((width, width), fixed),
                  pl.BlockSpec((1, width), fixed), pl.BlockSpec((1, width), fixed)],
        out_specs=[pl.BlockSpec((tm, width), row)] * 3,
        out_shape=[jax.ShapeDtypeStruct((m, width), F32)] * 3,
        compiler_params=_cparams("parallel"),
        name="proj_qkv",
    )(x2d, g.reshape(1, d), w, hm, jnp.tile(g_q, n_heads).reshape(1, width),
      jnp.tile(g_k, n_heads).reshape(1, width))


def _s5_disc_kernel(are_ref, aim_ref, ldt_ref, bre_ref, bim_ref, abre_ref, abim_ref, bbre_ref, bbim_ref):
    lr = are_ref[...]
    li = aim_ref[...]
    dt = jnp.exp(ldt_ref[...])
    mag = jnp.exp(lr * dt)
    ab_re = mag * jnp.cos(li * dt)
    ab_im = mag * jnp.sin(li * dt)
    den = lr * lr + li * li
    zr = ((ab_re - 1.0) * lr + ab_im * li) / den
    zi = (ab_im * lr - (ab_re - 1.0) * li) / den
    abre_ref[...] = ab_re
    abim_ref[...] = ab_im
    br = bre_ref[...]
    bi = bim_ref[...]
    bbre_ref[...] = zr[:, None, :] * br - zi[:, None, :] * bi
    bbim_ref[...] = zr[:, None, :] * bi + zi[:, None, :] * br


def s5_discretize(a_re, a_im, log_dt, b_re, b_im):
    g, p = a_re.shape
    gc = b_re.shape[1]
    return pl.pallas_call(
        _s5_disc_kernel,
        out_shape=[jax.ShapeDtypeStruct((g, p), F32)] * 2 + [jax.ShapeDtypeStruct((g, gc, p), F32)] * 2,
        name="s5_discretize",
    )(a_re, a_im, log_dt.reshape(g, 1), b_re, b_im)


def _gelu_tanh(x):
    c = math.sqrt(2.0 / math.pi)
    return 0.5 * x * (1.0 + jnp.tanh(c * (x + 0.044715 * (x * x * x))))


def _s5_kernel(u_ref, s0re_ref, s0im_ref, abre_ref, abim_ref, bb_ref, cc_ref, d_ref, wglu_ref,
               y_ref, sre_ref, sim_ref, xs_ref, cre_ref, cim_ref, *, tt, n_state, passes, lane_chunk):
    ti = pl.program_id(1)

    @pl.when(ti == 0)
    def _():
        cre_ref[...] = s0re_ref[...]
        cim_ref[...] = s0im_ref[...]

    width = u_ref.shape[-1]
    u = u_ref[...].reshape(tt * SUBLANES, width)
    xs_ref[...] = _mm(u, bb_ref[...], passes)

    for c0 in range(0, n_state, lane_chunk):
        re_sl = pl.ds(c0, lane_chunk)
        im_sl = pl.ds(n_state + c0, lane_chunk)
        a_re = jnp.broadcast_to(abre_ref[:, re_sl], (SUBLANES, lane_chunk))
        a_im = jnp.broadcast_to(abim_ref[:, re_sl], (SUBLANES, lane_chunk))

        def step(t, carry):
            s_re, s_im = carry
            rows = pl.ds(pl.multiple_of(t * SUBLANES, SUBLANES), SUBLANES)
            n_re = a_re * s_re - a_im * s_im + xs_ref[rows, re_sl]
            n_im = a_re * s_im + a_im * s_re + xs_ref[rows, im_sl]
            xs_ref[rows, re_sl] = n_re
            xs_ref[rows, im_sl] = n_im
            return n_re, n_im

        s_re, s_im = lax.fori_loop(0, tt, step, (cre_ref[:, re_sl], cim_ref[:, re_sl]), unroll=8)
        cre_ref[:, re_sl] = s_re
        cim_ref[:, re_sl] = s_im

    y = _mm(xs_ref[...], cc_ref[...], passes) + d_ref[...] * u
    y = _gelu_tanh(y)
    y = y * jax.nn.sigmoid(_mm(y, wglu_ref[...]))
    y_ref[...] = y.reshape(tt, SUBLANES, width)
    sre_ref[...] = cre_ref[...]
    sim_ref[...] = cim_ref[...]


def s5_mixer(u_tm, s0_re, s0_im, ab_re, ab_im, bb, cc, d_skip, w_glu, tt, passes):
    t, b, width = u_tm.shape
    n_state = ab_re.shape[-1]
    assert t % tt == 0 and b % SUBLANES == 0
    fixed = lambda bi, ti: (0, 0)
    st = lambda bi, ti: (bi, 0)
    return pl.pallas_call(
        functools.partial(_s5_kernel, tt=tt, n_state=n_state, passes=passes, lane_chunk=512),
        grid=(b // SUBLANES, t // tt),
        in_specs=[pl.BlockSpec((tt, SUBLANES, width), lambda bi, ti: (ti, bi, 0)),
                  pl.BlockSpec((SUBLANES, n_state), st), pl.BlockSpec((SUBLANES, n_state), st),
                  pl.BlockSpec((1, n_state), fixed), pl.BlockSpec((1, n_state), fixed),
                  pl.BlockSpec((width, 2 * n_state), fixed), pl.BlockSpec((2 * n_state, width), fixed),
                  pl.BlockSpec((1, width), fixed), pl.BlockSpec((width, width), fixed)],
        out_specs=[pl.BlockSpec((tt, SUBLANES, width), lambda bi, ti: (ti, bi, 0)),
                   pl.BlockSpec((SUBLANES, n_state), st), pl.BlockSpec((SUBLANES, n_state), st)],
        out_shape=[jax.ShapeDtypeStruct((t, b, width), F32),
                   jax.ShapeDtypeStruct((b, n_state), F32), jax.ShapeDtypeStruct((b, n_state), F32)],
        scratch_shapes=[pltpu.VMEM((tt * SUBLANES, 2 * n_state), F32),
                        pltpu.VMEM((SUBLANES, n_state), F32), pltpu.VMEM((SUBLANES, n_state), F32)],
        compiler_params=_cparams("parallel", "arbitrary"),
        name="s5_mixer",
    )(u_tm, s0_re, s0_im, ab_re.reshape(1, n_state), ab_im.reshape(1, n_state), bb, cc,
      d_skip.reshape(1, width), w_glu)


def s5_operands(a_re, a_im, log_dt, b_re, b_im, c_re, c_im):
    g, p = a_re.shape
    gc = b_re.shape[-1]
    ab_re, ab_im, bb_re, bb_im = s5_discretize(a_re, a_im, log_dt, b_re.transpose(0, 2, 1),
                                               b_im.transpose(0, 2, 1))
    eye = jnp.eye(g, dtype=F32)
    bd = lambda m: jnp.einsum('gcp,gh->gchp', m, eye).reshape(g * gc, g * p)
    bb = jnp.concatenate([bd(bb_re), bd(bb_im)], axis=1)
    bdc = lambda m: jnp.einsum('gcp,gh->hpgc', m, eye).reshape(g * p, g * gc)
    cc = jnp.concatenate([bdc(c_re), -bdc(c_im)], axis=0)
    return ab_re.reshape(g * p), ab_im.reshape(g * p), bb, cc


def _softplus(x):
    return jnp.maximum(x, 0.0) + jnp.log1p(jnp.exp(-jnp.abs(x)))


def _rwkv_kernel(u_ref, shift0_ref, wkv0_ref, mu_ref, w0_ref, wup_ref, a0_ref, aup_ref, gup_ref, kk_ref,
                 ka_ref, rk_ref, lnw_ref, lnb_ref, hsum_ref, y_ref, wkv_ref, shift_ref,
                 prev_ref, s_ref, yt_ref, *, tt, chunk, width, passes, t_valid):
    ti = pl.program_id(1)
    n_heads = width // HEAD_DIM

    @pl.when(ti == 0)
    def _():
        prev_ref[...] = shift0_ref[...]
        s_ref[...] = wkv0_ref[...]

    u = u_ref[...]
    row = lax.broadcasted_iota(jnp.int32, (tt, 1), 0)
    prev = jnp.where(row == 0, prev_ref[...], pltpu.roll(u, 1, axis=0))
    xs = u + mu_ref[...] * (prev - u)
    prev_ref[...] = u[t_valid - 1:t_valid, :]
    shift_ref[...] = u[t_valid - 1:t_valid, :]

    w3 = 3 * width
    o1 = w3 + RWKV_DECAY_LORA
    o2 = o1 + RWKV_ICLR_LORA
    r = xs[:, :width]
    k = xs[:, width:2 * width]
    v = xs[:, 2 * width:w3]
    w_log = -_softplus(-(w0_ref[...] + _mm(jnp.tanh(xs[:, w3:o1]), wup_ref[...], 3))) - 0.5
    log_decay = -jnp.exp(w_log)
    a = jax.nn.sigmoid(a0_ref[...] + _mm(xs[:, o1:o2], aup_ref[...], 3))
    g = _mm(jax.nn.sigmoid(xs[:, o2:]), gup_ref[...], 3)
    hsum = hsum_ref[...]
    kk = k * kk_ref[...]
    kk = kk / jnp.maximum(jnp.sqrt(_mm(kk * kk, hsum, 2)), 1e-12)
    k2 = k * (1.0 + (a - 1.0) * ka_ref[...])
    a_vec = -kk
    b_vec = kk * a
    if t_valid < tt:
        live = row < t_valid
        log_decay = jnp.where(live, log_decay, 0.0)
        a_vec = jnp.where(live, a_vec, 0.0)
        b_vec = jnp.where(live, b_vec, 0.0)
        k2 = jnp.where(live, k2, 0.0)
    vt = v.T

    ri = lax.broadcasted_iota(jnp.int32, (chunk, chunk), 0)
    ci = lax.broadcasted_iota(jnp.int32, (chunk, chunk), 1)
    upper_strict = ri < ci
    upper_incl = ri <= ci
    lower_incl = (ri >= ci).astype(F32)
    n_double = (min(chunk, t_valid) - 1).bit_length()

    for c in range(tt // chunk):
        rows = slice(c * chunk, (c + 1) * chunk)
        ld = log_decay[rows]
        cl = _mm(lower_incl, ld, 3)
        e_incl = jnp.exp(cl)
        e_inv = jnp.exp(-cl)
        rt = r[rows] * e_incl
        at = a_vec[rows] * jnp.exp(cl - ld)
        bt = b_vec[rows] * e_inv
        kt = k2[rows] * e_inv
        p_end = e_incl[chunk - 1:chunk, :]
        for h in range(n_heads):
            hs = slice(h * HEAD_DIM, (h + 1) * HEAD_DIM)
            bk = jnp.concatenate([bt[:, hs], kt[:, hs]], axis=0)
            ar = jnp.concatenate([at[:, hs], rt[:, hs]], axis=0)
            gram = _mm(bk, ar, passes, _NT)
            n_t = jnp.where(upper_strict, gram[:chunk, :chunk], 0.0)
            m2_t = jnp.where(upper_strict, gram[chunk:, :chunk], 0.0)
            m34_t = jnp.concatenate([jnp.where(upper_incl, gram[:chunk, chunk:], 0.0),
                                     jnp.where(upper_incl, gram[chunk:, chunk:], 0.0)], axis=0)
            s0 = s_ref[h]
            sa = _mm(s0, ar, passes, _NT)
            vt_h = vt[hs, rows]
            x = sa[:, :chunk] + _mm(vt_h, m2_t, passes)
            nk = n_t
            for d in range(n_double):
                x = x + _mm(x, nk, passes)
                if d + 1 < n_double:
                    nk = _mm(nk, nk, passes)
            uv = jnp.concatenate([x, vt_h], axis=1)
            yt_ref[hs, rows] = sa[:, chunk:] + _mm(uv, m34_t, passes)
            s_ref[h] = (s0 + _mm(uv, bk, passes)) * p_end[:, hs]

    y = yt_ref[...].T
    hmean = hsum * (1.0 / HEAD_DIM)
    yc = y - _mm(y, hmean, 2)
    var = _mm(yc * yc, hmean, 2)
    yn = yc * lax.rsqrt(var + RWKV_GN_EPS) * lnw_ref[...] + lnb_ref[...]
    bonus = _mm(r * k2 * rk_ref[...], hsum, 2) * v
    y_ref[...] = (yn + bonus) * g
    wkv_ref[...] = s_ref[...]


def rwkv_mixer(u, shift0, wkv0, p, tt, chunk, passes):
    b, t_valid, n_in = u.shape
    n_heads = wkv0.shape[1]
    width = n_heads * HEAD_DIM
    if t_valid < tt:
        assert t_valid <= RWKV_MAX_CHUNK
        u = _pad_rows(u, tt)
        chunk = tt
    else:
        assert chunk <= RWKV_MAX_CHUNK
        t_valid = tt
    t = u.shape[1]
    assert t % tt == 0 and tt % chunk == 0
    fixed = lambda bi, ti: (0, 0)
    vec = lambda n: pl.BlockSpec((1, n), fixed)
    args = [p['rwkv_mu'].reshape(1, n_in), p['rwkv_w0'].reshape(1, width), p['rwkv_w_up'],
            p['rwkv_a0'].reshape(1, width), p['rwkv_a_up'], p['rwkv_g_up'], p['rwkv_k_k'].reshape(1, width),
            p['rwkv_k_a'].reshape(1, width), p['rwkv_r_k'].reshape(1, width), p['rwkv_ln_w'].reshape(1, width),
            p['rwkv_ln_b'].reshape(1, width), _head_block_matrix(width, 1.0)]
    specs = [vec(n_in), vec(width), pl.BlockSpec(p['rwkv_w_up'].shape, fixed), vec(width),
             pl.BlockSpec(p['rwkv_a_up'].shape, fixed), pl.BlockSpec(p['rwkv_g_up'].shape, fixed),
             vec(width), vec(width), vec(width), vec(width), vec(width), pl.BlockSpec((width, width), fixed)]
    y, wkv, shift = pl.pallas_call(
        functools.partial(_rwkv_kernel, tt=tt, chunk=chunk, width=width, passes=passes, t_valid=t_valid),
        grid=(b, t // tt),
        in_specs=[pl.BlockSpec((None, tt, n_in), lambda bi, ti: (bi, ti, 0)),
                  pl.BlockSpec((None, 1, n_in), lambda bi, ti: (bi, 0, 0)),
                  pl.BlockSpec((None, n_heads, HEAD_DIM, HEAD_DIM), lambda bi, ti: (bi, 0, 0, 0))] + specs,
        out_specs=[pl.BlockSpec((None, tt, width), lambda bi, ti: (bi, ti, 0)),
                   pl.BlockSpec((None, n_heads, HEAD_DIM, HEAD_DIM), lambda bi, ti: (bi, 0, 0, 0)),
                   pl.BlockSpec((None, 1, n_in), lambda bi, ti: (bi, 0, 0))],
        out_shape=[jax.ShapeDtypeStruct((b, t, width), F32),
                   jax.ShapeDtypeStruct((b, n_heads, HEAD_DIM, HEAD_DIM), F32),
                   jax.ShapeDtypeStruct((b, 1, n_in), F32)],
        scratch_shapes=[pltpu.VMEM((1, n_in), F32), pltpu.VMEM((n_heads, HEAD_DIM, HEAD_DIM), F32),
                        pltpu.VMEM((width, tt), F32)],
        compiler_params=_cparams("parallel", "arbitrary"),
        name="rwkv_mixer",
    )(u, shift0.reshape(b, 1, n_in), wkv0, *args)
    return y[:, :t_valid] if t_valid < tt else y, wkv, shift.reshape(b, n_in)


NEG_BIG = -1e30


def _sb_prompt_kernel(q_ref, k_ref, v_ref, o_ref, *, tq, tk, width):
    qi = pl.program_id(1)
    n_heads = width // HEAD_DIM
    row = lax.broadcasted_iota(jnp.int32, (tq, tk), 0)
    col = lax.broadcasted_iota(jnp.int32, (tq, tk), 1)
    ki = lax.broadcasted_iota(jnp.int32, (tk, tk), 0)
    kj = lax.broadcasted_iota(jnp.int32, (tk, tk), 1)
    later = jnp.where(ki > kj, 1.0, 0.0).astype(BF16)
    n_kblocks = (qi * tq + tq + tk - 1) // tk
    for h in range(n_heads):
        hs = slice(h * HEAD_DIM, (h + 1) * HEAD_DIM)
        q_h = q_ref[:, hs].astype(BF16)

        def body(jj, carry):
            acc, run = carry
            j = n_kblocks - 1 - jj
            ks = pl.ds(pl.multiple_of(j * tk, tk), tk)
            z = _dg(q_h, k_ref[ks, hs].astype(BF16), _NT)
            sp = _softplus(z)
            mask = (j * tk + col) < (qi * tq + row)
            l1m = jnp.where(mask, -sp, 0.0)
            rest = _mm(l1m, later, 2) + run
            w = jnp.where(mask, jnp.exp(z - sp + rest), 0.0)
            acc = acc + _mm(w, v_ref[ks, hs])
            return acc, rest[:, 0:1] + l1m[:, 0:1]

        acc, _ = lax.fori_loop(0, n_kblocks, body,
                               (jnp.zeros((tq, HEAD_DIM), F32), jnp.zeros((tq, 1), F32)))
        o_ref[:, hs] = acc


def sb_prompt(q, k, v, tq, tk):
    b, t, width = q.shape
    tq, tk = min(tq, t), min(tk, t)
    assert t % tq == 0 and t % tk == 0
    full = pl.BlockSpec((None, t, width), lambda bi, qi: (bi, 0, 0))
    tile = pl.BlockSpec((None, tq, width), lambda bi, qi: (bi, qi, 0))
    return pl.pallas_call(
        functools.partial(_sb_prompt_kernel, tq=tq, tk=tk, width=width),
        grid=(b, t // tq),
        in_specs=[tile, full, full],
        out_specs=tile,
        out_shape=jax.ShapeDtypeStruct((b, t, width), F32),
        compiler_params=_cparams("parallel", "arbitrary"),
        name="sb_prompt",
    )(q, k, v)


def _topk_block_mask(gate, n_valid, topk):
    nb = gate.shape[1]
    col = lax.broadcasted_iota(jnp.int32, gate.shape, 1)
    rank = jnp.zeros(gate.shape, jnp.int32)
    for m in range(nb):
        gm = gate[:, m:m + 1]
        beats = (gm > gate) | ((gm == gate) & (m < col))
        rank = rank + jnp.where(beats & (m < n_valid), 1, 0)
    return (col < n_valid) & (rank < topk)


def _moba_prompt_kernel(q_ref, k_ref, v_ref, o_ref, *, tq, width, nb):
    qi = pl.program_id(1)
    n_heads = width // HEAD_DIM
    row = lax.broadcasted_iota(jnp.int32, (tq, tq), 0)
    col = lax.broadcasted_iota(jnp.int32, (tq, tq), 1)
    causal = jnp.where(col <= row, 1.0, 0.0)
    blk_col = lax.broadcasted_iota(jnp.int32, (tq, nb), 1)
    kmean = jnp.concatenate(
        [jnp.sum(k_ref[n * tq:(n + 1) * tq, :], axis=0, keepdims=True) for n in range(nb)], axis=0) * (1.0 / tq)
    for h in range(n_heads):
        hs = slice(h * HEAD_DIM, (h + 1) * HEAD_DIM)
        q_f = q_ref[:, hs]
        gate = _mm(q_f, kmean[:, hs], 3, _NT)
        sel = jnp.where(_topk_block_mask(gate, qi, MOBA_TOPK), 1.0, 0.0)
        q_h = q_f.astype(BF16)

        def body(n, carry):
            acc, m_run, l_run = carry
            ks = pl.ds(pl.multiple_of(n * tq, tq), tq)
            s = _dg(q_h, k_ref[ks, hs].astype(BF16), _NT)
            picked = jnp.sum(jnp.where(blk_col == n, sel, 0.0), axis=1, keepdims=True)
            mask = jnp.where(n == qi, causal, picked) > 0.5
            s = jnp.where(mask, s, NEG_BIG)
            m_new = jnp.maximum(m_run, jnp.max(s, axis=1, keepdims=True))
            p = jnp.where(mask, jnp.exp(s - m_new), 0.0)
            alpha = jnp.exp(m_run - m_new)
            l_new = alpha * l_run + jnp.sum(p, axis=1, keepdims=True)
            acc = alpha * acc + _mm(p, v_ref[ks, hs])
            return acc, m_new, l_new

        acc, _, l_run = lax.fori_loop(0, qi + 1, body,
                                      (jnp.zeros((tq, HEAD_DIM), F32), jnp.full((tq, 1), NEG_BIG, F32),
                                       jnp.zeros((tq, 1), F32)))
        o_ref[:, hs] = acc / l_run


def moba_prompt(q, k, v):
    b, t, width = q.shape
    tq = MOBA_BLOCK
    assert t % tq == 0
    full = pl.BlockSpec((None, t, width), lambda bi, qi: (bi, 0, 0))
    tile = pl.BlockSpec((None, tq, width), lambda bi, qi: (bi, qi, 0))
    return pl.pallas_call(
        functools.partial(_moba_prompt_kernel, tq=tq, width=width, nb=t // tq),
        grid=(b, t // tq),
        in_specs=[tile, full, full],
        out_specs=tile,
        out_shape=jax.ShapeDtypeStruct((b, t, width), F32),
        compiler_params=_cparams("parallel", "arbitrary"),
        name="moba_prompt",
    )(q, k, v)


def _stack_heads(q, n_heads):
    t, width = q.shape
    rows = lax.broadcasted_iota(jnp.int32, (n_heads * t, width), 0) // t
    lanes = lax.broadcasted_iota(jnp.int32, (n_heads * t, width), 1) // HEAD_DIM
    return jnp.where(rows == lanes, jnp.concatenate([q] * n_heads, axis=0), 0.0)


def _unstack_heads(acc, n_heads):
    rows_total, width = acc.shape
    t = rows_total // n_heads
    rows = lax.broadcasted_iota(jnp.int32, (rows_total, width), 0) // t
    lanes = lax.broadcasted_iota(jnp.int32, (rows_total, width), 1) // HEAD_DIM
    kept = jnp.where(rows == lanes, acc, 0.0).reshape(n_heads, t, width)
    return jnp.sum(kept, axis=0)


def _later_matrix(tk):
    ki = lax.broadcasted_iota(jnp.int32, (tk, tk), 0)
    kj = lax.broadcasted_iota(jnp.int32, (tk, tk), 1)
    return jnp.where(ki > kj, 1.0, 0.0).astype(BF16)


def _sb_sample_kernel(pt_ref, q_ref, knew_ref, vnew_ref, *refs, n_pages_step, t_new, n_heads):
    k_refs = refs[:n_pages_step]
    v_refs = refs[n_pages_step:2 * n_pages_step]
    o_ref, qs_ref, acc_ref, run_ref = refs[2 * n_pages_step:]
    step = pl.program_id(1)
    tk = knew_ref.shape[0]
    later = _later_matrix(tk)

    def tile(k_tile, v_tile, mask):
        z = _dg(qs_ref[...], k_tile.astype(BF16), _NT)
        sp = _softplus(z)
        l1m = -sp if mask is None else jnp.where(mask, -sp, 0.0)
        rest = _mm(l1m, later, 2) + run_ref[...]
        w = jnp.exp(z - sp + rest)
        if mask is not None:
            w = jnp.where(mask, w, 0.0)
        acc_ref[...] += _mm(w, v_tile)
        run_ref[...] = rest[:, 0:1] + l1m[:, 0:1]

    @pl.when(step == 0)
    def _():
        qs_ref[...] = _stack_heads(q_ref[...], n_heads).astype(BF16)
        acc_ref[...] = jnp.zeros_like(acc_ref)
        run_ref[...] = jnp.zeros_like(run_ref)
        rows = lax.broadcasted_iota(jnp.int32, (n_heads * t_new, tk), 0) % t_new
        cols = lax.broadcasted_iota(jnp.int32, (n_heads * t_new, tk), 1)
        tile(knew_ref[...], vnew_ref[...], cols < rows)

    for i in range(n_pages_step):
        tile(k_refs[i][...], v_refs[i][...], None)

    @pl.when(step == pl.num_programs(1) - 1)
    def _():
        o_ref[...] = _unstack_heads(acc_ref[...], n_heads)


def _page_specs(n_pages, n_pages_step, width, page_size, reverse):
    def spec(i):
        def index(bi, si, pt):
            p = si * n_pages_step + i
            return (pt[bi, n_pages - 1 - p if reverse else p], 0, 0)
        return pl.BlockSpec((None, page_size, width), index)
    return [spec(i) for i in range(n_pages_step)]


def _pad_rows(x, rows):
    return jnp.pad(x, ((0, 0), (0, rows - x.shape[1]), (0, 0)))


def sb_sample(q, k_new, v_new, k_pool, v_pool, pages, n_pages_step):
    b, t_new, width = q.shape
    n_heads = width // HEAD_DIM
    page_size = k_pool.shape[1]
    n_pages = pages.shape[1]
    assert n_pages % n_pages_step == 0 and t_new <= page_size
    new = pl.BlockSpec((None, t_new, width), lambda bi, si, pt: (bi, 0, 0))
    new_pad = pl.BlockSpec((None, page_size, width), lambda bi, si, pt: (bi, 0, 0))
    page_specs = _page_specs(n_pages, n_pages_step, width, page_size, reverse=True)
    return pl.pallas_call(
        functools.partial(_sb_sample_kernel, n_pages_step=n_pages_step, t_new=t_new, n_heads=n_heads),
        grid_spec=pltpu.PrefetchScalarGridSpec(
            num_scalar_prefetch=1, grid=(b, n_pages // n_pages_step),
            in_specs=[new, new_pad, new_pad] + page_specs + page_specs,
            out_specs=new,
            scratch_shapes=[pltpu.VMEM((n_heads * t_new, width), BF16),
                            pltpu.VMEM((n_heads * t_new, width), F32),
                            pltpu.VMEM((n_heads * t_new, 1), F32)]),
        out_shape=jax.ShapeDtypeStruct((b, t_new, width), F32),
        compiler_params=_cparams("parallel", "arbitrary"),
        name="sb_sample",
    )(pages, q, _pad_rows(k_new, page_size), _pad_rows(v_new, page_size),
      *([k_pool] * n_pages_step), *([v_pool] * n_pages_step))


def _block_mean_kernel(pt_ref, *refs, n_pages_step, pages_per_block):
    k_refs = refs[:n_pages_step]
    o_ref = refs[n_pages_step]
    scale = 1.0 / (pages_per_block * k_refs[0].shape[0])
    means = []
    for n in range(n_pages_step // pages_per_block):
        tot = None
        for i in range(pages_per_block):
            part = jnp.sum(k_refs[n * pages_per_block + i][...], axis=0, keepdims=True)
            tot = part if tot is None else tot + part
        means.append(tot * scale)
    o_ref[...] = jnp.concatenate(means, axis=0)


def paged_block_means(k_pool, pages, block):
    b, n_pages = pages.shape
    page_size, width = k_pool.shape[1:]
    pages_per_block = block // page_size
    n_pages_step = SUBLANES * pages_per_block
    assert block % page_size == 0 and n_pages % n_pages_step == 0
    return pl.pallas_call(
        functools.partial(_block_mean_kernel, n_pages_step=n_pages_step, pages_per_block=pages_per_block),
        grid_spec=pltpu.PrefetchScalarGridSpec(
            num_scalar_prefetch=1, grid=(b, n_pages // n_pages_step),
            in_specs=_page_specs(n_pages, n_pages_step, width, page_size, reverse=False),
            out_specs=pl.BlockSpec((None, SUBLANES, width), lambda bi, si, pt: (bi, si, 0))),
        out_shape=jax.ShapeDtypeStruct((b, n_pages // pages_per_block, width), F32),
        compiler_params=_cparams("parallel", "arbitrary"),
        name="paged_block_means",
    )(pages, *([k_pool] * n_pages_step))


def _moba_sample_kernel(pt_ref, q_ref, kmean_ref, knew_ref, vnew_ref, *refs, n_pages_step, pages_per_block,
                        t_new, n_heads):
    k_refs = refs[:n_pages_step]
    v_refs = refs[n_pages_step:2 * n_pages_step]
    o_ref, qs_ref, sel_ref, acc_ref, m_ref, l_ref = refs[2 * n_pages_step:]
    step = pl.program_id(1)
    nb = kmean_ref.shape[0]
    rows_total = n_heads * t_new
    blk_col = lax.broadcasted_iota(jnp.int32, (rows_total, nb), 1)

    def tile(k_tile, v_tile, mask):
        s = jnp.where(mask, _dg(qs_ref[...], k_tile.astype(BF16), _NT), NEG_BIG)
        m_new = jnp.maximum(m_ref[...], jnp.max(s, axis=1, keepdims=True))
        p = jnp.where(mask, jnp.exp(s - m_new), 0.0)
        alpha = jnp.exp(m_ref[...] - m_new)
        l_ref[...] = alpha * l_ref[...] + jnp.sum(p, axis=1, keepdims=True)
        acc_ref[...] = alpha * acc_ref[...] + _mm(p, v_tile)
        m_ref[...] = m_new

    @pl.when(step == 0)
    def _():
        qs = _stack_heads(q_ref[...], n_heads)
        qs_ref[...] = qs.astype(BF16)
        gate = _mm(qs, kmean_ref[...], 3, _NT)
        sel_ref[...] = jnp.where(_topk_block_mask(gate, nb, MOBA_TOPK), 1.0, 0.0)
        acc_ref[...] = jnp.zeros_like(acc_ref)
        l_ref[...] = jnp.zeros_like(l_ref)
        m_ref[...] = jnp.full(m_ref.shape, NEG_BIG, F32)
        tk = knew_ref.shape[0]
        rows = lax.broadcasted_iota(jnp.int32, (rows_total, tk), 0) % t_new
        cols = lax.broadcasted_iota(jnp.int32, (rows_total, tk), 1)
        tile(knew_ref[...], vnew_ref[...], cols <= rows)

    for i in range(n_pages_step):
        blk = (step * n_pages_step + i) // pages_per_block
        picked = jnp.sum(jnp.where(blk_col == blk, sel_ref[...], 0.0), axis=1, keepdims=True) > 0.5
        tile(k_refs[i][...], v_refs[i][...], jnp.broadcast_to(picked, (rows_total, k_refs[i].shape[0])))

    @pl.when(step == pl.num_programs(1) - 1)
    def _():
        o_ref[...] = _unstack_heads(acc_ref[...] / l_ref[...], n_heads)


def moba_sample(q, k_new, v_new, k_pool, v_pool, pages, kmean, n_pages_step):
    b, t_new, width = q.shape
    n_heads = width // HEAD_DIM
    page_size = k_pool.shape[1]
    n_pages = pages.shape[1]
    nb = kmean.shape[1]
    pages_per_block = MOBA_BLOCK // page_size
    assert n_pages % n_pages_step == 0 and t_new <= page_size and n_pages == nb * pages_per_block
    assert (n_pages * page_size) % MOBA_BLOCK == 0 and t_new <= MOBA_BLOCK
    new = pl.BlockSpec((None, t_new, width), lambda bi, si, pt: (bi, 0, 0))
    new_pad = pl.BlockSpec((None, page_size, width), lambda bi, si, pt: (bi, 0, 0))
    page_specs = _page_specs(n_pages, n_pages_step, width, page_size, reverse=False)
    rows_total = n_heads * t_new
    return pl.pallas_call(
        functools.partial(_moba_sample_kernel, n_pages_step=n_pages_step, pages_per_block=pages_per_block,
                          t_new=t_new, n_heads=n_heads),
        grid_spec=pltpu.PrefetchScalarGridSpec(
            num_scalar_prefetch=1, grid=(b, n_pages // n_pages_step),
            in_specs=[new, pl.BlockSpec((None, nb, width), lambda bi, si, pt: (bi, 0, 0)), new_pad, new_pad]
                     + page_specs + page_specs,
            out_specs=new,
            scratch_shapes=[pltpu.VMEM((rows_total, width), BF16), pltpu.VMEM((rows_total, nb), F32),
                            pltpu.VMEM((rows_total, width), F32), pltpu.VMEM((rows_total, 1), F32),
                            pltpu.VMEM((rows_total, 1), F32)]),
        out_shape=jax.ShapeDtypeStruct((b, t_new, width), F32),
        compiler_params=_cparams("parallel", "arbitrary"),
        name="moba_sample",
    )(pages, q, kmean, _pad_rows(k_new, page_size), _pad_rows(v_new, page_size),
      *([k_pool] * n_pages_step), *([v_pool] * n_pages_step))


def _merge_kernel(x_ref, g_ref, wg_ref, y0_ref, y1_ref, y2_ref, y3_ref, wb_ref, wo_ref, o_ref, *, d_model):
    x = x_ref[...]
    h = _rms(x, g_ref[...]).astype(BF16)
    merged = None
    for n, y_ref in enumerate((y0_ref, y1_ref, y2_ref, y3_ref)):
        gate = jax.nn.sigmoid(_dg(h, wg_ref[:, n * d_model:(n + 1) * d_model], _NN))
        term = gate * _mm(y_ref[...], wb_ref[n])
        merged = term if merged is None else merged + term
    o_ref[...] = x + _mm(merged, wo_ref[...])


def merge(x2d, g, w_gate, ys, w_branch, w_out, tm):
    m, d = x2d.shape
    width = ys[0].shape[1]
    tm = min(tm, m)
    assert m % tm == 0
    row = lambda i: (i, 0)
    fixed = lambda i: (0, 0)
    return pl.pallas_call(
        functools.partial(_merge_kernel, d_model=d),
        grid=(m // tm,),
        in_specs=[pl.BlockSpec((tm, d), row), pl.BlockSpec((1, d), fixed),
                  pl.BlockSpec((d, N_BRANCH * d), fixed)]
                 + [pl.BlockSpec((tm, width), row)] * N_BRANCH
                 + [pl.BlockSpec((N_BRANCH, width, d), lambda i: (0, 0, 0)), pl.BlockSpec((d, d), fixed)],
        out_specs=pl.BlockSpec((tm, d), row),
        out_shape=jax.ShapeDtypeStruct((m, d), F32),
        compiler_params=_cparams("parallel"),
        name="merge",
    )(x2d, g.reshape(1, d), w_gate, *ys, w_branch, w_out)


def _ffn_kernel(x_ref, g_ref, wup_ref, cw_ref, cb_ref, wdn_ref, conv0_ref, o_ref, conv_ref, st_ref,
                *, nseq, tt, d_ff, col_chunk):
    ti = pl.program_id(1)

    @pl.when(ti == 0)
    def _():
        st_ref[...] = conv0_ref[...]

    d_model = x_ref.shape[-1]
    x = x_ref[...].reshape(nseq * tt, d_model)
    h = _rms(x, g_ref[...]).astype(BF16)
    t_idx = lax.broadcasted_iota(jnp.int32, (1, tt, 1), 1)

    def conv_cols(c0):
        cols = slice(c0, c0 + col_chunk)
        up = _dg(h, wup_ref[:, cols], _NN).reshape(nseq, tt, col_chunk)
        st = st_ref[:, :, cols]
        p1 = jnp.where(t_idx == 0, st[:, 1:2, :], pltpu.roll(up, 1, axis=1))
        p2 = jnp.where(t_idx == 0, st[:, 0:1, :], jnp.where(t_idx == 1, st[:, 1:2, :], pltpu.roll(up, 2, axis=1)))
        st_ref[:, :, cols] = up[:, tt - 2:, :]
        cw = cw_ref[:, cols]
        c = cb_ref[:, cols] + cw[2:3, :] * up + cw[1:2, :] * p1 + cw[0:1, :] * p2
        return c.reshape(nseq * tt, col_chunk)

    acc = x
    for c0 in range(0, d_ff, col_chunk):
        a = conv_cols(c0)
        b = conv_cols(d_ff + c0)
        acc = acc + _mm(a * jax.nn.sigmoid(a) * b, wdn_ref[c0:c0 + col_chunk, :])
    o_ref[...] = acc.reshape(nseq, tt, d_model)
    conv_ref[...] = st_ref[...]


def conv_ffn(x, g, w_up, conv_w, conv_b, w_down, conv0, nseq, tt, col_chunk):
    b, t, d = x.shape
    d_ff = w_down.shape[0]
    assert b % nseq == 0 and t % tt == 0 and d_ff % col_chunk == 0 and tt >= CONV_W - 1
    fixed = lambda bi, ti: (0, 0)
    seq = lambda bi, ti: (bi, ti, 0)
    st = lambda bi, ti: (bi, 0, 0)
    return pl.pallas_call(
        functools.partial(_ffn_kernel, nseq=nseq, tt=tt, d_ff=d_ff, col_chunk=col_chunk),
        grid=(b // nseq, t // tt),
        in_specs=[pl.BlockSpec((nseq, tt, d), seq), pl.BlockSpec((1, d), fixed),
                  pl.BlockSpec((d, 2 * d_ff), fixed), pl.BlockSpec((CONV_W, 2 * d_ff), fixed),
                  pl.BlockSpec((1, 2 * d_ff), fixed), pl.BlockSpec((d_ff, d), fixed),
                  pl.BlockSpec((nseq, CONV_W - 1, 2 * d_ff), st)],
        out_specs=[pl.BlockSpec((nseq, tt, d), seq), pl.BlockSpec((nseq, CONV_W - 1, 2 * d_ff), st)],
        out_shape=[jax.ShapeDtypeStruct((b, t, d), F32), jax.ShapeDtypeStruct((b, CONV_W - 1, 2 * d_ff), F32)],
        scratch_shapes=[pltpu.VMEM((nseq, CONV_W - 1, 2 * d_ff), F32)],
        compiler_params=_cparams("parallel", "arbitrary"),
        name="conv_ffn",
    )(x, g.reshape(1, d), w_up, conv_w, conv_b.reshape(1, 2 * d_ff), w_down, conv0)


PROMPT_TILES = dict(proj_tm=512, s5_tt=128, s5_passes=1, rwkv_tt=256, rwkv_chunk=64, rwkv_passes=1,
                    sb_tq=256, sb_tk=256, merge_tm=256, ffn_tt=256, ffn_cols=1408)
SAMPLE_TILES = dict(proj_tm=256, s5_tt=8, s5_passes=3, rwkv_tt=128, rwkv_chunk=64, rwkv_passes=1,
                    merge_tm=256, ffn_cols=1408, sb_pages=8, moba_pages=8)


def _layer(x, st, past, p, tiles):
    b, t, d = x.shape
    width = d // 2
    n_rwkv = 3 * width + RWKV_DECAY_LORA + RWKV_ICLR_LORA + RWKV_GATE_LORA
    off_sb = width + n_rwkv
    off_mb = off_sb + 3 * width
    off_gate = off_mb + 3 * width
    x2 = x.reshape(b * t, d)
    w_in = p['w_in']
    tm = min(tiles['proj_tm'], b * t)

    u_s5, u_rw = proj_plain(x2, p['norm1_g'], w_in[:, :off_sb], (width, n_rwkv), tm)
    q_sb, k_sb, v_sb = proj_qkv(x2, p['norm1_g'], w_in[:, off_sb:off_mb], p['sb_q_g'], p['sb_k_g'], tm)
    q_mb, k_mb, v_mb = proj_qkv(x2, p['norm1_g'], w_in[:, off_mb:off_gate], p['moba_q_g'], p['moba_k_g'], tm)

    n_state = p['s5_ab_re'].shape[0]
    y_s5, s5_re, s5_im = s5_mixer(u_s5.reshape(b, t, width).transpose(1, 0, 2), st['s5_re'].reshape(b, n_state),
                                  st['s5_im'].reshape(b, n_state), p['s5_ab_re'], p['s5_ab_im'],
                                  p['s5_bb'][tiles['s5_passes']], p['s5_cc'][tiles['s5_passes']], p['s5_d'],
                                  p['s5_w_glu'], tiles['s5_tt'], tiles['s5_passes'])
    y_s5 = y_s5.transpose(1, 0, 2).reshape(b * t, width)

    y_rw, wkv, shift = rwkv_mixer(u_rw.reshape(b, t, n_rwkv), st['shift'], st['wkv'], p, tiles['rwkv_tt'],
                                  tiles['rwkv_chunk'], tiles['rwkv_passes'])

    seq = lambda a: a.reshape(b, t, width)
    if past is None:
        y_sb = sb_prompt(seq(q_sb), seq(k_sb), seq(v_sb), tiles['sb_tq'], tiles['sb_tk'])
        y_mb = moba_prompt(seq(q_mb), seq(k_mb), seq(v_mb))
    else:
        y_sb = sb_sample(seq(q_sb), seq(k_sb), seq(v_sb), past['sb_k'], past['sb_v'], past['pages'],
                         tiles['sb_pages'])
        kmean = paged_block_means(past['moba_k'], past['pages'], MOBA_BLOCK)
        y_mb = moba_sample(seq(q_mb), seq(k_mb), seq(v_mb), past['moba_k'], past['moba_v'], past['pages'], kmean,
                           tiles['moba_pages'])

    ys = [y_s5, y_rw.reshape(b * t, width), y_sb.reshape(b * t, width), y_mb.reshape(b * t, width)]
    x1 = merge(x2, p['norm1_g'], w_in[:, off_gate:], ys, p['w_branch'], p['w_out'], tiles['merge_tm'])
    if past is None:
        nseq, tt = 1, tiles['ffn_tt']
    else:
        nseq, tt = b, t
    x_out, conv = conv_ffn(x1.reshape(b, t, d), p['norm2_g'], p['ffn_w_up'], p['ffn_conv_w'], p['ffn_conv_b'],
                           p['ffn_w_down'], st['conv'], nseq, tt, tiles['ffn_cols'])
    n_heads = width // HEAD_DIM
    kv = lambda a: a.reshape(b, t, n_heads, HEAD_DIM)
    new_st = (s5_re.reshape(st['s5_re'].shape), s5_im.reshape(st['s5_im'].shape), wkv, shift, conv,
              kv(k_sb), kv(v_sb), kv(k_mb), kv(v_mb))
    return x_out, new_st


def kernel(x_prompt, x_sample, state_s5_re, state_s5_im, state_rwkv_wkv, state_rwkv_shift, state_ffn_conv, cache_sb_k, cache_sb_v, cache_moba_k, cache_moba_v, page_table, norm1_g, w_in, s5_a_re, s5_a_im, s5_log_dt, s5_b_re, s5_b_im, s5_c_re, s5_c_im, s5_d, s5_w_glu, rwkv_mu, rwkv_w0, rwkv_w_up, rwkv_a0, rwkv_a_up, rwkv_g_up, rwkv_k_k, rwkv_k_a, rwkv_r_k, rwkv_ln_w, rwkv_ln_b, sb_q_g, sb_k_g, moba_q_g, moba_k_g, w_branch, w_out, norm2_g, ffn_w_up, ffn_conv_w, ffn_conv_b, ffn_w_down):
    depth = w_in.shape[0]
    bp = x_prompt.shape[0]
    n_pool, page_size, n_heads, head_dim = cache_sb_k.shape[1:]
    width = n_heads * head_dim
    pool = lambda c: c.reshape(depth * n_pool, page_size, width)
    pools = dict(sb_k=pool(cache_sb_k), sb_v=pool(cache_sb_v), moba_k=pool(cache_moba_k), moba_v=pool(cache_moba_v))
    g_s5, p_s5 = s5_a_re.shape[1:]
    n_rwkv = state_rwkv_shift.shape[-1]
    d_ff2 = state_ffn_conv.shape[-1]
    zero_st = dict(s5_re=jnp.zeros((bp, g_s5, p_s5), F32), s5_im=jnp.zeros((bp, g_s5, p_s5), F32),
                   shift=jnp.zeros((bp, n_rwkv), F32), wkv=jnp.zeros((bp, n_heads, head_dim, head_dim), F32),
                   conv=jnp.zeros((bp, CONV_W - 1, d_ff2), F32))
    yp, ys = x_prompt, x_sample
    states_p, states_s = [], []
    for l in range(depth):
        ab_re, ab_im, bb, cc = s5_operands(s5_a_re[l], s5_a_im[l], s5_log_dt[l], s5_b_re[l], s5_b_im[l],
                                           s5_c_re[l], s5_c_im[l])
        p = dict(norm1_g=norm1_g[l], w_in=w_in[l].astype(BF16), s5_ab_re=ab_re, s5_ab_im=ab_im,
                 s5_bb={1: bb.astype(BF16), 3: bb}, s5_cc={1: cc.astype(BF16), 3: cc}, s5_d=s5_d[l],
                 s5_w_glu=s5_w_glu[l].astype(BF16), rwkv_mu=rwkv_mu[l], rwkv_w0=rwkv_w0[l],
                 rwkv_w_up=rwkv_w_up[l], rwkv_a0=rwkv_a0[l], rwkv_a_up=rwkv_a_up[l], rwkv_g_up=rwkv_g_up[l],
                 rwkv_k_k=rwkv_k_k[l], rwkv_k_a=rwkv_k_a[l], rwkv_r_k=rwkv_r_k[l].reshape(-1),
                 rwkv_ln_w=rwkv_ln_w[l], rwkv_ln_b=rwkv_ln_b[l], sb_q_g=sb_q_g[l], sb_k_g=sb_k_g[l],
                 moba_q_g=moba_q_g[l], moba_k_g=moba_k_g[l], w_branch=w_branch[l].astype(BF16),
                 w_out=w_out[l].astype(BF16), norm2_g=norm2_g[l], ffn_w_up=ffn_w_up[l].astype(BF16),
                 ffn_conv_w=ffn_conv_w[l], ffn_conv_b=ffn_conv_b[l], ffn_w_down=ffn_w_down[l].astype(BF16))
        yp, st_p = _layer(yp, zero_st, None, p, PROMPT_TILES)
        st_s = dict(s5_re=state_s5_re[l], s5_im=state_s5_im[l], shift=state_rwkv_shift[l], wkv=state_rwkv_wkv[l],
                    conv=state_ffn_conv[l])
        past = dict(pools, pages=page_table + l * n_pool)
        ys, st_s = _layer(ys, st_s, past, p, SAMPLE_TILES)
        states_p.append(st_p)
        states_s.append(st_s)
    stacked_p = [jnp.stack(z, axis=0) for z in zip(*states_p)]
    stacked_s = [jnp.stack(z, axis=0) for z in zip(*states_s)]
    out = [yp, ys]
    for a, c in zip(stacked_p, stacked_s):
        out += [a, c]
    return tuple(out)
```

```python
import functools
import math

import jax
import jax.numpy as jnp
from jax import lax
from jax.experimental import pallas as pl
from jax.experimental.pallas import tpu as pltpu

F32 = jnp.float32
BF16 = jnp.bfloat16

HEAD_DIM = 64
S5_GROUP = 16
S5_STATE = 64
RWKV_DECAY_LORA = 64
RWKV_ICLR_LORA = 64
RWKV_GATE_LORA = 128
N_BRANCH = 4
MOBA_BLOCK = 256
MOBA_TOPK = 3
CONV_W = 3
EPS = 1e-6
RWKV_GN_EPS = 64e-5
RWKV_MAX_CHUNK = 64

V7X_VMEM_LIMIT_BYTES = 56 * 1024 * 1024
SUBLANES = 8


def _cparams(*sem):
    return pltpu.CompilerParams(dimension_semantics=sem, vmem_limit_bytes=V7X_VMEM_LIMIT_BYTES)


def _split_bf16(x):
    hi = x.astype(BF16)
    lo = (x - hi.astype(F32)).astype(BF16)
    return hi, lo


_NN = (((1,), (0,)), ((), ()))
_NT = (((1,), (1,)), ((), ()))


def _dg(a, b, dims):
    return lax.dot_general(a, b, dims, preferred_element_type=F32)


def _mm(a, b, passes=1, dims=_NN):
    if passes == 1:
        return _dg(a.astype(BF16), b.astype(BF16), dims)
    ah, al = _split_bf16(a)
    if b.dtype == BF16:
        return _dg(ah, b, dims) + _dg(al, b, dims)
    bh, bl = _split_bf16(b)
    if passes == 2:
        return _dg(ah, bh, dims) + _dg(al, bh, dims)
    return _dg(ah, bh, dims) + (_dg(al, bh, dims) + _dg(ah, bl, dims))


def _head_block_matrix(width, value):
    r = jnp.arange(width) // HEAD_DIM
    return jnp.where(r[:, None] == r[None, :], value, 0.0).astype(BF16)


def _rms(x, g):
    return x * lax.rsqrt(jnp.mean(x * x, axis=-1, keepdims=True) + EPS) * g


def _proj_plain_kernel(x_ref, g_ref, w_ref, *out_refs, widths):
    h = _rms(x_ref[...], g_ref[...]).astype(BF16)
    u = _dg(h, w_ref[...], _NN)
    off = 0
    for o_ref, n in zip(out_refs, widths):
        o_ref[...] = u[:, off:off + n]
        off += n


def proj_plain(x, g, w, widths, tm, first_time_major):
    b, t, d = x.shape
    n = w.shape[1]
    assert sum(widths) == n and t % tm == 0
    fixed = lambda bi, ti: (0, 0)
    seq = lambda bi, ti: (bi, ti, 0)
    out_specs = [pl.BlockSpec((None, tm, k), seq) for k in widths]
    out_shape = [jax.ShapeDtypeStruct((b, t, k), F32) for k in widths]
    if first_time_major:
        out_specs[0] = pl.BlockSpec((tm, widths[0]), lambda bi, ti: (ti, bi))
        out_shape[0] = jax.ShapeDtypeStruct((t, b * widths[0]), F32)
    return pl.pallas_call(
        functools.partial(_proj_plain_kernel, widths=widths),
        grid=(b, t // tm),
        in_specs=[pl.BlockSpec((None, tm, d), seq), pl.BlockSpec((1, d), fixed), pl.BlockSpec((d, n), fixed)],
        out_specs=out_specs,
        out_shape=out_shape,
        compiler_params=_cparams("parallel", "parallel"),
        name="proj_plain",
    )(x, g.reshape(1, d), w)


def _head_rms(x, hm, g):
    ms = _mm(x * x, hm, passes=2)
    return x * lax.rsqrt(ms + EPS) * g


def _proj_qkv_kernel(x_ref, g_ref, w_ref, hm_ref, gq_ref, gk_ref, q_ref, k_ref, v_ref, *, width, q_scale):
    h = _rms(x_ref[...], g_ref[...]).astype(BF16)
    u = _dg(h, w_ref[...], _NN)
    hm = hm_ref[...]
    q_ref[...] = _head_rms(u[:, :width], hm, gq_ref[...]) * q_scale
    k_ref[...] = _head_rms(u[:, width:2 * width], hm, gk_ref[...])
    v_ref[...] = u[:, 2 * width:]


def proj_qkv(x2d, g, w, g_q, g_k, tm):
    m, d = x2d.shape
    width = w.shape[1] // 3
    n_heads = width // HEAD_DIM
    hm = _head_block_matrix(width, 1.0 / HEAD_DIM)
    row = lambda i: (i, 0)
    fixed = lambda i: (0, 0)
    return pl.pallas_call(
        functools.partial(_proj_qkv_kernel, width=width, q_scale=HEAD_DIM ** -0.5),
        grid=(m // tm,),
        in_specs=[pl.BlockSpec((tm, d), row), pl.BlockSpec((1, d), fixed),
                  pl.BlockSpec((d, 3 * width), fixed), pl.BlockSpec((width, width), fixed),
                  pl.BlockSpec((1, width), fixed), pl.BlockSpec((1, width), fixed)],
        out_specs=[pl.BlockSpec((tm, width), row)] * 3,
        out_shape=[jax.ShapeDtypeStruct((m, width), F32)] * 3,
        compiler_params=_cparams("parallel"),
        name="proj_qkv",
    )(x2d, g.reshape(1, d), w, hm, jnp.tile(g_q, n_heads).reshape(1, width),
      jnp.tile(g_k, n_heads).reshape(1, width))


def _s5_disc_kernel(are_ref, aim_ref, ldt_ref, bre_ref, bim_ref, abre_ref, abim_ref, bbre_ref, bbim_ref):
    lr = are_ref[...]
    li = aim_ref[...]
    dt = jnp.exp(ldt_ref[...])
    mag = jnp.exp(lr * dt)
    ab_re = mag * jnp.cos(li * dt)
    ab_im = mag * jnp.sin(li * dt)
    den = lr * lr + li * li
    zr = ((ab_re - 1.0) * lr + ab_im * li) / den
    zi = (ab_im * lr - (ab_re - 1.0) * li) / den
    abre_ref[...] = ab_re
    abim_ref[...] = ab_im
    br = bre_ref[...]
    bi = bim_ref[...]
    bbre_ref[...] = zr[:, None, :] * br - zi[:, None, :] * bi
    bbim_ref[...] = zr[:, None, :] * bi + zi[:, None, :] * br


def s5_discretize(a_re, a_im, log_dt, b_re, b_im):
    g, p = a_re.shape
    gc = b_re.shape[1]
    return pl.pallas_call(
        _s5_disc_kernel,
        out_shape=[jax.ShapeDtypeStruct((g, p), F32)] * 2 + [jax.ShapeDtypeStruct((g, gc, p), F32)] * 2,
        name="s5_discretize",
    )(a_re, a_im, log_dt.reshape(g, 1), b_re, b_im)


def _gelu_tanh(x):
    c = math.sqrt(2.0 / math.pi)
    return 0.5 * x * (1.0 + jnp.tanh(c * (x + 0.044715 * (x * x * x))))


def _s5_kernel(u_ref, s0re_ref, s0im_ref, abre_ref, abim_ref, bb_ref, cc_ref, d_ref, wglu_ref,
               y_ref, sre_ref, sim_ref, xs_ref, cre_ref, cim_ref, *, tt, n_state, passes, lane_chunk):
    ti = pl.program_id(1)

    @pl.when(ti == 0)
    def _():
        cre_ref[...] = s0re_ref[...]
        cim_ref[...] = s0im_ref[...]

    width = u_ref.shape[-1]
    u = u_ref[...].reshape(tt * SUBLANES, width)
    xs_ref[...] = _mm(u, bb_ref[...], passes)

    for c0 in range(0, n_state, lane_chunk):
        re_sl = pl.ds(c0, lane_chunk)
        im_sl = pl.ds(n_state + c0, lane_chunk)
        a_re = jnp.broadcast_to(abre_ref[:, re_sl], (SUBLANES, lane_chunk))
        a_im = jnp.broadcast_to(abim_ref[:, re_sl], (SUBLANES, lane_chunk))

        def step(t, carry):
            s_re, s_im = carry
            rows = pl.ds(pl.multiple_of(t * SUBLANES, SUBLANES), SUBLANES)
            n_re = a_re * s_re - a_im * s_im + xs_ref[rows, re_sl]
            n_im = a_re * s_im + a_im * s_re + xs_ref[rows, im_sl]
            xs_ref[rows, re_sl] = n_re
            xs_ref[rows, im_sl] = n_im
            return n_re, n_im

        s_re, s_im = lax.fori_loop(0, tt, step, (cre_ref[:, re_sl], cim_ref[:, re_sl]), unroll=8)
        cre_ref[:, re_sl] = s_re
        cim_ref[:, re_sl] = s_im

    y = _mm(xs_ref[...], cc_ref[...], passes) + d_ref[...] * u
    y = _gelu_tanh(y)
    y = y * jax.nn.sigmoid(_mm(y, wglu_ref[...]))
    y_ref[...] = y.reshape(tt, SUBLANES, width)
    sre_ref[...] = cre_ref[...]
    sim_ref[...] = cim_ref[...]


def s5_mixer(u_tm, s0_re, s0_im, ab_re, ab_im, bb, cc, d_skip, w_glu, tt, passes):
    t, b, width = u_tm.shape
    n_state = ab_re.shape[-1]
    assert t % tt == 0 and b % SUBLANES == 0
    fixed = lambda bi, ti: (0, 0)
    st = lambda bi, ti: (bi, 0)
    return pl.pallas_call(
        functools.partial(_s5_kernel, tt=tt, n_state=n_state, passes=passes, lane_chunk=512),
        grid=(b // SUBLANES, t // tt),
        in_specs=[pl.BlockSpec((tt, SUBLANES, width), lambda bi, ti: (ti, bi, 0)),
                  pl.BlockSpec((SUBLANES, n_state), st), pl.BlockSpec((SUBLANES, n_state), st),
                  pl.BlockSpec((1, n_state), fixed), pl.BlockSpec((1, n_state), fixed),
                  pl.BlockSpec((width, 2 * n_state), fixed), pl.BlockSpec((2 * n_state, width), fixed),
                  pl.BlockSpec((1, width), fixed), pl.BlockSpec((width, width), fixed)],
        out_specs=[pl.BlockSpec((tt, SUBLANES, width), lambda bi, ti: (ti, bi, 0)),
                   pl.BlockSpec((SUBLANES, n_state), st), pl.BlockSpec((SUBLANES, n_state), st)],
        out_shape=[jax.ShapeDtypeStruct((t, b, width), F32),
                   jax.ShapeDtypeStruct((b, n_state), F32), jax.ShapeDtypeStruct((b, n_state), F32)],
        scratch_shapes=[pltpu.VMEM((tt * SUBLANES, 2 * n_state), F32),
                        pltpu.VMEM((SUBLANES, n_state), F32), pltpu.VMEM((SUBLANES, n_state), F32)],
        compiler_params=_cparams("parallel", "arbitrary"),
        name="s5_mixer",
    )(u_tm, s0_re, s0_im, ab_re.reshape(1, n_state), ab_im.reshape(1, n_state), bb, cc,
      d_skip.reshape(1, width), w_glu)


def s5_operands(a_re, a_im, log_dt, b_re, b_im, c_re, c_im):
    g, p = a_re.shape
    gc = b_re.shape[-1]
    ab_re, ab_im, bb_re, bb_im = s5_discretize(a_re, a_im, log_dt, b_re.transpose(0, 2, 1),
                                               b_im.transpose(0, 2, 1))
    eye = jnp.eye(g, dtype=F32)
    bd = lambda m: jnp.einsum('gcp,gh->gchp', m, eye).reshape(g * gc, g * p)
    bb = jnp.concatenate([bd(bb_re), bd(bb_im)], axis=1)
    bdc = lambda m: jnp.einsum('gcp,gh->hpgc', m, eye).reshape(g * p, g * gc)
    cc = jnp.concatenate([bdc(c_re), -bdc(c_im)], axis=0)
    return ab_re.reshape(g * p), ab_im.reshape(g * p), bb, cc


def _softplus(x):
    return jnp.maximum(x, 0.0) + jnp.log(1.0 + jnp.exp(-jnp.abs(x)))


def _rwkv_kernel(u_ref, shift0_ref, wkv0_ref, mu_ref, w0_ref, wup_ref, a0_ref, aup_ref, gup_ref, kk_ref,
                 ka_ref, rk_ref, lnw_ref, lnb_ref, hsum_ref, y_ref, wkv_ref, shift_ref,
                 prev_ref, s_ref, yt_ref, *, tt, chunk, width, passes, t_valid):
    ti = pl.program_id(1)
    n_heads = width // HEAD_DIM

    @pl.when(ti == 0)
    def _():
        prev_ref[...] = shift0_ref[...]
        s_ref[...] = wkv0_ref[...]

    u = u_ref[...]
    row = lax.broadcasted_iota(jnp.int32, (tt, 1), 0)
    prev = jnp.where(row == 0, prev_ref[...], pltpu.roll(u, 1, axis=0))
    xs = u + mu_ref[...] * (prev - u)
    prev_ref[...] = u[t_valid - 1:t_valid, :]
    shift_ref[...] = u[t_valid - 1:t_valid, :]

    w3 = 3 * width
    o1 = w3 + RWKV_DECAY_LORA
    o2 = o1 + RWKV_ICLR_LORA
    r = xs[:, :width]
    k = xs[:, width:2 * width]
    v = xs[:, 2 * width:w3]
    w_log = -_softplus(-(w0_ref[...] + _mm(jnp.tanh(xs[:, w3:o1]), wup_ref[...], 3))) - 0.5
    log_decay = -jnp.exp(w_log)
    a = jax.nn.sigmoid(a0_ref[...] + _mm(xs[:, o1:o2], aup_ref[...], 3))
    g = _mm(jax.nn.sigmoid(xs[:, o2:]), gup_ref[...], 3)
    hsum = hsum_ref[...]
    kk = k * kk_ref[...]
    kk = kk / jnp.maximum(jnp.sqrt(_mm(kk * kk, hsum, 2)), 1e-12)
    k2 = k * (1.0 + (a - 1.0) * ka_ref[...])
    a_vec = -kk
    b_vec = kk * a
    if t_valid < tt:
        live = row < t_valid
        log_decay = jnp.where(live, log_decay, 0.0)
        a_vec = jnp.where(live, a_vec, 0.0)
        b_vec = jnp.where(live, b_vec, 0.0)
        k2 = jnp.where(live, k2, 0.0)
    vt = v.T

    ri = lax.broadcasted_iota(jnp.int32, (chunk, chunk), 0)
    ci = lax.broadcasted_iota(jnp.int32, (chunk, chunk), 1)
    upper_strict = ri < ci
    upper_incl = ri <= ci
    lower_incl = (ri >= ci).astype(F32)
    n_double = (min(chunk, t_valid) - 1).bit_length()

    heads = range(n_heads)
    hsl = [slice(h * HEAD_DIM, (h + 1) * HEAD_DIM) for h in heads]
    eye = (ri == ci).astype(F32)

    pre = []
    for c in range(tt // chunk):
        rows = slice(c * chunk, (c + 1) * chunk)
        ld = log_decay[rows]
        cl = _mm(lower_incl, ld, 3)
        e_incl = jnp.exp(cl)
        e_inv = jnp.exp(-cl)
        rt = r[rows] * e_incl
        at = a_vec[rows] * jnp.exp(cl - ld)
        bt = b_vec[rows] * e_inv
        kt = k2[rows] * e_inv
        bk = [jnp.concatenate([bt[:, hs], kt[:, hs]], axis=0) for hs in hsl]
        ar = [jnp.concatenate([at[:, hs], rt[:, hs]], axis=0) for hs in hsl]
        gram = [_mm(bk[h], ar[h], passes, _NT) for h in heads]
        n_t = [jnp.where(upper_strict, gm[:chunk, :chunk], 0.0) for gm in gram]
        m2_t = [jnp.where(upper_strict, gm[chunk:, :chunk], 0.0) for gm in gram]
        m34_t = [jnp.concatenate([jnp.where(upper_incl, gm[:chunk, chunk:], 0.0),
                                  jnp.where(upper_incl, gm[chunk:, chunk:], 0.0)], axis=0) for gm in gram]
        vt_c = [vt[hs, rows] for hs in hsl]
        vm = [_mm(vt_c[h], m2_t[h], passes) for h in heads]
        t_inv = [eye + n for n in n_t]
        nk = n_t
        for _ in range(1, n_double):
            nk = [_mm(n, n, passes) for n in nk]
            t_inv = [t_inv[h] + _mm(t_inv[h], nk[h], passes) for h in heads]
        pre.append((rows, bk, ar, m34_t, vt_c, vm, t_inv, e_incl[chunk - 1:chunk, :]))

    state = [s_ref[h] for h in heads]
    for rows, bk, ar, m34_t, vt_c, vm, t_inv, p_end in pre:
        sa = [_mm(state[h], ar[h], passes, _NT) for h in heads]
        x = [_mm(sa[h][:, :chunk] + vm[h], t_inv[h], passes) for h in heads]
        uv = [jnp.concatenate([x[h], vt_c[h]], axis=1) for h in heads]
        for h in heads:
            yt_ref[hsl[h], rows] = sa[h][:, chunk:] + _mm(uv[h], m34_t[h], passes)
        state = [(state[h] + _mm(uv[h], bk[h], passes)) * p_end[:, hsl[h]] for h in heads]
    for h in heads:
        s_ref[h] = state[h]

    y = yt_ref[...].T
    hmean = hsum * (1.0 / HEAD_DIM)
    yc = y - _mm(y, hmean, 2)
    var = _mm(yc * yc, hmean, 2)
    yn = yc * lax.rsqrt(var + RWKV_GN_EPS) * lnw_ref[...] + lnb_ref[...]
    bonus = _mm(r * k2 * rk_ref[...], hsum, 2) * v
    y_ref[...] = (yn + bonus) * g
    wkv_ref[...] = s_ref[...]


def rwkv_mixer(u, shift0, wkv0, p, tt, chunk, passes):
    b, t_valid, n_in = u.shape
    n_heads = wkv0.shape[1]
    width = n_heads * HEAD_DIM
    if t_valid < tt:
        assert t_valid <= RWKV_MAX_CHUNK
        u = _pad_rows(u, tt)
        chunk = tt
    else:
        assert chunk <= RWKV_MAX_CHUNK
        t_valid = tt
    t = u.shape[1]
    assert t % tt == 0 and tt % chunk == 0
    fixed = lambda bi, ti: (0, 0)
    vec = lambda n: pl.BlockSpec((1, n), fixed)
    args = [p['rwkv_mu'].reshape(1, n_in), p['rwkv_w0'].reshape(1, width), p['rwkv_w_up'],
            p['rwkv_a0'].reshape(1, width), p['rwkv_a_up'], p['rwkv_g_up'], p['rwkv_k_k'].reshape(1, width),
            p['rwkv_k_a'].reshape(1, width), p['rwkv_r_k'].reshape(1, width), p['rwkv_ln_w'].reshape(1, width),
            p['rwkv_ln_b'].reshape(1, width), _head_block_matrix(width, 1.0)]
    specs = [vec(n_in), vec(width), pl.BlockSpec(p['rwkv_w_up'].shape, fixed), vec(width),
             pl.BlockSpec(p['rwkv_a_up'].shape, fixed), pl.BlockSpec(p['rwkv_g_up'].shape, fixed),
             vec(width), vec(width), vec(width), vec(width), vec(width), pl.BlockSpec((width, width), fixed)]
    y, wkv, shift = pl.pallas_call(
        functools.partial(_rwkv_kernel, tt=tt, chunk=chunk, width=width, passes=passes, t_valid=t_valid),
        grid=(b, t // tt),
        in_specs=[pl.BlockSpec((None, tt, n_in), lambda bi, ti: (bi, ti, 0)),
                  pl.BlockSpec((None, 1, n_in), lambda bi, ti: (bi, 0, 0)),
                  pl.BlockSpec((None, n_heads, HEAD_DIM, HEAD_DIM), lambda bi, ti: (bi, 0, 0, 0))] + specs,
        out_specs=[pl.BlockSpec((None, tt, width), lambda bi, ti: (bi, ti, 0)),
                   pl.BlockSpec((None, n_heads, HEAD_DIM, HEAD_DIM), lambda bi, ti: (bi, 0, 0, 0)),
                   pl.BlockSpec((None, 1, n_in), lambda bi, ti: (bi, 0, 0))],
        out_shape=[jax.ShapeDtypeStruct((b, t, width), F32),
                   jax.ShapeDtypeStruct((b, n_heads, HEAD_DIM, HEAD_DIM), F32),
                   jax.ShapeDtypeStruct((b, 1, n_in), F32)],
        scratch_shapes=[pltpu.VMEM((1, n_in), F32), pltpu.VMEM((n_heads, HEAD_DIM, HEAD_DIM), F32),
                        pltpu.VMEM((width, tt), F32)],
        compiler_params=_cparams("parallel", "arbitrary"),
        name="rwkv_mixer",
    )(u, shift0.reshape(b, 1, n_in), wkv0, *args)
    return y[:, :t_valid] if t_valid < tt else y, wkv, shift.reshape(b, n_in)


NEG_BIG = -1e30
_TN = (((0,), (0,)), ((), ()))
HEAD_PAIR = 2 * HEAD_DIM


def _proj_qkv_t_kernel(x_ref, g_ref, wt_ref, gq_ref, gk_ref, qt_ref, kt_ref, vt_ref, krow_ref, vtb_ref, kmean_ref,
                       *, width, q_scale, block):
    ti = pl.program_id(1)
    tm = x_ref.shape[0]
    n_heads = width // HEAD_DIM
    h = _rms(x_ref[...], g_ref[...]).astype(BF16)
    ut = _dg(wt_ref[...], h, _NT)

    def head_rms_t(xt, g_col):
        x3 = xt.reshape(n_heads, HEAD_DIM, tm)
        ms = jnp.mean(x3 * x3, axis=1, keepdims=True)
        return (x3 * lax.rsqrt(ms + EPS) * g_col[None, :, :]).reshape(width, tm)

    qt_ref[...] = head_rms_t(ut[:width], gq_ref[...]) * q_scale
    kt = head_rms_t(ut[width:2 * width], gk_ref[...])
    kt_ref[...] = kt
    vt = ut[2 * width:]
    vt_ref[...] = vt
    vtb_ref[...] = vt.astype(BF16)
    k_rows = kt.T
    krow_ref[...] = k_rows.astype(BF16)
    @pl.when(ti == 0)
    def _():
        kmean_ref[...] = jnp.zeros_like(kmean_ref)

    per_tile = tm // block
    blk = lax.broadcasted_iota(jnp.int32, kmean_ref.shape, 0)
    kmean = kmean_ref[...]
    for n in range(per_tile):
        mean = jnp.sum(k_rows[n * block:(n + 1) * block], axis=0, keepdims=True) * (1.0 / block)
        kmean = jnp.where(blk == ti * per_tile + n, mean, kmean)
    kmean_ref[...] = kmean


def proj_qkv_t(x, g, w_t, g_q, g_k, tm):
    b, t, d = x.shape
    width = w_t.shape[0] // 3
    nb = t // MOBA_BLOCK
    assert t % tm == 0 and tm % MOBA_BLOCK == 0
    fixed = lambda bi, ti: (0, 0)
    tr = pl.BlockSpec((None, width, tm), lambda bi, ti: (bi, 0, ti))
    return pl.pallas_call(
        functools.partial(_proj_qkv_t_kernel, width=width, q_scale=HEAD_DIM ** -0.5, block=MOBA_BLOCK),
        grid=(b, t // tm),
        in_specs=[pl.BlockSpec((None, tm, d), lambda bi, ti: (bi, ti, 0)), pl.BlockSpec((1, d), fixed),
                  pl.BlockSpec((3 * width, d), fixed), pl.BlockSpec((HEAD_DIM, 1), fixed),
                  pl.BlockSpec((HEAD_DIM, 1), fixed)],
        out_specs=[tr, tr, tr, pl.BlockSpec((None, tm, width), lambda bi, ti: (bi, ti, 0)), tr,
                   pl.BlockSpec((None, nb, width), lambda bi, ti: (bi, 0, 0))],
        out_shape=[jax.ShapeDtypeStruct((b, width, t), F32)] * 3
                  + [jax.ShapeDtypeStruct((b, t, width), BF16), jax.ShapeDtypeStruct((b, width, t), BF16),
                     jax.ShapeDtypeStruct((b, nb, width), F32)],
        compiler_params=_cparams("parallel", "arbitrary"),
        name="proj_qkv_t",
    )(x, g.reshape(1, d), w_t, g_q.reshape(HEAD_DIM, 1), g_k.reshape(HEAD_DIM, 1))


def _head_pair_queries(qt_ref, h, tq):
    p0 = (h // 2) * HEAD_PAIR
    rows = lax.broadcasted_iota(jnp.int32, (HEAD_PAIR, tq), 0) // HEAD_DIM
    return jnp.where(rows == h % 2, qt_ref[p0:p0 + HEAD_PAIR, :], 0.0)


def _sb_prompt_kernel(qt_ref, k_ref, vt_ref, o_ref, qm_ref, acc_ref, run_ref, *, tq, width):
    qi = pl.program_id(1)
    n_heads = width // HEAD_DIM
    krow = lax.broadcasted_iota(jnp.int32, (tq, tq), 0)
    qcol = lax.broadcasted_iota(jnp.int32, (tq, tq), 1)
    later = jnp.where(qcol > krow, 1.0, 0.0).astype(BF16)
    strict = krow < qcol
    for h in range(n_heads):
        qm_ref[h] = _head_pair_queries(qt_ref, h, tq).astype(BF16)
    acc_ref[...] = jnp.zeros_like(acc_ref)
    run_ref[...] = jnp.zeros_like(run_ref)

    heads = range(n_heads)
    hsl = [slice(h * HEAD_DIM, (h + 1) * HEAD_DIM) for h in heads]

    def block(j, diag):
        ks = pl.ds(pl.multiple_of(j * tq, tq), tq)
        run = run_ref[...]
        z = [_dg(k_ref[ks, (h // 2) * HEAD_PAIR:(h // 2 + 1) * HEAD_PAIR], qm_ref[h], _NN) for h in heads]
        sp = [_softplus(zh) for zh in z]
        l1m = [jnp.where(strict, -x, 0.0) if diag else -x for x in sp]
        parts = [_split_bf16(x) for x in l1m]
        rest = [_dg(later, hi, _NN) + _dg(later, lo, _NN) + run[h:h + 1, :] for h, (hi, lo) in enumerate(parts)]
        w = [jnp.exp(z[h] - sp[h] + rest[h]) for h in heads]
        if diag:
            w = [jnp.where(strict, x, 0.0) for x in w]
        pv = [_dg(vt_ref[hsl[h], ks], w[h].astype(BF16), _NN) for h in heads]
        for h in heads:
            acc_ref[hsl[h], :] += pv[h]
        run_ref[...] = jnp.concatenate([rest[h][0:1, :] + l1m[h][0:1, :] for h in heads], axis=0)

    block(qi, True)

    def body(jj, carry):
        block(qi - 1 - jj, False)
        return carry

    lax.fori_loop(0, qi, body, 0)
    o_ref[...] = acc_ref[...]


def _attn_prompt_call(kernel_fn, name, qt, k_rows, vt_b, extra, extra_specs, scratch, tq):
    b, width, t = qt.shape
    n_heads = width // HEAD_DIM
    tile = pl.BlockSpec((None, width, tq), lambda bi, qi: (bi, 0, qi))
    return pl.pallas_call(
        kernel_fn,
        grid=(b, t // tq),
        in_specs=[tile, pl.BlockSpec((None, t, width), lambda bi, qi: (bi, 0, 0)),
                  pl.BlockSpec((None, width, t), lambda bi, qi: (bi, 0, 0))] + extra_specs,
        out_specs=tile,
        out_shape=jax.ShapeDtypeStruct((b, width, t), F32),
        scratch_shapes=[pltpu.VMEM((n_heads, HEAD_PAIR, tq), BF16), pltpu.VMEM((width, tq), F32)] + scratch,
        compiler_params=_cparams("parallel", "arbitrary"),
        name=name,
    )(qt, k_rows, vt_b, *extra)


def sb_prompt(qt, k_rows, vt_b, tq):
    b, width, t = qt.shape
    assert t % tq == 0
    n_heads = width // HEAD_DIM
    return _attn_prompt_call(functools.partial(_sb_prompt_kernel, tq=tq, width=width), "sb_prompt", qt, k_rows,
                             vt_b, [], [], [pltpu.VMEM((n_heads, tq), F32)], tq)


def _topk_block_mask(gate, n_valid, topk, axis):
    nb = gate.shape[axis]
    idx = lax.broadcasted_iota(jnp.int32, gate.shape, axis)
    rank = jnp.zeros(gate.shape, jnp.int32)
    for m in range(nb):
        gm = gate[:, m:m + 1] if axis == 1 else gate[m:m + 1, :]
        beats = (gm > gate) | ((gm == gate) & (m < idx))
        rank = rank + jnp.where(beats & (m < n_valid), 1, 0)
    return (idx < n_valid) & (rank < topk)


def _moba_prompt_kernel(qt_ref, k_ref, vt_ref, kmean_ref, o_ref, qm_ref, acc_ref, sel_ref, m_ref, l_ref,
                        *, tq, width):
    qi = pl.program_id(1)
    n_heads = width // HEAD_DIM
    krow = lax.broadcasted_iota(jnp.int32, (tq, tq), 0)
    qcol = lax.broadcasted_iota(jnp.int32, (tq, tq), 1)
    causal = krow <= qcol
    for h in range(n_heads):
        p0 = (h // 2) * HEAD_PAIR
        q_pair = _head_pair_queries(qt_ref, h, tq)
        qm_ref[h] = q_pair.astype(BF16)
        gate = _mm(kmean_ref[:, p0:p0 + HEAD_PAIR], q_pair, 3)
        sel_ref[h] = jnp.where(_topk_block_mask(gate, qi, MOBA_TOPK, 0), 1.0, 0.0)
    acc_ref[...] = jnp.zeros_like(acc_ref)
    l_ref[...] = jnp.zeros_like(l_ref)
    m_ref[...] = jnp.full(m_ref.shape, NEG_BIG, F32)

    heads = range(n_heads)
    hsl = [slice(h * HEAD_DIM, (h + 1) * HEAD_DIM) for h in heads]

    def block(n, diag):
        ks = pl.ds(pl.multiple_of(n * tq, tq), tq)
        m_old = m_ref[...]
        l_old = l_ref[...]
        s = [_dg(k_ref[ks, (h // 2) * HEAD_PAIR:(h // 2 + 1) * HEAD_PAIR], qm_ref[h], _NN) for h in heads]
        if diag:
            mask = [causal] * n_heads
        else:
            mask = [jnp.broadcast_to(sel_ref[h, pl.ds(n, 1), :] > 0.5, (tq, tq)) for h in heads]
        s = [jnp.where(mask[h], s[h], NEG_BIG) for h in heads]
        m_new = [jnp.maximum(m_old[h:h + 1, :], jnp.max(s[h], axis=0, keepdims=True)) for h in heads]
        p = [jnp.where(mask[h], jnp.exp(s[h] - m_new[h]), 0.0) for h in heads]
        alpha = [jnp.exp(m_old[h:h + 1, :] - m_new[h]) for h in heads]
        l_new = [alpha[h] * l_old[h:h + 1, :] + jnp.sum(p[h], axis=0, keepdims=True) for h in heads]
        pv = [_dg(vt_ref[hsl[h], ks], p[h].astype(BF16), _NN) for h in heads]
        for h in heads:
            acc_ref[hsl[h], :] = alpha[h] * acc_ref[hsl[h], :] + pv[h]
        m_ref[...] = jnp.concatenate(m_new, axis=0)
        l_ref[...] = jnp.concatenate(l_new, axis=0)

    block(qi, True)

    def body(n, carry):
        block(n, False)
        return carry

    lax.fori_loop(0, qi, body, 0)
    for h in range(n_heads):
        hs = slice(h * HEAD_DIM, (h + 1) * HEAD_DIM)
        o_ref[hs, :] = acc_ref[hs, :] / l_ref[h:h + 1, :]


def moba_prompt(qt, k_rows, vt_b, kmean):
    b, width, t = qt.shape
    tq = MOBA_BLOCK
    nb = t // tq
    assert t % tq == 0
    n_heads = width // HEAD_DIM
    return _attn_prompt_call(functools.partial(_moba_prompt_kernel, tq=tq, width=width), "moba_prompt", qt, k_rows,
                             vt_b, [kmean], [pl.BlockSpec((None, nb, width), lambda bi, qi: (bi, 0, 0))],
                             [pltpu.VMEM((n_heads, nb, tq), F32), pltpu.VMEM((n_heads, tq), F32),
                              pltpu.VMEM((n_heads, tq), F32)], tq)


def _stack_heads(q, n_heads):
    t, width = q.shape
    rows = lax.broadcasted_iota(jnp.int32, (n_heads * t, width), 0) // t
    lanes = lax.broadcasted_iota(jnp.int32, (n_heads * t, width), 1) // HEAD_DIM
    return jnp.where(rows == lanes, jnp.concatenate([q] * n_heads, axis=0), 0.0)


def _unstack_heads(acc, n_heads):
    rows_total, width = acc.shape
    t = rows_total // n_heads
    rows = lax.broadcasted_iota(jnp.int32, (rows_total, width), 0) // t
    lanes = lax.broadcasted_iota(jnp.int32, (rows_total, width), 1) // HEAD_DIM
    kept = jnp.where(rows == lanes, acc, 0.0).reshape(n_heads, t, width)
    return jnp.sum(kept, axis=0)


def _later_matrix(tk):
    ki = lax.broadcasted_iota(jnp.int32, (tk, tk), 0)
    kj = lax.broadcasted_iota(jnp.int32, (tk, tk), 1)
    return jnp.where(ki > kj, 1.0, 0.0).astype(BF16)


def _sb_sample_kernel(pt_ref, q_ref, knew_ref, vnew_ref, *refs, n_pages_step, t_new, n_heads):
    k_refs = refs[:n_pages_step]
    v_refs = refs[n_pages_step:2 * n_pages_step]
    o_ref, qs_ref, acc_ref, run_ref = refs[2 * n_pages_step:]
    step = pl.program_id(1)
    tk = knew_ref.shape[0]
    later = _later_matrix(tk)

    def tile(k_tile, v_tile, mask, paged):
        z = _dg(qs_ref[...], k_tile.astype(BF16), _NN if paged else _NT)
        sp = _softplus(z)
        l1m = -sp if mask is None else jnp.where(mask, -sp, 0.0)
        rest = _mm(l1m, later, 2) + run_ref[...]
        w = jnp.exp(z - sp + rest)
        if mask is not None:
            w = jnp.where(mask, w, 0.0)
        acc_ref[...] += _mm(w, v_tile, 1, _NT if paged else _NN)
        run_ref[...] = rest[:, 0:1] + l1m[:, 0:1]

    @pl.when(step == 0)
    def _():
        qs_ref[...] = _stack_heads(q_ref[...], n_heads).astype(BF16)
        acc_ref[...] = jnp.zeros_like(acc_ref)
        run_ref[...] = jnp.zeros_like(run_ref)
        rows = lax.broadcasted_iota(jnp.int32, (n_heads * t_new, tk), 0) % t_new
        cols = lax.broadcasted_iota(jnp.int32, (n_heads * t_new, tk), 1)
        tile(knew_ref[...], vnew_ref[...], cols < rows, False)

    qs = qs_ref[...]
    z = [_dg(qs, k_refs[i][...].astype(BF16), _NN) for i in range(n_pages_step)]
    sp = [_softplus(x) for x in z]
    cum = [_mm(-x, later, 2) for x in sp]
    run = run_ref[...]
    acc = acc_ref[...]
    for i in range(n_pages_step):
        w = jnp.exp(z[i] - sp[i] + cum[i] + run)
        acc = acc + _mm(w, v_refs[i][...], 1, _NT)
        run = run + (cum[i][:, 0:1] - sp[i][:, 0:1])
    acc_ref[...] = acc
    run_ref[...] = run

    @pl.when(step == pl.num_programs(1) - 1)
    def _():
        o_ref[...] = _unstack_heads(acc_ref[...], n_heads)


def _page_specs(n_pages, n_pages_step, width, page_size, reverse):
    def spec(i):
        def index(bi, si, pt):
            p = si * n_pages_step + i
            return (pt[bi, n_pages - 1 - p if reverse else p], 0, 0)
        return pl.BlockSpec((None, width, page_size), index)
    return [spec(i) for i in range(n_pages_step)]


def _pad_rows(x, rows):
    return jnp.pad(x, ((0, 0), (0, rows - x.shape[1]), (0, 0)))


def sb_sample(q, k_new, v_new, k_pool, v_pool, pages, n_pages_step):
    b, t_new, width = q.shape
    n_heads = width // HEAD_DIM
    page_size = k_pool.shape[2]
    n_pages = pages.shape[1]
    assert n_pages % n_pages_step == 0 and t_new <= page_size
    new = pl.BlockSpec((None, t_new, width), lambda bi, si, pt: (bi, 0, 0))
    new_pad = pl.BlockSpec((None, page_size, width), lambda bi, si, pt: (bi, 0, 0))
    page_specs = _page_specs(n_pages, n_pages_step, width, page_size, reverse=True)
    return pl.pallas_call(
        functools.partial(_sb_sample_kernel, n_pages_step=n_pages_step, t_new=t_new, n_heads=n_heads),
        grid_spec=pltpu.PrefetchScalarGridSpec(
            num_scalar_prefetch=1, grid=(b, n_pages // n_pages_step),
            in_specs=[new, new_pad, new_pad] + page_specs + page_specs,
            out_specs=new,
            scratch_shapes=[pltpu.VMEM((n_heads * t_new, width), BF16),
                            pltpu.VMEM((n_heads * t_new, width), F32),
                            pltpu.VMEM((n_heads * t_new, 1), F32)]),
        out_shape=jax.ShapeDtypeStruct((b, t_new, width), F32),
        compiler_params=_cparams("parallel", "arbitrary"),
        name="sb_sample",
    )(pages, q, _pad_rows(k_new, page_size), _pad_rows(v_new, page_size),
      *([k_pool] * n_pages_step), *([v_pool] * n_pages_step))


def _block_mean_kernel(pt_ref, *refs, n_pages_step, pages_per_block):
    k_refs = refs[:n_pages_step]
    o_ref = refs[n_pages_step]
    page_size = k_refs[0].shape[1]
    scale = 1.0 / (pages_per_block * page_size)
    ones = jnp.ones((SUBLANES, page_size), BF16)
    means = []
    for n in range(n_pages_step // pages_per_block):
        tot = None
        for i in range(pages_per_block):
            hi, lo = _split_bf16(k_refs[n * pages_per_block + i][...])
            part = _dg(ones, hi, _NT) + _dg(ones, lo, _NT)
            tot = part if tot is None else tot + part
        means.append(tot[0:1, :] * scale)
    o_ref[...] = jnp.concatenate(means, axis=0)


def paged_block_means(k_pool, pages, block):
    b, n_pages = pages.shape
    width, page_size = k_pool.shape[1:]
    pages_per_block = block // page_size
    n_pages_step = SUBLANES * pages_per_block
    assert block % page_size == 0 and n_pages % n_pages_step == 0
    return pl.pallas_call(
        functools.partial(_block_mean_kernel, n_pages_step=n_pages_step, pages_per_block=pages_per_block),
        grid_spec=pltpu.PrefetchScalarGridSpec(
            num_scalar_prefetch=1, grid=(b, n_pages // n_pages_step),
            in_specs=_page_specs(n_pages, n_pages_step, width, page_size, reverse=False),
            out_specs=pl.BlockSpec((None, SUBLANES, width), lambda bi, si, pt: (bi, si, 0))),
        out_shape=jax.ShapeDtypeStruct((b, n_pages // pages_per_block, width), F32),
        compiler_params=_cparams("parallel", "arbitrary"),
        name="paged_block_means",
    )(pages, *([k_pool] * n_pages_step))


def _moba_sample_kernel(pt_ref, q_ref, kmean_ref, knew_ref, vnew_ref, *refs, n_pages_step, pages_per_block,
                        t_new, n_heads):
    k_refs = refs[:n_pages_step]
    v_refs = refs[n_pages_step:2 * n_pages_step]
    o_ref, qs_ref, sel_ref, acc_ref, m_ref, l_ref = refs[2 * n_pages_step:]
    step = pl.program_id(1)
    nb = kmean_ref.shape[0]
    rows_total = n_heads * t_new
    blk_col = lax.broadcasted_iota(jnp.int32, (rows_total, nb), 1)

    def tile(k_tile, v_tile, mask, paged):
        s = jnp.where(mask, _dg(qs_ref[...], k_tile.astype(BF16), _NN if paged else _NT), NEG_BIG)
        m_new = jnp.maximum(m_ref[...], jnp.max(s, axis=1, keepdims=True))
        p = jnp.where(mask, jnp.exp(s - m_new), 0.0)
        alpha = jnp.exp(m_ref[...] - m_new)
        l_ref[...] = alpha * l_ref[...] + jnp.sum(p, axis=1, keepdims=True)
        acc_ref[...] = alpha * acc_ref[...] + _mm(p, v_tile, 1, _NT if paged else _NN)
        m_ref[...] = m_new

    @pl.when(step == 0)
    def _():
        qs = _stack_heads(q_ref[...], n_heads)
        qs_ref[...] = qs.astype(BF16)
        gate = _mm(qs, kmean_ref[...], 3, _NT)
        sel_ref[...] = jnp.where(_topk_block_mask(gate, nb, MOBA_TOPK, 1), 1.0, 0.0)
        acc_ref[...] = jnp.zeros_like(acc_ref)
        l_ref[...] = jnp.zeros_like(l_ref)
        m_ref[...] = jnp.full(m_ref.shape, NEG_BIG, F32)
        tk = knew_ref.shape[0]
        rows = lax.broadcasted_iota(jnp.int32, (rows_total, tk), 0) % t_new
        cols = lax.broadcasted_iota(jnp.int32, (rows_total, tk), 1)
        tile(knew_ref[...], vnew_ref[...], cols <= rows, False)

    qs = qs_ref[...]
    sel = sel_ref[...]
    page_size = k_refs[0].shape[1]
    mask, s = [], []
    for i in range(n_pages_step):
        blk = (step * n_pages_step + i) // pages_per_block
        picked = jnp.sum(jnp.where(blk_col == blk, sel, 0.0), axis=1, keepdims=True) > 0.5
        mask.append(jnp.broadcast_to(picked, (rows_total, page_size)))
        s.append(jnp.where(mask[i], _dg(qs, k_refs[i][...].astype(BF16), _NN), NEG_BIG))
    m_old = m_ref[...]
    m_new = m_old
    for x in s:
        m_new = jnp.maximum(m_new, jnp.max(x, axis=1, keepdims=True))
    p = [jnp.where(mask[i], jnp.exp(s[i] - m_new), 0.0) for i in range(n_pages_step)]
    alpha = jnp.exp(m_old - m_new)
    l_new = alpha * l_ref[...]
    acc = alpha * acc_ref[...]
    for i in range(n_pages_step):
        l_new = l_new + jnp.sum(p[i], axis=1, keepdims=True)
        acc = acc + _mm(p[i], v_refs[i][...], 1, _NT)
    l_ref[...] = l_new
    acc_ref[...] = acc
    m_ref[...] = m_new

    @pl.when(step == pl.num_programs(1) - 1)
    def _():
        o_ref[...] = _unstack_heads(acc_ref[...] / l_ref[...], n_heads)


def moba_sample(q, k_new, v_new, k_pool, v_pool, pages, kmean, n_pages_step):
    b, t_new, width = q.shape
    n_heads = width // HEAD_DIM
    page_size = k_pool.shape[2]
    n_pages = pages.shape[1]
    nb = kmean.shape[1]
    pages_per_block = MOBA_BLOCK // page_size
    assert n_pages % n_pages_step == 0 and t_new <= page_size and n_pages == nb * pages_per_block
    assert (n_pages * page_size) % MOBA_BLOCK == 0 and t_new <= MOBA_BLOCK
    new = pl.BlockSpec((None, t_new, width), lambda bi, si, pt: (bi, 0, 0))
    new_pad = pl.BlockSpec((None, page_size, width), lambda bi, si, pt: (bi, 0, 0))
    page_specs = _page_specs(n_pages, n_pages_step, width, page_size, reverse=False)
    rows_total = n_heads * t_new
    return pl.pallas_call(
        functools.partial(_moba_sample_kernel, n_pages_step=n_pages_step, pages_per_block=pages_per_block,
                          t_new=t_new, n_heads=n_heads),
        grid_spec=pltpu.PrefetchScalarGridSpec(
            num_scalar_prefetch=1, grid=(b, n_pages // n_pages_step),
            in_specs=[new, pl.BlockSpec((None, nb, width), lambda bi, si, pt: (bi, 0, 0)), new_pad, new_pad]
                     + page_specs + page_specs,
            out_specs=new,
            scratch_shapes=[pltpu.VMEM((rows_total, width), BF16), pltpu.VMEM((rows_total, nb), F32),
                            pltpu.VMEM((rows_total, width), F32), pltpu.VMEM((rows_total, 1), F32),
                            pltpu.VMEM((rows_total, 1), F32)]),
        out_shape=jax.ShapeDtypeStruct((b, t_new, width), F32),
        compiler_params=_cparams("parallel", "arbitrary"),
        name="moba_sample",
    )(pages, q, kmean, _pad_rows(k_new, page_size), _pad_rows(v_new, page_size),
      *([k_pool] * n_pages_step), *([v_pool] * n_pages_step))


def _merge_kernel(x_ref, g_ref, wg_ref, y0_ref, y1_ref, y2_ref, y3_ref, wb_ref, wo_ref, o_ref, *, d_model, layouts):
    x = x_ref[...]
    h = _rms(x, g_ref[...]).astype(BF16)
    merged = None
    for n, (y_ref, layout) in enumerate(zip((y0_ref, y1_ref, y2_ref, y3_ref), layouts)):
        gate = jax.nn.sigmoid(_dg(h, wg_ref[:, n * d_model:(n + 1) * d_model], _NN))
        term = gate * _mm(y_ref[...], wb_ref[n], 1, _TN if layout == "transposed" else _NN)
        merged = term if merged is None else merged + term
    o_ref[...] = x + _mm(merged, wo_ref[...])


def merge(x, g, w_gate, ys, layouts, w_branch, w_out, tm):
    b, t, d = x.shape
    width = w_branch.shape[1]
    tm = min(tm, t)
    assert t % tm == 0
    fixed = lambda bi, ti: (0, 0)
    seq = lambda bi, ti: (bi, ti, 0)
    y_spec = {"rows": pl.BlockSpec((None, tm, width), seq),
              "time_major": pl.BlockSpec((tm, width), lambda bi, ti: (ti, bi)),
              "transposed": pl.BlockSpec((None, width, tm), lambda bi, ti: (bi, 0, ti))}
    return pl.pallas_call(
        functools.partial(_merge_kernel, d_model=d, layouts=tuple(layouts)),
        grid=(b, t // tm),
        in_specs=[pl.BlockSpec((None, tm, d), seq), pl.BlockSpec((1, d), fixed),
                  pl.BlockSpec((d, N_BRANCH * d), fixed)]
                 + [y_spec[k] for k in layouts]
                 + [pl.BlockSpec((N_BRANCH, width, d), lambda bi, ti: (0, 0, 0)), pl.BlockSpec((d, d), fixed)],
        out_specs=pl.BlockSpec((None, tm, d), seq),
        out_shape=jax.ShapeDtypeStruct((b, t, d), F32),
        compiler_params=_cparams("parallel", "parallel"),
        name="merge",
    )(x, g.reshape(1, d), w_gate, *ys, w_branch, w_out)


def _ffn_kernel(x_ref, g_ref, wup_ref, cw_ref, cb_ref, wdn_ref, conv0_ref, o_ref, conv_ref, st_ref,
                *, nseq, tt, d_ff, col_chunk):
    ti = pl.program_id(1)

    @pl.when(ti == 0)
    def _():
        st_ref[...] = conv0_ref[...]

    d_model = x_ref.shape[-1]
    x = x_ref[...].reshape(nseq * tt, d_model)
    h = _rms(x, g_ref[...]).astype(BF16)
    t_idx = lax.broadcasted_iota(jnp.int32, (1, tt, 1), 1)

    def conv_cols(c0):
        cols = slice(c0, c0 + col_chunk)
        up = _dg(h, wup_ref[:, cols], _NN).reshape(nseq, tt, col_chunk)
        st = st_ref[:, :, cols]
        p1 = jnp.where(t_idx == 0, st[:, 1:2, :], pltpu.roll(up, 1, axis=1))
        p2 = jnp.where(t_idx == 0, st[:, 0:1, :], jnp.where(t_idx == 1, st[:, 1:2, :], pltpu.roll(up, 2, axis=1)))
        st_ref[:, :, cols] = up[:, tt - 2:, :]
        cw = cw_ref[:, cols]
        c = cb_ref[:, cols] + cw[2:3, :] * up + cw[1:2, :] * p1 + cw[0:1, :] * p2
        return c.reshape(nseq * tt, col_chunk)

    acc = x
    for c0 in range(0, d_ff, col_chunk):
        a = conv_cols(c0)
        b = conv_cols(d_ff + c0)
        acc = acc + _mm(a * jax.nn.sigmoid(a) * b, wdn_ref[c0:c0 + col_chunk, :])
    o_ref[...] = acc.reshape(nseq, tt, d_model)
    conv_ref[...] = st_ref[...]


def conv_ffn(x, g, w_up, conv_w, conv_b, w_down, conv0, nseq, tt, col_chunk):
    b, t, d = x.shape
    d_ff = w_down.shape[0]
    assert b % nseq == 0 and t % tt == 0 and d_ff % col_chunk == 0 and tt >= CONV_W - 1
    fixed = lambda bi, ti: (0, 0)
    seq = lambda bi, ti: (bi, ti, 0)
    st = lambda bi, ti: (bi, 0, 0)
    return pl.pallas_call(
        functools.partial(_ffn_kernel, nseq=nseq, tt=tt, d_ff=d_ff, col_chunk=col_chunk),
        grid=(b // nseq, t // tt),
        in_specs=[pl.BlockSpec((nseq, tt, d), seq), pl.BlockSpec((1, d), fixed),
                  pl.BlockSpec((d, 2 * d_ff), fixed), pl.BlockSpec((CONV_W, 2 * d_ff), fixed),
                  pl.BlockSpec((1, 2 * d_ff), fixed), pl.BlockSpec((d_ff, d), fixed),
                  pl.BlockSpec((nseq, CONV_W - 1, 2 * d_ff), st)],
        out_specs=[pl.BlockSpec((nseq, tt, d), seq), pl.BlockSpec((nseq, CONV_W - 1, 2 * d_ff), st)],
        out_shape=[jax.ShapeDtypeStruct((b, t, d), F32), jax.ShapeDtypeStruct((b, CONV_W - 1, 2 * d_ff), F32)],
        scratch_shapes=[pltpu.VMEM((nseq, CONV_W - 1, 2 * d_ff), F32)],
        compiler_params=_cparams("parallel", "arbitrary"),
        name="conv_ffn",
    )(x, g.reshape(1, d), w_up, conv_w, conv_b.reshape(1, 2 * d_ff), w_down, conv0)


PROMPT_TILES = dict(proj_tm=512, s5_tt=128, s5_passes=1, rwkv_tt=256, rwkv_chunk=64, rwkv_passes=1,
                    sb_tq=256, merge_tm=256, ffn_tt=256, ffn_cols=1408)
SAMPLE_TILES = dict(proj_tm=256, s5_tt=8, s5_passes=3, rwkv_tt=128, rwkv_chunk=64, rwkv_passes=1,
                    merge_tm=256, ffn_cols=1408, sb_pages=8, moba_pages=8)


def _layer(x, st, past, p, tiles):
    b, t, d = x.shape
    width = d // 2
    n_heads = width // HEAD_DIM
    n_rwkv = 3 * width + RWKV_DECAY_LORA + RWKV_ICLR_LORA + RWKV_GATE_LORA
    off_sb = width + n_rwkv
    off_mb = off_sb + 3 * width
    off_gate = off_mb + 3 * width
    w_in = p['w_in']
    prompt = past is None
    n_state = p['s5_ab_re'].shape[0]
    s5_args = (st['s5_re'].reshape(b, n_state), st['s5_im'].reshape(b, n_state), p['s5_ab_re'], p['s5_ab_im'],
               p['s5_bb'][tiles['s5_passes']], p['s5_cc'][tiles['s5_passes']], p['s5_d'], p['s5_w_glu'],
               tiles['s5_tt'], tiles['s5_passes'])

    if prompt:
        u_s5, u_rw = proj_plain(x, p['norm1_g'], w_in[:, :off_sb], (width, n_rwkv), tiles['proj_tm'], True)
        y_s5, s5_re, s5_im = s5_mixer(u_s5.reshape(t, b, width), *s5_args)
        y_s5 = y_s5.reshape(t, b * width)
        qt, k_sb, v_sb, k_rows, vt_b, _ = proj_qkv_t(x, p['norm1_g'], p['w_sb_t'], p['sb_q_g'], p['sb_k_g'],
                                                     tiles['proj_tm'])
        y_sb = sb_prompt(qt, k_rows, vt_b, tiles['sb_tq'])
        qt, k_mb, v_mb, k_rows, vt_b, kmean = proj_qkv_t(x, p['norm1_g'], p['w_mb_t'], p['moba_q_g'],
                                                         p['moba_k_g'], tiles['proj_tm'])
        y_mb = moba_prompt(qt, k_rows, vt_b, kmean)
        layouts = ("time_major", "rows", "transposed", "transposed")
        x_m = x
        kv = lambda a: a.reshape(b, n_heads, HEAD_DIM, t).transpose(0, 3, 1, 2)
    else:
        x_m = x.reshape(1, b * t, d)
        u_s5, u_rw = proj_plain(x_m, p['norm1_g'], w_in[:, :off_sb], (width, n_rwkv), b * t, False)
        y_s5, s5_re, s5_im = s5_mixer(u_s5.reshape(b, t, width).transpose(1, 0, 2), *s5_args)
        y_s5 = y_s5.transpose(1, 0, 2).reshape(1, b * t, width)
        x2 = x.reshape(b * t, d)
        seq = lambda a: a.reshape(b, t, width)
        q_sb, k_sb, v_sb = proj_qkv(x2, p['norm1_g'], w_in[:, off_sb:off_mb], p['sb_q_g'], p['sb_k_g'], b * t)
        q_mb, k_mb, v_mb = proj_qkv(x2, p['norm1_g'], w_in[:, off_mb:off_gate], p['moba_q_g'], p['moba_k_g'], b * t)
        y_sb = sb_sample(seq(q_sb), seq(k_sb), seq(v_sb), past['sb_k'], past['sb_v'], past['pages'],
                         tiles['sb_pages'])
        kmean = paged_block_means(past['moba_k'], past['pages'], MOBA_BLOCK)
        y_mb = moba_sample(seq(q_mb), seq(k_mb), seq(v_mb), past['moba_k'], past['moba_v'], past['pages'], kmean,
                           tiles['moba_pages'])
        y_sb, y_mb = y_sb.reshape(1, b * t, width), y_mb.reshape(1, b * t, width)
        layouts = ("rows",) * N_BRANCH
        kv = lambda a: a.reshape(b, t, n_heads, HEAD_DIM)

    y_rw, wkv, shift = rwkv_mixer(u_rw.reshape(b, t, n_rwkv), st['shift'], st['wkv'], p, tiles['rwkv_tt'],
                                  tiles['rwkv_chunk'], tiles['rwkv_passes'])
    ys = [y_s5, y_rw.reshape(x_m.shape[0], x_m.shape[1], width), y_sb, y_mb]
    x1 = merge(x_m, p['norm1_g'], w_in[:, off_gate:], ys, layouts, p['w_branch'], p['w_out'], tiles['merge_tm'])
    nseq, tt = (1, tiles['ffn_tt']) if prompt else (b, t)
    x_out, conv = conv_ffn(x1.reshape(b, t, d), p['norm2_g'], p['ffn_w_up'], p['ffn_conv_w'], p['ffn_conv_b'],
                           p['ffn_w_down'], st['conv'], nseq, tt, tiles['ffn_cols'])
    new_st = (s5_re.reshape(st['s5_re'].shape), s5_im.reshape(st['s5_im'].shape), wkv, shift, conv,
              kv(k_sb), kv(v_sb), kv(k_mb), kv(v_mb))
    return x_out, new_st


def kernel(x_prompt, x_sample, state_s5_re, state_s5_im, state_rwkv_wkv, state_rwkv_shift, state_ffn_conv, cache_sb_k, cache_sb_v, cache_moba_k, cache_moba_v, page_table, norm1_g, w_in, s5_a_re, s5_a_im, s5_log_dt, s5_b_re, s5_b_im, s5_c_re, s5_c_im, s5_d, s5_w_glu, rwkv_mu, rwkv_w0, rwkv_w_up, rwkv_a0, rwkv_a_up, rwkv_g_up, rwkv_k_k, rwkv_k_a, rwkv_r_k, rwkv_ln_w, rwkv_ln_b, sb_q_g, sb_k_g, moba_q_g, moba_k_g, w_branch, w_out, norm2_g, ffn_w_up, ffn_conv_w, ffn_conv_b, ffn_w_down):
    depth = w_in.shape[0]
    bp = x_prompt.shape[0]
    n_pool, page_size, n_heads, head_dim = cache_sb_k.shape[1:]
    width = n_heads * head_dim
    pool = lambda c: c.transpose(0, 1, 3, 4, 2).reshape(depth * n_pool, width, page_size)
    pools = dict(sb_k=pool(cache_sb_k), sb_v=pool(cache_sb_v), moba_k=pool(cache_moba_k), moba_v=pool(cache_moba_v))
    g_s5, p_s5 = s5_a_re.shape[1:]
    n_rwkv = state_rwkv_shift.shape[-1]
    d_ff2 = state_ffn_conv.shape[-1]
    zero_st = dict(s5_re=jnp.zeros((bp, g_s5, p_s5), F32), s5_im=jnp.zeros((bp, g_s5, p_s5), F32),
                   shift=jnp.zeros((bp, n_rwkv), F32), wkv=jnp.zeros((bp, n_heads, head_dim, head_dim), F32),
                   conv=jnp.zeros((bp, CONV_W - 1, d_ff2), F32))
    yp, ys = x_prompt, x_sample
    states_p, states_s = [], []
    for l in range(depth):
        ab_re, ab_im, bb, cc = s5_operands(s5_a_re[l], s5_a_im[l], s5_log_dt[l], s5_b_re[l], s5_b_im[l],
                                           s5_c_re[l], s5_c_im[l])
        w_l = w_in[l].astype(BF16)
        width_l = w_l.shape[0] // 2
        off_sb = width_l + state_rwkv_shift.shape[-1]
        p = dict(norm1_g=norm1_g[l], w_in=w_l, w_sb_t=w_l[:, off_sb:off_sb + 3 * width_l].T,
                 w_mb_t=w_l[:, off_sb + 3 * width_l:off_sb + 6 * width_l].T, s5_ab_re=ab_re, s5_ab_im=ab_im,
                 s5_bb={1: bb.astype(BF16), 3: bb}, s5_cc={1: cc.astype(BF16), 3: cc}, s5_d=s5_d[l],
                 s5_w_glu=s5_w_glu[l].astype(BF16), rwkv_mu=rwkv_mu[l], rwkv_w0=rwkv_w0[l],
                 rwkv_w_up=rwkv_w_up[l], rwkv_a0=rwkv_a0[l], rwkv_a_up=rwkv_a_up[l], rwkv_g_up=rwkv_g_up[l],
                 rwkv_k_k=rwkv_k_k[l], rwkv_k_a=rwkv_k_a[l], rwkv_r_k=rwkv_r_k[l].reshape(-1),
                 rwkv_ln_w=rwkv_ln_w[l], rwkv_ln_b=rwkv_ln_b[l], sb_q_g=sb_q_g[l], sb_k_g=sb_k_g[l],
                 moba_q_g=moba_q_g[l], moba_k_g=moba_k_g[l], w_branch=w_branch[l].astype(BF16),
                 w_out=w_out[l].astype(BF16), norm2_g=norm2_g[l], ffn_w_up=ffn_w_up[l].astype(BF16),
                 ffn_conv_w=ffn_conv_w[l], ffn_conv_b=ffn_conv_b[l], ffn_w_down=ffn_w_down[l].astype(BF16))
        yp, st_p = _layer(yp, zero_st, None, p, PROMPT_TILES)
        st_s = dict(s5_re=state_s5_re[l], s5_im=state_s5_im[l], shift=state_rwkv_shift[l], wkv=state_rwkv_wkv[l],
                    conv=state_ffn_conv[l])
        past = dict(pools, pages=page_table + l * n_pool)
        ys, st_s = _layer(ys, st_s, past, p, SAMPLE_TILES)
        states_p.append(st_p)
        states_s.append(st_s)
    stacked_p = [jnp.stack(z, axis=0) for z in zip(*states_p)]
    stacked_s = [jnp.stack(z, axis=0) for z in zip(*states_s)]
    out = [yp, ys]
    for a, c in zip(stacked_p, stacked_s):
        out += [a, c]
    return tuple(out)
```

```python
import functools
import math

import jax
import jax.numpy as jnp
from jax import lax
from jax.experimental import pallas as pl
from jax.experimental.pallas import tpu as pltpu

F32 = jnp.float32
BF16 = jnp.bfloat16

HEAD_DIM = 64
S5_GROUP = 16
S5_STATE = 64
RWKV_DECAY_LORA = 64
RWKV_ICLR_LORA = 64
RWKV_GATE_LORA = 128
N_BRANCH = 4
MOBA_BLOCK = 256
MOBA_TOPK = 3
CONV_W = 3
EPS = 1e-6
RWKV_GN_EPS = 64e-5
RWKV_MAX_CHUNK = 64

V7X_VMEM_LIMIT_BYTES = 56 * 1024 * 1024
SUBLANES = 8
LANES = 128


def _cparams(*sem):
    return pltpu.CompilerParams(dimension_semantics=sem, vmem_limit_bytes=V7X_VMEM_LIMIT_BYTES)


def _split_bf16(x):
    hi = x.astype(BF16)
    lo = (x - hi.astype(F32)).astype(BF16)
    return hi, lo


_NN = (((1,), (0,)), ((), ()))
_NT = (((1,), (1,)), ((), ()))


def _dg(a, b, dims):
    return lax.dot_general(a, b, dims, preferred_element_type=F32)


def _mm(a, b, passes=1, dims=_NN):
    if passes == 1:
        return _dg(a.astype(BF16), b.astype(BF16), dims)
    ah, al = _split_bf16(a)
    if b.dtype == BF16:
        return _dg(ah, b, dims) + _dg(al, b, dims)
    bh, bl = _split_bf16(b)
    if passes == 2:
        return _dg(ah, bh, dims) + _dg(al, bh, dims)
    return _dg(ah, bh, dims) + (_dg(al, bh, dims) + _dg(ah, bl, dims))


def _head_block_matrix(width, value):
    r = jnp.arange(width) // HEAD_DIM
    return jnp.where(r[:, None] == r[None, :], value, 0.0).astype(BF16)


def _rms(x, g):
    return x * lax.rsqrt(jnp.mean(x * x, axis=-1, keepdims=True) + EPS) * g


def _proj_plain_kernel(x_ref, g_ref, w_ref, *out_refs, widths):
    h = _rms(x_ref[...], g_ref[...]).astype(BF16)
    u = _dg(h, w_ref[...], _NN)
    off = 0
    for o_ref, n in zip(out_refs, widths):
        o_ref[...] = u[:, off:off + n]
        off += n


def proj_plain(x, g, w, widths, tm, first_time_major):
    b, t, d = x.shape
    n = w.shape[1]
    assert sum(widths) == n and t % tm == 0
    fixed = lambda bi, ti: (0, 0)
    seq = lambda bi, ti: (bi, ti, 0)
    out_specs = [pl.BlockSpec((None, tm, k), seq) for k in widths]
    out_shape = [jax.ShapeDtypeStruct((b, t, k), F32) for k in widths]
    if first_time_major:
        out_specs[0] = pl.BlockSpec((tm, widths[0]), lambda bi, ti: (ti, bi))
        out_shape[0] = jax.ShapeDtypeStruct((t, b * widths[0]), F32)
    return pl.pallas_call(
        functools.partial(_proj_plain_kernel, widths=widths),
        grid=(b, t // tm),
        in_specs=[pl.BlockSpec((None, tm, d), seq), pl.BlockSpec((1, d), fixed), pl.BlockSpec((d, n), fixed)],
        out_specs=out_specs,
        out_shape=out_shape,
        compiler_params=_cparams("parallel", "parallel"),
        name="proj_plain",
    )(x, g.reshape(1, d), w)


def _head_rms(x, hm, g):
    ms = _mm(x * x, hm, passes=2)
    return x * lax.rsqrt(ms + EPS) * g


def _proj_qkv_kernel(x_ref, g_ref, w_ref, hm_ref, gq_ref, gk_ref, q_ref, k_ref, v_ref, *, width, q_scale):
    h = _rms(x_ref[...], g_ref[...]).astype(BF16)
    u = _dg(h, w_ref[...], _NN)
    hm = hm_ref[...]
    q_ref[...] = _head_rms(u[:, :width], hm, gq_ref[...]) * q_scale
    k_ref[...] = _head_rms(u[:, width:2 * width], hm, gk_ref[...])
    v_ref[...] = u[:, 2 * width:]


def proj_qkv(x2d, g, w, g_q, g_k, tm, q_scale):
    m, d = x2d.shape
    width = w.shape[1] // 3
    n_heads = width // HEAD_DIM
    hm = _head_block_matrix(width, 1.0 / HEAD_DIM)
    row = lambda i: (i, 0)
    fixed = lambda i: (0, 0)
    return pl.pallas_call(
        functools.partial(_proj_qkv_kernel, width=width, q_scale=q_scale),
        grid=(m // tm,),
        in_specs=[pl.BlockSpec((tm, d), row), pl.BlockSpec((1, d), fixed),
                  pl.BlockSpec((d, 3 * width), fixed), pl.BlockSpec((width, width), fixed),
                  pl.BlockSpec((1, width), fixed), pl.BlockSpec((1, width), fixed)],
        out_specs=[pl.BlockSpec((tm, width), row)] * 3,
        out_shape=[jax.ShapeDtypeStruct((m, width), F32)] * 3,
        compiler_params=_cparams("parallel"),
        name="proj_qkv",
    )(x2d, g.reshape(1, d), w, hm, jnp.tile(g_q, n_heads).reshape(1, width),
      jnp.tile(g_k, n_heads).reshape(1, width))


def _s5_disc_kernel(are_ref, aim_ref, ldt_ref, bre_ref, bim_ref, abre_ref, abim_ref, bbre_ref, bbim_ref):
    lr = are_ref[...]
    li = aim_ref[...]
    dt = jnp.exp(ldt_ref[...])
    mag = jnp.exp(lr * dt)
    ab_re = mag * jnp.cos(li * dt)
    ab_im = mag * jnp.sin(li * dt)
    den = lr * lr + li * li
    zr = ((ab_re - 1.0) * lr + ab_im * li) / den
    zi = (ab_im * lr - (ab_re - 1.0) * li) / den
    abre_ref[...] = ab_re
    abim_ref[...] = ab_im
    br = bre_ref[...]
    bi = bim_ref[...]
    bbre_ref[...] = zr[:, None, :] * br - zi[:, None, :] * bi
    bbim_ref[...] = zr[:, None, :] * bi + zi[:, None, :] * br


def s5_discretize(a_re, a_im, log_dt, b_re, b_im):
    g, p = a_re.shape
    gc = b_re.shape[1]
    return pl.pallas_call(
        _s5_disc_kernel,
        out_shape=[jax.ShapeDtypeStruct((g, p), F32)] * 2 + [jax.ShapeDtypeStruct((g, gc, p), F32)] * 2,
        name="s5_discretize",
    )(a_re, a_im, log_dt.reshape(g, 1), b_re, b_im)


def _gelu_tanh(x):
    c = math.sqrt(2.0 / math.pi)
    return 0.5 * x * (1.0 + jnp.tanh(c * (x + 0.044715 * (x * x * x))))


def _s5_kernel(u_ref, s0re_ref, s0im_ref, abre_ref, abim_ref, bb_ref, cc_ref, d_ref, wglu_ref,
               y_ref, sre_ref, sim_ref, xs_ref, cre_ref, cim_ref, *, tt, n_state, passes, lane_chunk):
    ti = pl.program_id(1)

    @pl.when(ti == 0)
    def _():
        cre_ref[...] = s0re_ref[...]
        cim_ref[...] = s0im_ref[...]

    width = u_ref.shape[-1]
    n_cl, c_in, c_state = bb_ref.shape
    half = c_state // 2
    u = u_ref[...].reshape(tt * SUBLANES, width)
    for c in range(n_cl):
        xc = _mm(u[:, c * c_in:(c + 1) * c_in], bb_ref[c], passes)
        xs_ref[:, c * half:(c + 1) * half] = xc[:, :half]
        xs_ref[:, n_state + c * half:n_state + (c + 1) * half] = xc[:, half:]

    for c0 in range(0, n_state, lane_chunk):
        re_sl = pl.ds(c0, lane_chunk)
        im_sl = pl.ds(n_state + c0, lane_chunk)
        a_re = jnp.broadcast_to(abre_ref[:, re_sl], (SUBLANES, lane_chunk))
        a_im = jnp.broadcast_to(abim_ref[:, re_sl], (SUBLANES, lane_chunk))

        def step(t, carry):
            s_re, s_im = carry
            rows = pl.ds(pl.multiple_of(t * SUBLANES, SUBLANES), SUBLANES)
            n_re = a_re * s_re - a_im * s_im + xs_ref[rows, re_sl]
            n_im = a_re * s_im + a_im * s_re + xs_ref[rows, im_sl]
            xs_ref[rows, re_sl] = n_re
            xs_ref[rows, im_sl] = n_im
            return n_re, n_im

        s_re, s_im = lax.fori_loop(0, tt, step, (cre_ref[:, re_sl], cim_ref[:, re_sl]), unroll=8)
        cre_ref[:, re_sl] = s_re
        cim_ref[:, re_sl] = s_im

    y = jnp.concatenate(
        [_mm(xs_ref[:, c * half:(c + 1) * half], cc_ref[c, :half, :], passes)
         + _mm(xs_ref[:, n_state + c * half:n_state + (c + 1) * half], cc_ref[c, half:, :], passes)
         for c in range(n_cl)], axis=1) + d_ref[...] * u
    y = _gelu_tanh(y)
    y = y * jax.nn.sigmoid(_mm(y, wglu_ref[...]))
    y_ref[...] = y.reshape(tt, SUBLANES, width)
    sre_ref[...] = cre_ref[...]
    sim_ref[...] = cim_ref[...]


def s5_mixer(u_tm, s0_re, s0_im, ab_re, ab_im, bb, cc, d_skip, w_glu, tt, passes):
    t, b, width = u_tm.shape
    n_state = ab_re.shape[-1]
    assert t % tt == 0 and b % SUBLANES == 0
    fixed = lambda bi, ti: (0, 0)
    st = lambda bi, ti: (bi, 0)
    return pl.pallas_call(
        functools.partial(_s5_kernel, tt=tt, n_state=n_state, passes=passes, lane_chunk=512),
        grid=(b // SUBLANES, t // tt),
        in_specs=[pl.BlockSpec((tt, SUBLANES, width), lambda bi, ti: (ti, bi, 0)),
                  pl.BlockSpec((SUBLANES, n_state), st), pl.BlockSpec((SUBLANES, n_state), st),
                  pl.BlockSpec((1, n_state), fixed), pl.BlockSpec((1, n_state), fixed),
                  pl.BlockSpec(bb.shape, lambda bi, ti: (0, 0, 0)), pl.BlockSpec(cc.shape, lambda bi, ti: (0, 0, 0)),
                  pl.BlockSpec((1, width), fixed), pl.BlockSpec((width, width), fixed)],
        out_specs=[pl.BlockSpec((tt, SUBLANES, width), lambda bi, ti: (ti, bi, 0)),
                   pl.BlockSpec((SUBLANES, n_state), st), pl.BlockSpec((SUBLANES, n_state), st)],
        out_shape=[jax.ShapeDtypeStruct((t, b, width), F32),
                   jax.ShapeDtypeStruct((b, n_state), F32), jax.ShapeDtypeStruct((b, n_state), F32)],
        scratch_shapes=[pltpu.VMEM((tt * SUBLANES, 2 * n_state), F32),
                        pltpu.VMEM((SUBLANES, n_state), F32), pltpu.VMEM((SUBLANES, n_state), F32)],
        compiler_params=_cparams("parallel", "arbitrary"),
        name="s5_mixer",
    )(u_tm, s0_re, s0_im, ab_re.reshape(1, n_state), ab_im.reshape(1, n_state), bb, cc,
      d_skip.reshape(1, width), w_glu)


def s5_operands(a_re, a_im, log_dt, b_re, b_im, c_re, c_im):
    g, p = a_re.shape
    gc = b_re.shape[-1]
    ab_re, ab_im, bb_re, bb_im = s5_discretize(a_re, a_im, log_dt, b_re.transpose(0, 2, 1),
                                               b_im.transpose(0, 2, 1))
    cl = LANES // gc
    assert g % cl == 0
    eye = jnp.eye(cl, dtype=F32)
    split = lambda m: m.reshape(g // cl, cl, gc, p)
    bd = lambda m: jnp.einsum('kacp,ab->kacbp', split(m), eye).reshape(g // cl, cl * gc, cl * p)
    bb = jnp.concatenate([bd(bb_re), bd(bb_im)], axis=2)
    bdc = lambda m: jnp.einsum('kacp,ab->kbpac', split(m), eye).reshape(g // cl, cl * p, cl * gc)
    cc = jnp.concatenate([bdc(c_re), -bdc(c_im)], axis=1)
    return ab_re.reshape(g * p), ab_im.reshape(g * p), bb, cc


def _softplus(x):
    return jnp.maximum(x, 0.0) + jnp.log(1.0 + jnp.exp(-jnp.abs(x)))


def _softplus2(x):
    return jnp.maximum(x, 0.0) + jnp.log2(1.0 + jnp.exp2(-jnp.abs(x)))


LOG2_E = math.log2(math.e)


def _rwkv_kernel(u_ref, shift0_ref, wkv0_ref, mu_ref, w0_ref, wup_ref, a0_ref, aup_ref, gup_ref, kk_ref,
                 ka_ref, rk_ref, lnw_ref, lnb_ref, hsum_ref, y_ref, wkv_ref, shift_ref,
                 prev_ref, s_ref, yt_ref, *, tt, chunk, width, passes, t_valid):
    ti = pl.program_id(1)
    n_heads = width // HEAD_DIM

    @pl.when(ti == 0)
    def _():
        prev_ref[...] = shift0_ref[...]
        s_ref[...] = wkv0_ref[...]

    u = u_ref[...]
    row = lax.broadcasted_iota(jnp.int32, (tt, 1), 0)
    prev = jnp.where(row == 0, prev_ref[...], pltpu.roll(u, 1, axis=0))
    xs = u + mu_ref[...] * (prev - u)
    prev_ref[...] = u[t_valid - 1:t_valid, :]
    shift_ref[...] = u[t_valid - 1:t_valid, :]

    w3 = 3 * width
    o1 = w3 + RWKV_DECAY_LORA
    o2 = o1 + RWKV_ICLR_LORA
    r = xs[:, :width]
    k = xs[:, width:2 * width]
    v = xs[:, 2 * width:w3]
    w_log = -_softplus(-(w0_ref[...] + _mm(jnp.tanh(xs[:, w3:o1]), wup_ref[...], 3))) - 0.5
    log_decay = -jnp.exp(w_log)
    a = jax.nn.sigmoid(a0_ref[...] + _mm(xs[:, o1:o2], aup_ref[...], 3))
    g = _mm(jax.nn.sigmoid(xs[:, o2:]), gup_ref[...], 3)
    hsum = hsum_ref[...]
    kk = k * kk_ref[...]
    kk = kk / jnp.maximum(jnp.sqrt(_mm(kk * kk, hsum, 2)), 1e-12)
    k2 = k * (1.0 + (a - 1.0) * ka_ref[...])
    a_vec = -kk
    b_vec = kk * a
    if t_valid < tt:
        live = row < t_valid
        log_decay = jnp.where(live, log_decay, 0.0)
        a_vec = jnp.where(live, a_vec, 0.0)
        b_vec = jnp.where(live, b_vec, 0.0)
        k2 = jnp.where(live, k2, 0.0)
    vt = v.T

    ri = lax.broadcasted_iota(jnp.int32, (chunk, chunk), 0)
    ci = lax.broadcasted_iota(jnp.int32, (chunk, chunk), 1)
    upper_strict = ri < ci
    upper_incl = ri <= ci
    lower_incl = (ri >= ci).astype(F32)
    n_double = (min(chunk, t_valid) - 1).bit_length()

    heads = range(n_heads)
    hsl = [slice(h * HEAD_DIM, (h + 1) * HEAD_DIM) for h in heads]
    eye = (ri == ci).astype(F32)

    pre = []
    for c in range(tt // chunk):
        rows = slice(c * chunk, (c + 1) * chunk)
        ld = log_decay[rows]
        cl = _mm(lower_incl, ld, 3)
        e_incl = jnp.exp(cl)
        e_inv = jnp.exp(-cl)
        rt = r[rows] * e_incl
        at = a_vec[rows] * jnp.exp(cl - ld)
        bt = b_vec[rows] * e_inv
        kt = k2[rows] * e_inv
        bk = [jnp.concatenate([bt[:, hs], kt[:, hs]], axis=0) for hs in hsl]
        ar = [jnp.concatenate([at[:, hs], rt[:, hs]], axis=0) for hs in hsl]
        gram = [_mm(bk[h], ar[h], passes, _NT) for h in heads]
        n_t = [jnp.where(upper_strict, gm[:chunk, :chunk], 0.0) for gm in gram]
        m2_t = [jnp.where(upper_strict, gm[chunk:, :chunk], 0.0) for gm in gram]
        m34_t = [jnp.concatenate([jnp.where(upper_incl, gm[:chunk, chunk:], 0.0),
                                  jnp.where(upper_incl, gm[chunk:, chunk:], 0.0)], axis=0) for gm in gram]
        vt_c = [vt[hs, rows] for hs in hsl]
        vm = [_mm(vt_c[h], m2_t[h], passes) for h in heads]
        t_inv = [eye + n for n in n_t]
        nk = n_t
        for _ in range(1, n_double):
            nk = [_mm(n, n, passes) for n in nk]
            t_inv = [t_inv[h] + _mm(t_inv[h], nk[h], passes) for h in heads]
        pre.append((rows, bk, ar, m34_t, vt_c, vm, t_inv, e_incl[chunk - 1:chunk, :]))

    state = [s_ref[h] for h in heads]
    for rows, bk, ar, m34_t, vt_c, vm, t_inv, p_end in pre:
        sa = [_mm(state[h], ar[h], passes, _NT) for h in heads]
        x = [_mm(sa[h][:, :chunk] + vm[h], t_inv[h], passes) for h in heads]
        uv = [jnp.concatenate([x[h], vt_c[h]], axis=1) for h in heads]
        for h in heads:
            yt_ref[hsl[h], rows] = sa[h][:, chunk:] + _mm(uv[h], m34_t[h], passes)
        state = [(state[h] + _mm(uv[h], bk[h], passes)) * p_end[:, hsl[h]] for h in heads]
    for h in heads:
        s_ref[h] = state[h]

    y = yt_ref[...].T
    hmean = hsum * (1.0 / HEAD_DIM)
    yc = y - _mm(y, hmean, 2)
    var = _mm(yc * yc, hmean, 2)
    yn = yc * lax.rsqrt(var + RWKV_GN_EPS) * lnw_ref[...] + lnb_ref[...]
    bonus = _mm(r * k2 * rk_ref[...], hsum, 2) * v
    y_ref[...] = (yn + bonus) * g
    wkv_ref[...] = s_ref[...]


def rwkv_mixer(u, shift0, wkv0, p, tt, chunk, passes):
    b, t_valid, n_in = u.shape
    n_heads = wkv0.shape[1]
    width = n_heads * HEAD_DIM
    if t_valid < tt:
        assert t_valid <= RWKV_MAX_CHUNK
        u = _pad_rows(u, tt)
        chunk = tt
    else:
        assert chunk <= RWKV_MAX_CHUNK
        t_valid = tt
    t = u.shape[1]
    assert t % tt == 0 and tt % chunk == 0
    fixed = lambda bi, ti: (0, 0)
    vec = lambda n: pl.BlockSpec((1, n), fixed)
    args = [p['rwkv_mu'].reshape(1, n_in), p['rwkv_w0'].reshape(1, width), p['rwkv_w_up'],
            p['rwkv_a0'].reshape(1, width), p['rwkv_a_up'], p['rwkv_g_up'], p['rwkv_k_k'].reshape(1, width),
            p['rwkv_k_a'].reshape(1, width), p['rwkv_r_k'].reshape(1, width), p['rwkv_ln_w'].reshape(1, width),
            p['rwkv_ln_b'].reshape(1, width), _head_block_matrix(width, 1.0)]
    specs = [vec(n_in), vec(width), pl.BlockSpec(p['rwkv_w_up'].shape, fixed), vec(width),
             pl.BlockSpec(p['rwkv_a_up'].shape, fixed), pl.BlockSpec(p['rwkv_g_up'].shape, fixed),
             vec(width), vec(width), vec(width), vec(width), vec(width), pl.BlockSpec((width, width), fixed)]
    y, wkv, shift = pl.pallas_call(
        functools.partial(_rwkv_kernel, tt=tt, chunk=chunk, width=width, passes=passes, t_valid=t_valid),
        grid=(b, t // tt),
        in_specs=[pl.BlockSpec((None, tt, n_in), lambda bi, ti: (bi, ti, 0)),
                  pl.BlockSpec((None, 1, n_in), lambda bi, ti: (bi, 0, 0)),
                  pl.BlockSpec((None, n_heads, HEAD_DIM, HEAD_DIM), lambda bi, ti: (bi, 0, 0, 0))] + specs,
        out_specs=[pl.BlockSpec((None, tt, width), lambda bi, ti: (bi, ti, 0)),
                   pl.BlockSpec((None, n_heads, HEAD_DIM, HEAD_DIM), lambda bi, ti: (bi, 0, 0, 0)),
                   pl.BlockSpec((None, 1, n_in), lambda bi, ti: (bi, 0, 0))],
        out_shape=[jax.ShapeDtypeStruct((b, t, width), F32),
                   jax.ShapeDtypeStruct((b, n_heads, HEAD_DIM, HEAD_DIM), F32),
                   jax.ShapeDtypeStruct((b, 1, n_in), F32)],
        scratch_shapes=[pltpu.VMEM((1, n_in), F32), pltpu.VMEM((n_heads, HEAD_DIM, HEAD_DIM), F32),
                        pltpu.VMEM((width, tt), F32)],
        compiler_params=_cparams("parallel", "arbitrary"),
        name="rwkv_mixer",
    )(u, shift0.reshape(b, 1, n_in), wkv0, *args)
    return y[:, :t_valid] if t_valid < tt else y, wkv, shift.reshape(b, n_in)


NEG_BIG = -1e30
_TN = (((0,), (0,)), ((), ()))
HEAD_PAIR = 2 * HEAD_DIM


def _proj_qkv_t_kernel(x_ref, g_ref, wt_ref, gq_ref, gk_ref, qt_ref, kt_ref, vt_ref, krow_ref, vtb_ref, kmean_ref,
                       *, width, q_scale, block):
    ti = pl.program_id(1)
    tm = x_ref.shape[0]
    n_heads = width // HEAD_DIM
    h = _rms(x_ref[...], g_ref[...]).astype(BF16)
    ut = _dg(wt_ref[...], h, _NT)

    def head_rms_t(xt, g_col):
        x3 = xt.reshape(n_heads, HEAD_DIM, tm)
        ms = jnp.mean(x3 * x3, axis=1, keepdims=True)
        return (x3 * lax.rsqrt(ms + EPS) * g_col[None, :, :]).reshape(width, tm)

    qt_ref[...] = head_rms_t(ut[:width], gq_ref[...]) * q_scale
    kt = head_rms_t(ut[width:2 * width], gk_ref[...])
    kt_ref[...] = kt
    vt = ut[2 * width:]
    vt_ref[...] = vt
    vtb_ref[...] = vt.astype(BF16)
    k_rows = kt.T
    krow_ref[...] = k_rows.astype(BF16)
    @pl.when(ti == 0)
    def _():
        kmean_ref[...] = jnp.zeros_like(kmean_ref)

    per_tile = tm // block
    blk = lax.broadcasted_iota(jnp.int32, kmean_ref.shape, 0)
    kmean = kmean_ref[...]
    for n in range(per_tile):
        mean = jnp.sum(k_rows[n * block:(n + 1) * block], axis=0, keepdims=True) * (1.0 / block)
        kmean = jnp.where(blk == ti * per_tile + n, mean, kmean)
    kmean_ref[...] = kmean


def proj_qkv_t(x, g, w_t, g_q, g_k, tm, q_scale, layer, depth, prev_kv):
    b, t, d = x.shape
    width = w_t.shape[0] // 3
    nb = t // MOBA_BLOCK
    assert t % tm == 0 and tm % MOBA_BLOCK == 0
    fixed = lambda bi, ti: (0, 0)
    tr = pl.BlockSpec((None, width, tm), lambda bi, ti: (bi, 0, ti))
    slab = pl.BlockSpec((None, None, width, tm), lambda bi, ti: (layer, bi, 0, ti))
    kernel_fn = functools.partial(_proj_qkv_t_kernel, width=width, q_scale=q_scale, block=MOBA_BLOCK)
    in_specs = [pl.BlockSpec((None, tm, d), lambda bi, ti: (bi, ti, 0)), pl.BlockSpec((1, d), fixed),
                pl.BlockSpec((3 * width, d), fixed), pl.BlockSpec((HEAD_DIM, 1), fixed),
                pl.BlockSpec((HEAD_DIM, 1), fixed)]
    args = [x, g.reshape(1, d), w_t, g_q.reshape(HEAD_DIM, 1), g_k.reshape(HEAD_DIM, 1)]
    aliases = {}
    if prev_kv is not None:
        n_in = len(args)
        args += list(prev_kv)
        in_specs += [pl.BlockSpec(memory_space=pl.ANY)] * 2
        aliases = {n_in: 1, n_in + 1: 2}
        body = kernel_fn
        kernel_fn = lambda *refs: body(*refs[:n_in], *refs[n_in + 2:])
    return pl.pallas_call(
        kernel_fn,
        grid=(b, t // tm),
        in_specs=in_specs,
        out_specs=[tr, slab, slab, pl.BlockSpec((None, tm, width), lambda bi, ti: (bi, ti, 0)), tr,
                   pl.BlockSpec((None, nb, width), lambda bi, ti: (bi, 0, 0))],
        out_shape=[jax.ShapeDtypeStruct((b, width, t), F32)] + [jax.ShapeDtypeStruct((depth, b, width, t), F32)] * 2
                  + [jax.ShapeDtypeStruct((b, t, width), BF16), jax.ShapeDtypeStruct((b, width, t), BF16),
                     jax.ShapeDtypeStruct((b, nb, width), F32)],
        input_output_aliases=aliases,
        compiler_params=_cparams("parallel", "arbitrary"),
        name="proj_qkv_t",
    )(*args)


def _head_pair_queries(qt_ref, h, tq):
    p0 = (h // 2) * HEAD_PAIR
    rows = lax.broadcasted_iota(jnp.int32, (HEAD_PAIR, tq), 0) // HEAD_DIM
    return jnp.where(rows == h % 2, qt_ref[p0:p0 + HEAD_PAIR, :], 0.0)


def _sb_prompt_kernel(qt_ref, k_ref, vt_ref, o_ref, qm_ref, acc_ref, run_ref, *, tq, width):
    qi = pl.program_id(1)
    n_heads = width // HEAD_DIM
    krow = lax.broadcasted_iota(jnp.int32, (tq, tq), 0)
    qcol = lax.broadcasted_iota(jnp.int32, (tq, tq), 1)
    later = jnp.where(qcol > krow, 1.0, 0.0).astype(BF16)
    strict = krow < qcol
    for h in range(n_heads):
        qm_ref[h] = _head_pair_queries(qt_ref, h, tq).astype(BF16)
    acc_ref[...] = jnp.zeros_like(acc_ref)
    run_ref[...] = jnp.zeros_like(run_ref)

    heads = range(n_heads)
    hsl = [slice(h * HEAD_DIM, (h + 1) * HEAD_DIM) for h in heads]

    def block(j, diag):
        ks = pl.ds(pl.multiple_of(j * tq, tq), tq)
        run = run_ref[...]
        z = [_dg(k_ref[ks, (h // 2) * HEAD_PAIR:(h // 2 + 1) * HEAD_PAIR], qm_ref[h], _NN) for h in heads]
        sp = [_softplus2(zh) for zh in z]
        l1m = [jnp.where(strict, -x, 0.0) if diag else -x for x in sp]
        parts = [_split_bf16(x) for x in l1m]
        rest = [_dg(later, hi, _NN) + _dg(later, lo, _NN) + run[h:h + 1, :] for h, (hi, lo) in enumerate(parts)]
        w = [jnp.exp2(z[h] - sp[h] + rest[h]) for h in heads]
        if diag:
            w = [jnp.where(strict, x, 0.0) for x in w]
        pv = [_dg(vt_ref[hsl[h], ks], w[h].astype(BF16), _NN) for h in heads]
        for h in heads:
            acc_ref[hsl[h], :] += pv[h]
        run_ref[...] = jnp.concatenate([rest[h][0:1, :] + l1m[h][0:1, :] for h in heads], axis=0)

    block(qi, True)

    def body(jj, carry):
        block(qi - 1 - jj, False)
        return carry

    lax.fori_loop(0, qi, body, 0)
    o_ref[...] = acc_ref[...]


def _attn_prompt_call(kernel_fn, name, qt, k_rows, vt_b, extra, extra_specs, scratch, tq):
    b, width, t = qt.shape
    n_heads = width // HEAD_DIM
    tile = pl.BlockSpec((None, width, tq), lambda bi, qi: (bi, 0, qi))
    return pl.pallas_call(
        kernel_fn,
        grid=(b, t // tq),
        in_specs=[tile, pl.BlockSpec((None, t, width), lambda bi, qi: (bi, 0, 0)),
                  pl.BlockSpec((None, width, t), lambda bi, qi: (bi, 0, 0))] + extra_specs,
        out_specs=tile,
        out_shape=jax.ShapeDtypeStruct((b, width, t), F32),
        scratch_shapes=[pltpu.VMEM((n_heads, HEAD_PAIR, tq), BF16), pltpu.VMEM((width, tq), F32)] + scratch,
        compiler_params=_cparams("parallel", "arbitrary"),
        name=name,
    )(qt, k_rows, vt_b, *extra)


def sb_prompt(qt, k_rows, vt_b, tq):
    b, width, t = qt.shape
    assert t % tq == 0
    n_heads = width // HEAD_DIM
    return _attn_prompt_call(functools.partial(_sb_prompt_kernel, tq=tq, width=width), "sb_prompt", qt, k_rows,
                             vt_b, [], [], [pltpu.VMEM((n_heads, tq), F32)], tq)


def _topk_block_mask(gate, n_valid, topk, axis):
    nb = gate.shape[axis]
    idx = lax.broadcasted_iota(jnp.int32, gate.shape, axis)
    rank = jnp.zeros(gate.shape, jnp.int32)
    for m in range(nb):
        gm = gate[:, m:m + 1] if axis == 1 else gate[m:m + 1, :]
        beats = (gm > gate) | ((gm == gate) & (m < idx))
        rank = rank + jnp.where(beats & (m < n_valid), 1, 0)
    return (idx < n_valid) & (rank < topk)


def _moba_prompt_kernel(qt_ref, k_ref, vt_ref, kmean_ref, o_ref, qm_ref, acc_ref, sel_ref, m_ref, l_ref,
                        *, tq, width):
    qi = pl.program_id(1)
    n_heads = width // HEAD_DIM
    krow = lax.broadcasted_iota(jnp.int32, (tq, tq), 0)
    qcol = lax.broadcasted_iota(jnp.int32, (tq, tq), 1)
    causal = krow <= qcol
    for h in range(n_heads):
        p0 = (h // 2) * HEAD_PAIR
        q_pair = _head_pair_queries(qt_ref, h, tq)
        qm_ref[h] = q_pair.astype(BF16)
        gate = _mm(kmean_ref[:, p0:p0 + HEAD_PAIR], q_pair, 3)
        sel_ref[h] = jnp.where(_topk_block_mask(gate, qi, MOBA_TOPK, 0), 1.0, 0.0)
    acc_ref[...] = jnp.zeros_like(acc_ref)
    l_ref[...] = jnp.zeros_like(l_ref)
    m_ref[...] = jnp.full(m_ref.shape, NEG_BIG, F32)

    heads = range(n_heads)
    hsl = [slice(h * HEAD_DIM, (h + 1) * HEAD_DIM) for h in heads]

    def block(n, diag):
        ks = pl.ds(pl.multiple_of(n * tq, tq), tq)
        m_old = m_ref[...]
        l_old = l_ref[...]
        s = [_dg(k_ref[ks, (h // 2) * HEAD_PAIR:(h // 2 + 1) * HEAD_PAIR], qm_ref[h], _NN) for h in heads]
        if diag:
            mask = [causal] * n_heads
        else:
            mask = [jnp.broadcast_to(sel_ref[h, pl.ds(n, 1), :] > 0.5, (tq, tq)) for h in heads]
        s = [jnp.where(mask[h], s[h], NEG_BIG) for h in heads]
        m_new = [jnp.maximum(m_old[h:h + 1, :], jnp.max(s[h], axis=0, keepdims=True)) for h in heads]
        p = [jnp.where(mask[h], jnp.exp(s[h] - m_new[h]), 0.0) for h in heads]
        alpha = [jnp.exp(m_old[h:h + 1, :] - m_new[h]) for h in heads]
        l_new = [alpha[h] * l_old[h:h + 1, :] + jnp.sum(p[h], axis=0, keepdims=True) for h in heads]
        pv = [_dg(vt_ref[hsl[h], ks], p[h].astype(BF16), _NN) for h in heads]
        for h in heads:
            acc_ref[hsl[h], :] = alpha[h] * acc_ref[hsl[h], :] + pv[h]
        m_ref[...] = jnp.concatenate(m_new, axis=0)
        l_ref[...] = jnp.concatenate(l_new, axis=0)

    block(qi, True)

    def body(n, carry):
        block(n, False)
        return carry

    lax.fori_loop(0, qi, body, 0)
    for h in range(n_heads):
        hs = slice(h * HEAD_DIM, (h + 1) * HEAD_DIM)
        o_ref[hs, :] = acc_ref[hs, :] / l_ref[h:h + 1, :]


def moba_prompt(qt, k_rows, vt_b, kmean):
    b, width, t = qt.shape
    tq = MOBA_BLOCK
    nb = t // tq
    assert t % tq == 0
    n_heads = width // HEAD_DIM
    return _attn_prompt_call(functools.partial(_moba_prompt_kernel, tq=tq, width=width), "moba_prompt", qt, k_rows,
                             vt_b, [kmean], [pl.BlockSpec((None, nb, width), lambda bi, qi: (bi, 0, 0))],
                             [pltpu.VMEM((n_heads, nb, tq), F32), pltpu.VMEM((n_heads, tq), F32),
                              pltpu.VMEM((n_heads, tq), F32)], tq)


def _stack_heads(q, n_heads):
    t, width = q.shape
    rows = lax.broadcasted_iota(jnp.int32, (n_heads * t, width), 0) // t
    lanes = lax.broadcasted_iota(jnp.int32, (n_heads * t, width), 1) // HEAD_DIM
    return jnp.where(rows == lanes, jnp.concatenate([q] * n_heads, axis=0), 0.0)


def _unstack_heads(acc, n_heads):
    rows_total, width = acc.shape
    t = rows_total // n_heads
    rows = lax.broadcasted_iota(jnp.int32, (rows_total, width), 0) // t
    lanes = lax.broadcasted_iota(jnp.int32, (rows_total, width), 1) // HEAD_DIM
    kept = jnp.where(rows == lanes, acc, 0.0).reshape(n_heads, t, width)
    return jnp.sum(kept, axis=0)


def _later_matrix(tk):
    ki = lax.broadcasted_iota(jnp.int32, (tk, tk), 0)
    kj = lax.broadcasted_iota(jnp.int32, (tk, tk), 1)
    return jnp.where(ki > kj, 1.0, 0.0).astype(BF16)


def _sb_sample_kernel(pt_ref, q_ref, knew_ref, vnew_ref, *refs, n_pages_step, t_new, n_heads):
    k_refs = refs[:n_pages_step]
    v_refs = refs[n_pages_step:2 * n_pages_step]
    o_ref, qs_ref, acc_ref, run_ref = refs[2 * n_pages_step:]
    step = pl.program_id(1)
    tk = knew_ref.shape[0]
    later = _later_matrix(tk)

    def tile(k_tile, v_tile, mask, paged):
        z = _dg(qs_ref[...], k_tile.astype(BF16), _NN if paged else _NT)
        sp = _softplus2(z)
        l1m = -sp if mask is None else jnp.where(mask, -sp, 0.0)
        rest = _mm(l1m, later, 2) + run_ref[...]
        w = jnp.exp2(z - sp + rest)
        if mask is not None:
            w = jnp.where(mask, w, 0.0)
        acc_ref[...] += _mm(w, v_tile, 1, _NT if paged else _NN)
        run_ref[...] = rest[:, 0:1] + l1m[:, 0:1]

    @pl.when(step == 0)
    def _():
        qs_ref[...] = _stack_heads(q_ref[...], n_heads).astype(BF16)
        acc_ref[...] = jnp.zeros_like(acc_ref)
        run_ref[...] = jnp.zeros_like(run_ref)
        rows = lax.broadcasted_iota(jnp.int32, (n_heads * t_new, tk), 0) % t_new
        cols = lax.broadcasted_iota(jnp.int32, (n_heads * t_new, tk), 1)
        tile(knew_ref[...], vnew_ref[...], cols < rows, False)

    qs = qs_ref[...]
    z = [_dg(qs, k_refs[i][...].astype(BF16), _NN) for i in range(n_pages_step)]
    sp = [_softplus2(x) for x in z]
    cum = [_mm(-x, later, 2) for x in sp]
    run = run_ref[...]
    acc = acc_ref[...]
    for i in range(n_pages_step):
        w = jnp.exp2(z[i] - sp[i] + cum[i] + run)
        acc = acc + _mm(w, v_refs[i][...], 1, _NT)
        run = run + (cum[i][:, 0:1] - sp[i][:, 0:1])
    acc_ref[...] = acc
    run_ref[...] = run

    @pl.when(step == pl.num_programs(1) - 1)
    def _():
        o_ref[...] = _unstack_heads(acc_ref[...], n_heads)


def _page_specs(n_pages, n_pages_step, width, page_size, reverse):
    def spec(i):
        def index(bi, si, pt):
            p = si * n_pages_step + i
            return (pt[bi, n_pages - 1 - p if reverse else p], 0, 0)
        return pl.BlockSpec((None, width, page_size), index)
    return [spec(i) for i in range(n_pages_step)]


def _pad_rows(x, rows):
    return jnp.pad(x, ((0, 0), (0, rows - x.shape[1]), (0, 0)))


def sb_sample(q, k_new, v_new, k_pool, v_pool, pages, n_pages_step):
    b, t_new, width = q.shape
    n_heads = width // HEAD_DIM
    page_size = k_pool.shape[2]
    n_pages = pages.shape[1]
    assert n_pages % n_pages_step == 0 and t_new <= page_size
    new = pl.BlockSpec((None, t_new, width), lambda bi, si, pt: (bi, 0, 0))
    new_pad = pl.BlockSpec((None, page_size, width), lambda bi, si, pt: (bi, 0, 0))
    page_specs = _page_specs(n_pages, n_pages_step, width, page_size, reverse=True)
    return pl.pallas_call(
        functools.partial(_sb_sample_kernel, n_pages_step=n_pages_step, t_new=t_new, n_heads=n_heads),
        grid_spec=pltpu.PrefetchScalarGridSpec(
            num_scalar_prefetch=1, grid=(b, n_pages // n_pages_step),
            in_specs=[new, new_pad, new_pad] + page_specs + page_specs,
            out_specs=new,
            scratch_shapes=[pltpu.VMEM((n_heads * t_new, width), BF16),
                            pltpu.VMEM((n_heads * t_new, width), F32),
                            pltpu.VMEM((n_heads * t_new, 1), F32)]),
        out_shape=jax.ShapeDtypeStruct((b, t_new, width), F32),
        compiler_params=_cparams("parallel", "arbitrary"),
        name="sb_sample",
    )(pages, q, _pad_rows(k_new, page_size), _pad_rows(v_new, page_size),
      *([k_pool] * n_pages_step), *([v_pool] * n_pages_step))


def _block_mean_kernel(pt_ref, *refs, n_pages_step, pages_per_block):
    k_refs = refs[:n_pages_step]
    o_ref = refs[n_pages_step]
    page_size = k_refs[0].shape[1]
    scale = 1.0 / (pages_per_block * page_size)
    ones = jnp.ones((SUBLANES, page_size), BF16)
    means = []
    for n in range(n_pages_step // pages_per_block):
        tot = None
        for i in range(pages_per_block):
            hi, lo = _split_bf16(k_refs[n * pages_per_block + i][...])
            part = _dg(ones, hi, _NT) + _dg(ones, lo, _NT)
            tot = part if tot is None else tot + part
        means.append(tot[0:1, :] * scale)
    o_ref[...] = jnp.concatenate(means, axis=0)


def paged_block_means(k_pool, pages, block):
    b, n_pages = pages.shape
    width, page_size = k_pool.shape[1:]
    pages_per_block = block // page_size
    n_pages_step = SUBLANES * pages_per_block
    assert block % page_size == 0 and n_pages % n_pages_step == 0
    return pl.pallas_call(
        functools.partial(_block_mean_kernel, n_pages_step=n_pages_step, pages_per_block=pages_per_block),
        grid_spec=pltpu.PrefetchScalarGridSpec(
            num_scalar_prefetch=1, grid=(b, n_pages // n_pages_step),
            in_specs=_page_specs(n_pages, n_pages_step, width, page_size, reverse=False),
            out_specs=pl.BlockSpec((None, SUBLANES, width), lambda bi, si, pt: (bi, si, 0))),
        out_shape=jax.ShapeDtypeStruct((b, n_pages // pages_per_block, width), F32),
        compiler_params=_cparams("parallel", "arbitrary"),
        name="paged_block_means",
    )(pages, *([k_pool] * n_pages_step))


def _moba_sample_kernel(pt_ref, q_ref, kmean_ref, knew_ref, vnew_ref, *refs, n_pages_step, pages_per_block,
                        t_new, n_heads):
    k_refs = refs[:n_pages_step]
    v_refs = refs[n_pages_step:2 * n_pages_step]
    o_ref, qs_ref, sel_ref, acc_ref, m_ref, l_ref = refs[2 * n_pages_step:]
    step = pl.program_id(1)
    nb = kmean_ref.shape[0]
    rows_total = n_heads * t_new

    def tile(k_tile, v_tile, mask, paged):
        s = jnp.where(mask, _dg(qs_ref[...], k_tile.astype(BF16), _NN if paged else _NT), NEG_BIG)
        m_new = jnp.maximum(m_ref[...], jnp.max(s, axis=1, keepdims=True))
        p = jnp.where(mask, jnp.exp(s - m_new), 0.0)
        alpha = jnp.exp(m_ref[...] - m_new)
        l_ref[...] = alpha * l_ref[...] + jnp.sum(p, axis=1, keepdims=True)
        acc_ref[...] = alpha * acc_ref[...] + _mm(p, v_tile, 1, _NT if paged else _NN)
        m_ref[...] = m_new

    @pl.when(step == 0)
    def _():
        qs = _stack_heads(q_ref[...], n_heads)
        qs_ref[...] = qs.astype(BF16)
        gate = _mm(qs, kmean_ref[...], 3, _NT)
        sel_ref[...] = jnp.where(_topk_block_mask(gate, nb, MOBA_TOPK, 1), 1.0, 0.0)
        acc_ref[...] = jnp.zeros_like(acc_ref)
        l_ref[...] = jnp.zeros_like(l_ref)
        m_ref[...] = jnp.full(m_ref.shape, NEG_BIG, F32)
        tk = knew_ref.shape[0]
        rows = lax.broadcasted_iota(jnp.int32, (rows_total, tk), 0) % t_new
        cols = lax.broadcasted_iota(jnp.int32, (rows_total, tk), 1)
        tile(knew_ref[...], vnew_ref[...], cols <= rows, False)

    qs = qs_ref[...]
    page_size = k_refs[0].shape[1]
    n_keys = n_pages_step * page_size
    key_blk = (step * n_keys + lax.broadcasted_iota(jnp.int32, (nb, n_keys), 1)) // MOBA_BLOCK
    expand = jnp.where(key_blk == lax.broadcasted_iota(jnp.int32, (nb, n_keys), 0), 1.0, 0.0).astype(BF16)
    picked = _dg(sel_ref[...].astype(BF16), expand, _NN) > 0.5
    mask = [picked[:, i * page_size:(i + 1) * page_size] for i in range(n_pages_step)]
    s = [jnp.where(mask[i], _dg(qs, k_refs[i][...].astype(BF16), _NN), NEG_BIG) for i in range(n_pages_step)]
    s_max = s[0]
    for x in s[1:]:
        s_max = jnp.maximum(s_max, x)
    m_old = m_ref[...]
    m_new = jnp.maximum(m_old, jnp.max(s_max, axis=1, keepdims=True))
    p = [jnp.where(mask[i], jnp.exp(s[i] - m_new), 0.0) for i in range(n_pages_step)]
    alpha = jnp.exp(m_old - m_new)
    p_sum = p[0]
    acc = alpha * acc_ref[...] + _mm(p[0], v_refs[0][...], 1, _NT)
    for i in range(1, n_pages_step):
        p_sum = p_sum + p[i]
        acc = acc + _mm(p[i], v_refs[i][...], 1, _NT)
    l_ref[...] = alpha * l_ref[...] + jnp.sum(p_sum, axis=1, keepdims=True)
    acc_ref[...] = acc
    m_ref[...] = m_new

    @pl.when(step == pl.num_programs(1) - 1)
    def _():
        o_ref[...] = _unstack_heads(acc_ref[...] / l_ref[...], n_heads)


def moba_sample(q, k_new, v_new, k_pool, v_pool, pages, kmean, n_pages_step):
    b, t_new, width = q.shape
    n_heads = width // HEAD_DIM
    page_size = k_pool.shape[2]
    n_pages = pages.shape[1]
    nb = kmean.shape[1]
    pages_per_block = MOBA_BLOCK // page_size
    assert n_pages % n_pages_step == 0 and t_new <= page_size and n_pages == nb * pages_per_block
    assert (n_pages * page_size) % MOBA_BLOCK == 0 and t_new <= MOBA_BLOCK
    new = pl.BlockSpec((None, t_new, width), lambda bi, si, pt: (bi, 0, 0))
    new_pad = pl.BlockSpec((None, page_size, width), lambda bi, si, pt: (bi, 0, 0))
    page_specs = _page_specs(n_pages, n_pages_step, width, page_size, reverse=False)
    rows_total = n_heads * t_new
    return pl.pallas_call(
        functools.partial(_moba_sample_kernel, n_pages_step=n_pages_step, pages_per_block=pages_per_block,
                          t_new=t_new, n_heads=n_heads),
        grid_spec=pltpu.PrefetchScalarGridSpec(
            num_scalar_prefetch=1, grid=(b, n_pages // n_pages_step),
            in_specs=[new, pl.BlockSpec((None, nb, width), lambda bi, si, pt: (bi, 0, 0)), new_pad, new_pad]
                     + page_specs + page_specs,
            out_specs=new,
            scratch_shapes=[pltpu.VMEM((rows_total, width), BF16), pltpu.VMEM((rows_total, nb), F32),
                            pltpu.VMEM((rows_total, width), F32), pltpu.VMEM((rows_total, 1), F32),
                            pltpu.VMEM((rows_total, 1), F32)]),
        out_shape=jax.ShapeDtypeStruct((b, t_new, width), F32),
        compiler_params=_cparams("parallel", "arbitrary"),
        name="moba_sample",
    )(pages, q, kmean, _pad_rows(k_new, page_size), _pad_rows(v_new, page_size),
      *([k_pool] * n_pages_step), *([v_pool] * n_pages_step))


def _merge_kernel(x_ref, g_ref, wg_ref, y0_ref, y1_ref, y2_ref, y3_ref, wb_ref, wo_ref, o_ref, *, d_model, layouts):
    x = x_ref[...]
    h = _rms(x, g_ref[...]).astype(BF16)
    merged = None
    for n, (y_ref, layout) in enumerate(zip((y0_ref, y1_ref, y2_ref, y3_ref), layouts)):
        gate = jax.nn.sigmoid(_dg(h, wg_ref[:, n * d_model:(n + 1) * d_model], _NN))
        term = gate * _mm(y_ref[...], wb_ref[n], 1, _TN if layout == "transposed" else _NN)
        merged = term if merged is None else merged + term
    o_ref[...] = x + _mm(merged, wo_ref[...])


def merge(x, g, w_gate, ys, layouts, w_branch, w_out, tm):
    b, t, d = x.shape
    width = w_branch.shape[1]
    tm = min(tm, t)
    assert t % tm == 0
    fixed = lambda bi, ti: (0, 0)
    seq = lambda bi, ti: (bi, ti, 0)
    y_spec = {"rows": pl.BlockSpec((None, tm, width), seq),
              "time_major": pl.BlockSpec((tm, width), lambda bi, ti: (ti, bi)),
              "transposed": pl.BlockSpec((None, width, tm), lambda bi, ti: (bi, 0, ti))}
    return pl.pallas_call(
        functools.partial(_merge_kernel, d_model=d, layouts=tuple(layouts)),
        grid=(b, t // tm),
        in_specs=[pl.BlockSpec((None, tm, d), seq), pl.BlockSpec((1, d), fixed),
                  pl.BlockSpec((d, N_BRANCH * d), fixed)]
                 + [y_spec[k] for k in layouts]
                 + [pl.BlockSpec((N_BRANCH, width, d), lambda bi, ti: (0, 0, 0)), pl.BlockSpec((d, d), fixed)],
        out_specs=pl.BlockSpec((None, tm, d), seq),
        out_shape=jax.ShapeDtypeStruct((b, t, d), F32),
        compiler_params=_cparams("parallel", "parallel"),
        name="merge",
    )(x, g.reshape(1, d), w_gate, *ys, w_branch, w_out)


def _ffn_kernel(x_ref, g_ref, wup_ref, cw_ref, cb_ref, wdn_ref, conv0_ref, o_ref, conv_ref, st_ref,
                *, nseq, tt, d_ff, col_chunk):
    ti = pl.program_id(1)

    @pl.when(ti == 0)
    def _():
        st_ref[...] = conv0_ref[...]

    d_model = x_ref.shape[-1]
    x = x_ref[...].reshape(nseq * tt, d_model)
    h = _rms(x, g_ref[...]).astype(BF16)
    t_idx = lax.broadcasted_iota(jnp.int32, (1, tt, 1), 1)

    def conv_cols(c0):
        cols = slice(c0, c0 + col_chunk)
        up = _dg(h, wup_ref[:, cols], _NN).reshape(nseq, tt, col_chunk)
        st = st_ref[:, :, cols]
        p1 = jnp.where(t_idx == 0, st[:, 1:2, :], pltpu.roll(up, 1, axis=1))
        p2 = jnp.where(t_idx == 0, st[:, 0:1, :], jnp.where(t_idx == 1, st[:, 1:2, :], pltpu.roll(up, 2, axis=1)))
        st_ref[:, :, cols] = up[:, tt - 2:, :]
        cw = cw_ref[:, cols]
        c = cb_ref[:, cols] + cw[2:3, :] * up + cw[1:2, :] * p1 + cw[0:1, :] * p2
        return c.reshape(nseq * tt, col_chunk)

    acc = x
    for c0 in range(0, d_ff, col_chunk):
        a = conv_cols(c0)
        b = conv_cols(d_ff + c0)
        acc = acc + _mm(a * jax.nn.sigmoid(a) * b, wdn_ref[c0:c0 + col_chunk, :])
    o_ref[...] = acc.reshape(nseq, tt, d_model)
    conv_ref[...] = st_ref[...]


def conv_ffn(x, g, w_up, conv_w, conv_b, w_down, conv0, nseq, tt, col_chunk):
    b, t, d = x.shape
    d_ff = w_down.shape[0]
    assert b % nseq == 0 and t % tt == 0 and d_ff % col_chunk == 0 and tt >= CONV_W - 1
    fixed = lambda bi, ti: (0, 0)
    seq = lambda bi, ti: (bi, ti, 0)
    st = lambda bi, ti: (bi, 0, 0)
    return pl.pallas_call(
        functools.partial(_ffn_kernel, nseq=nseq, tt=tt, d_ff=d_ff, col_chunk=col_chunk),
        grid=(b // nseq, t // tt),
        in_specs=[pl.BlockSpec((nseq, tt, d), seq), pl.BlockSpec((1, d), fixed),
                  pl.BlockSpec((d, 2 * d_ff), fixed), pl.BlockSpec((CONV_W, 2 * d_ff), fixed),
                  pl.BlockSpec((1, 2 * d_ff), fixed), pl.BlockSpec((d_ff, d), fixed),
                  pl.BlockSpec((nseq, CONV_W - 1, 2 * d_ff), st)],
        out_specs=[pl.BlockSpec((nseq, tt, d), seq), pl.BlockSpec((nseq, CONV_W - 1, 2 * d_ff), st)],
        out_shape=[jax.ShapeDtypeStruct((b, t, d), F32), jax.ShapeDtypeStruct((b, CONV_W - 1, 2 * d_ff), F32)],
        scratch_shapes=[pltpu.VMEM((nseq, CONV_W - 1, 2 * d_ff), F32)],
        compiler_params=_cparams("parallel", "arbitrary"),
        name="conv_ffn",
    )(x, g.reshape(1, d), w_up, conv_w, conv_b.reshape(1, 2 * d_ff), w_down, conv0)


MOBA_Q_SCALE = HEAD_DIM ** -0.5
SB_Q_SCALE = HEAD_DIM ** -0.5 * LOG2_E

PROMPT_TILES = dict(proj_tm=512, s5_tt=128, s5_passes=1, rwkv_tt=256, rwkv_chunk=64, rwkv_passes=1,
                    sb_tq=256, merge_tm=256, ffn_tt=256, ffn_cols=1408)
SAMPLE_TILES = dict(proj_tm=256, s5_tt=8, s5_passes=3, rwkv_tt=128, rwkv_chunk=64, rwkv_passes=1,
                    merge_tm=256, ffn_cols=1408, sb_pages=16, moba_pages=16)


def _layer(x, st, past, p, tiles):
    b, t, d = x.shape
    width = d // 2
    n_heads = width // HEAD_DIM
    n_rwkv = 3 * width + RWKV_DECAY_LORA + RWKV_ICLR_LORA + RWKV_GATE_LORA
    off_sb = width + n_rwkv
    off_mb = off_sb + 3 * width
    off_gate = off_mb + 3 * width
    w_in = p['w_in']
    prompt = past is None
    n_state = p['s5_ab_re'].shape[0]
    s5_args = (st['s5_re'].reshape(b, n_state), st['s5_im'].reshape(b, n_state), p['s5_ab_re'], p['s5_ab_im'],
               p['s5_bb'][tiles['s5_passes']], p['s5_cc'][tiles['s5_passes']], p['s5_d'], p['s5_w_glu'],
               tiles['s5_tt'], tiles['s5_passes'])

    if prompt:
        u_s5, u_rw = proj_plain(x, p['norm1_g'], w_in[:, :off_sb], (width, n_rwkv), tiles['proj_tm'], True)
        y_s5, s5_re, s5_im = s5_mixer(u_s5.reshape(t, b, width), *s5_args)
        y_s5 = y_s5.reshape(t, b * width)
        qt, k_sb, v_sb, k_rows, vt_b, _ = proj_qkv_t(x, p['norm1_g'], p['w_sb_t'], p['sb_q_g'], p['sb_k_g'],
                                                     tiles['proj_tm'], SB_Q_SCALE, p['layer'], p['depth'],
                                                     st['kv_sb'])
        y_sb = sb_prompt(qt, k_rows, vt_b, tiles['sb_tq'])
        qt, k_mb, v_mb, k_rows, vt_b, kmean = proj_qkv_t(x, p['norm1_g'], p['w_mb_t'], p['moba_q_g'],
                                                         p['moba_k_g'], tiles['proj_tm'], MOBA_Q_SCALE,
                                                         p['layer'], p['depth'], st['kv_mb'])
        y_mb = moba_prompt(qt, k_rows, vt_b, kmean)
        layouts = ("time_major", "rows", "transposed", "transposed")
        x_m = x
        kv = lambda a: a
    else:
        x_m = x.reshape(1, b * t, d)
        u_s5, u_rw = proj_plain(x_m, p['norm1_g'], w_in[:, :off_sb], (width, n_rwkv), b * t, False)
        y_s5, s5_re, s5_im = s5_mixer(u_s5.reshape(b, t, width).transpose(1, 0, 2), *s5_args)
        y_s5 = y_s5.transpose(1, 0, 2).reshape(1, b * t, width)
        x2 = x.reshape(b * t, d)
        seq = lambda a: a.reshape(b, t, width)
        q_sb, k_sb, v_sb = proj_qkv(x2, p['norm1_g'], w_in[:, off_sb:off_mb], p['sb_q_g'], p['sb_k_g'], b * t,
                                    SB_Q_SCALE)
        q_mb, k_mb, v_mb = proj_qkv(x2, p['norm1_g'], w_in[:, off_mb:off_gate], p['moba_q_g'], p['moba_k_g'], b * t,
                                    MOBA_Q_SCALE)
        y_sb = sb_sample(seq(q_sb), seq(k_sb), seq(v_sb), past['sb_k'], past['sb_v'], past['pages'],
                         tiles['sb_pages'])
        kmean = paged_block_means(past['moba_k'], past['pages'], MOBA_BLOCK)
        y_mb = moba_sample(seq(q_mb), seq(k_mb), seq(v_mb), past['moba_k'], past['moba_v'], past['pages'], kmean,
                           tiles['moba_pages'])
        y_sb, y_mb = y_sb.reshape(1, b * t, width), y_mb.reshape(1, b * t, width)
        layouts = ("rows",) * N_BRANCH
        kv = lambda a: a.reshape(b, t, n_heads, HEAD_DIM)

    y_rw, wkv, shift = rwkv_mixer(u_rw.reshape(b, t, n_rwkv), st['shift'], st['wkv'], p, tiles['rwkv_tt'],
                                  tiles['rwkv_chunk'], tiles['rwkv_passes'])
    ys = [y_s5, y_rw.reshape(x_m.shape[0], x_m.shape[1], width), y_sb, y_mb]
    x1 = merge(x_m, p['norm1_g'], w_in[:, off_gate:], ys, layouts, p['w_branch'], p['w_out'], tiles['merge_tm'])
    nseq, tt = (1, tiles['ffn_tt']) if prompt else (b, t)
    x_out, conv = conv_ffn(x1.reshape(b, t, d), p['norm2_g'], p['ffn_w_up'], p['ffn_conv_w'], p['ffn_conv_b'],
                           p['ffn_w_down'], st['conv'], nseq, tt, tiles['ffn_cols'])
    new_st = (s5_re.reshape(st['s5_re'].shape), s5_im.reshape(st['s5_im'].shape), wkv, shift, conv,
              kv(k_sb), kv(v_sb), kv(k_mb), kv(v_mb))
    return x_out, new_st


def kernel(x_prompt, x_sample, state_s5_re, state_s5_im, state_rwkv_wkv, state_rwkv_shift, state_ffn_conv, cache_sb_k, cache_sb_v, cache_moba_k, cache_moba_v, page_table, norm1_g, w_in, s5_a_re, s5_a_im, s5_log_dt, s5_b_re, s5_b_im, s5_c_re, s5_c_im, s5_d, s5_w_glu, rwkv_mu, rwkv_w0, rwkv_w_up, rwkv_a0, rwkv_a_up, rwkv_g_up, rwkv_k_k, rwkv_k_a, rwkv_r_k, rwkv_ln_w, rwkv_ln_b, sb_q_g, sb_k_g, moba_q_g, moba_k_g, w_branch, w_out, norm2_g, ffn_w_up, ffn_conv_w, ffn_conv_b, ffn_w_down):
    depth = w_in.shape[0]
    bp = x_prompt.shape[0]
    n_pool, page_size, n_heads, head_dim = cache_sb_k.shape[1:]
    width = n_heads * head_dim
    pool = lambda c: c.transpose(0, 1, 3, 4, 2).reshape(depth * n_pool, width, page_size)
    pools = dict(sb_k=pool(cache_sb_k), sb_v=pool(cache_sb_v), moba_k=pool(cache_moba_k), moba_v=pool(cache_moba_v))
    g_s5, p_s5 = s5_a_re.shape[1:]
    n_rwkv = state_rwkv_shift.shape[-1]
    d_ff2 = state_ffn_conv.shape[-1]
    zero_st = dict(s5_re=jnp.zeros((bp, g_s5, p_s5), F32), s5_im=jnp.zeros((bp, g_s5, p_s5), F32),
                   shift=jnp.zeros((bp, n_rwkv), F32), wkv=jnp.zeros((bp, n_heads, head_dim, head_dim), F32),
                   conv=jnp.zeros((bp, CONV_W - 1, d_ff2), F32))
    yp, ys = x_prompt, x_sample
    states_p, states_s = [], []
    kv_sb = kv_mb = None
    for l in range(depth):
        ab_re, ab_im, bb, cc = s5_operands(s5_a_re[l], s5_a_im[l], s5_log_dt[l], s5_b_re[l], s5_b_im[l],
                                           s5_c_re[l], s5_c_im[l])
        w_l = w_in[l].astype(BF16)
        width_l = w_l.shape[0] // 2
        off_sb = width_l + state_rwkv_shift.shape[-1]
        p = dict(layer=l, depth=depth, norm1_g=norm1_g[l], w_in=w_l, w_sb_t=w_l[:, off_sb:off_sb + 3 * width_l].T,
                 w_mb_t=w_l[:, off_sb + 3 * width_l:off_sb + 6 * width_l].T, s5_ab_re=ab_re, s5_ab_im=ab_im,
                 s5_bb={1: bb.astype(BF16), 3: bb}, s5_cc={1: cc.astype(BF16), 3: cc}, s5_d=s5_d[l],
                 s5_w_glu=s5_w_glu[l].astype(BF16), rwkv_mu=rwkv_mu[l], rwkv_w0=rwkv_w0[l],
                 rwkv_w_up=rwkv_w_up[l], rwkv_a0=rwkv_a0[l], rwkv_a_up=rwkv_a_up[l], rwkv_g_up=rwkv_g_up[l],
                 rwkv_k_k=rwkv_k_k[l], rwkv_k_a=rwkv_k_a[l], rwkv_r_k=rwkv_r_k[l].reshape(-1),
                 rwkv_ln_w=rwkv_ln_w[l], rwkv_ln_b=rwkv_ln_b[l], sb_q_g=sb_q_g[l], sb_k_g=sb_k_g[l],
                 moba_q_g=moba_q_g[l], moba_k_g=moba_k_g[l], w_branch=w_branch[l].astype(BF16),
                 w_out=w_out[l].astype(BF16), norm2_g=norm2_g[l], ffn_w_up=ffn_w_up[l].astype(BF16),
                 ffn_conv_w=ffn_conv_w[l], ffn_conv_b=ffn_conv_b[l], ffn_w_down=ffn_w_down[l].astype(BF16))
        yp, st_p = _layer(yp, dict(zero_st, kv_sb=kv_sb, kv_mb=kv_mb), None, p, PROMPT_TILES)
        kv_sb, kv_mb = st_p[5:7], st_p[7:9]
        st_s = dict(s5_re=state_s5_re[l], s5_im=state_s5_im[l], shift=state_rwkv_shift[l], wkv=state_rwkv_wkv[l],
                    conv=state_ffn_conv[l])
        past = dict(pools, pages=page_table + l * n_pool)
        ys, st_s = _layer(ys, st_s, past, p, SAMPLE_TILES)
        states_p.append(st_p[:5])
        states_s.append(st_s)
    t_p = x_prompt.shape[1]
    kv_out = lambda a: a.reshape(depth, bp, n_heads, head_dim, t_p).transpose(0, 1, 4, 2, 3)
    stacked_p = [jnp.stack(z, axis=0) for z in zip(*states_p)] + [kv_out(a) for a in (*kv_sb, *kv_mb)]
    stacked_s = [jnp.stack(z, axis=0) for z in zip(*states_s)]
    out = [yp, ys]
    for a, c in zip(stacked_p, stacked_s):
        out += [a, c]
    return tuple(out)
```

```python
import functools
import math

import jax
import jax.numpy as jnp
from jax import lax
from jax.experimental import pallas as pl
from jax.experimental.pallas import tpu as pltpu

F32 = jnp.float32
BF16 = jnp.bfloat16

HEAD_DIM = 64
S5_GROUP = 16
S5_STATE = 64
RWKV_DECAY_LORA = 64
RWKV_ICLR_LORA = 64
RWKV_GATE_LORA = 128
N_BRANCH = 4
MOBA_BLOCK = 256
MOBA_TOPK = 3
CONV_W = 3
EPS = 1e-6
RWKV_GN_EPS = 64e-5
RWKV_MAX_CHUNK = 64

V7X_VMEM_LIMIT_BYTES = 56 * 1024 * 1024
SUBLANES = 8
LANES = 128


def _resident(shape):
    zeros = (0,) * len(shape)
    return pl.BlockSpec(shape, lambda *_: zeros, pipeline_mode=pl.Buffered(1))


def _cparams(*sem):
    return pltpu.CompilerParams(dimension_semantics=sem, vmem_limit_bytes=V7X_VMEM_LIMIT_BYTES)


def _split_bf16(x):
    hi = x.astype(BF16)
    lo = (x - hi.astype(F32)).astype(BF16)
    return hi, lo


_NN = (((1,), (0,)), ((), ()))
_NT = (((1,), (1,)), ((), ()))


def _dg(a, b, dims):
    return lax.dot_general(a, b, dims, preferred_element_type=F32)


def _mm(a, b, passes=1, dims=_NN):
    if passes == 1:
        return _dg(a.astype(BF16), b.astype(BF16), dims)
    ah, al = _split_bf16(a)
    if b.dtype == BF16:
        return _dg(ah, b, dims) + _dg(al, b, dims)
    bh, bl = _split_bf16(b)
    if passes == 2:
        return _dg(ah, bh, dims) + _dg(al, bh, dims)
    return _dg(ah, bh, dims) + (_dg(al, bh, dims) + _dg(ah, bl, dims))


def _head_block_matrix(width, value):
    r = jnp.arange(width) // HEAD_DIM
    return jnp.where(r[:, None] == r[None, :], value, 0.0).astype(BF16)


def _rms(x, g):
    return x * lax.rsqrt(jnp.mean(x * x, axis=-1, keepdims=True) + EPS) * g


def _proj_plain_kernel(x_ref, g_ref, w_ref, *out_refs, widths):
    h = _rms(x_ref[...], g_ref[...]).astype(BF16)
    u = _dg(h, w_ref[...], _NN)
    off = 0
    for o_ref, n in zip(out_refs, widths):
        o_ref[...] = u[:, off:off + n]
        off += n


def proj_plain(x, g, w, widths, tm, first_time_major):
    b, t, d = x.shape
    n = w.shape[1]
    assert sum(widths) == n and t % tm == 0
    fixed = lambda bi, ti: (0, 0)
    seq = lambda bi, ti: (bi, ti, 0)
    out_specs = [pl.BlockSpec((None, tm, k), seq) for k in widths]
    out_shape = [jax.ShapeDtypeStruct((b, t, k), F32) for k in widths]
    if first_time_major:
        out_specs[0] = pl.BlockSpec((tm, widths[0]), lambda bi, ti: (ti, bi))
        out_shape[0] = jax.ShapeDtypeStruct((t, b * widths[0]), F32)
    return pl.pallas_call(
        functools.partial(_proj_plain_kernel, widths=widths),
        grid=(b, t // tm),
        in_specs=[pl.BlockSpec((None, tm, d), seq), pl.BlockSpec((1, d), fixed), _resident((d, n))],
        out_specs=out_specs,
        out_shape=out_shape,
        compiler_params=_cparams("parallel", "parallel"),
        name="proj_plain",
    )(x, g.reshape(1, d), w)


def _head_rms(x, hm, g):
    ms = _mm(x * x, hm, passes=2)
    return x * lax.rsqrt(ms + EPS) * g


def _proj_qkv_kernel(x_ref, g_ref, w_ref, hm_ref, gq_ref, gk_ref, q_ref, k_ref, v_ref, *, width, q_scale):
    h = _rms(x_ref[...], g_ref[...]).astype(BF16)
    u = _dg(h, w_ref[...], _NN)
    hm = hm_ref[...]
    q_ref[...] = _head_rms(u[:, :width], hm, gq_ref[...]) * q_scale
    k_ref[...] = _head_rms(u[:, width:2 * width], hm, gk_ref[...])
    v_ref[...] = u[:, 2 * width:]


def proj_qkv(x2d, g, w, g_q, g_k, tm, q_scale):
    m, d = x2d.shape
    width = w.shape[1] // 3
    n_heads = width // HEAD_DIM
    hm = _head_block_matrix(width, 1.0 / HEAD_DIM)
    row = lambda i: (i, 0)
    fixed = lambda i: (0, 0)
    return pl.pallas_call(
        functools.partial(_proj_qkv_kernel, width=width, q_scale=q_scale),
        grid=(m // tm,),
        in_specs=[pl.BlockSpec((tm, d), row), pl.BlockSpec((1, d), fixed),
                  pl.BlockSpec((d, 3 * width), fixed), pl.BlockSpec((width, width), fixed),
                  pl.BlockSpec((1, width), fixed), pl.BlockSpec((1, width), fixed)],
        out_specs=[pl.BlockSpec((tm, width), row)] * 3,
        out_shape=[jax.ShapeDtypeStruct((m, width), F32)] * 3,
        compiler_params=_cparams("parallel"),
        name="proj_qkv",
    )(x2d, g.reshape(1, d), w, hm, jnp.tile(g_q, n_heads).reshape(1, width),
      jnp.tile(g_k, n_heads).reshape(1, width))


def _s5_disc_kernel(are_ref, aim_ref, ldt_ref, bre_ref, bim_ref, abre_ref, abim_ref, bbre_ref, bbim_ref):
    lr = are_ref[...]
    li = aim_ref[...]
    dt = jnp.exp(ldt_ref[...])
    mag = jnp.exp(lr * dt)
    ab_re = mag * jnp.cos(li * dt)
    ab_im = mag * jnp.sin(li * dt)
    den = lr * lr + li * li
    zr = ((ab_re - 1.0) * lr + ab_im * li) / den
    zi = (ab_im * lr - (ab_re - 1.0) * li) / den
    abre_ref[...] = ab_re
    abim_ref[...] = ab_im
    br = bre_ref[...]
    bi = bim_ref[...]
    bbre_ref[...] = zr[:, None, :] * br - zi[:, None, :] * bi
    bbim_ref[...] = zr[:, None, :] * bi + zi[:, None, :] * br


def s5_discretize(a_re, a_im, log_dt, b_re, b_im):
    g, p = a_re.shape
    gc = b_re.shape[1]
    return pl.pallas_call(
        _s5_disc_kernel,
        out_shape=[jax.ShapeDtypeStruct((g, p), F32)] * 2 + [jax.ShapeDtypeStruct((g, gc, p), F32)] * 2,
        name="s5_discretize",
    )(a_re, a_im, log_dt.reshape(g, 1), b_re, b_im)


def _gelu_tanh(x):
    c = math.sqrt(2.0 / math.pi)
    return 0.5 * x * (1.0 + jnp.tanh(c * (x + 0.044715 * (x * x * x))))


def _s5_kernel(u_ref, s0re_ref, s0im_ref, abre_ref, abim_ref, bb_ref, cc_ref, d_ref, wglu_ref,
               y_ref, sre_ref, sim_ref, xs_ref, cre_ref, cim_ref, *, tt, n_state, passes, lane_chunk):
    ti = pl.program_id(1)

    @pl.when(ti == 0)
    def _():
        cre_ref[...] = s0re_ref[...]
        cim_ref[...] = s0im_ref[...]

    width = u_ref.shape[-1]
    n_cl, c_in, c_state = bb_ref.shape
    half = c_state // 2
    u = u_ref[...].reshape(tt * SUBLANES, width)
    for c in range(n_cl):
        xc = _mm(u[:, c * c_in:(c + 1) * c_in], bb_ref[c], passes)
        xs_ref[:, c * half:(c + 1) * half] = xc[:, :half]
        xs_ref[:, n_state + c * half:n_state + (c + 1) * half] = xc[:, half:]

    for c0 in range(0, n_state, lane_chunk):
        re_sl = pl.ds(c0, lane_chunk)
        im_sl = pl.ds(n_state + c0, lane_chunk)
        a_re = jnp.broadcast_to(abre_ref[:, re_sl], (SUBLANES, lane_chunk))
        a_im = jnp.broadcast_to(abim_ref[:, re_sl], (SUBLANES, lane_chunk))

        def step(t, carry):
            s_re, s_im = carry
            rows = pl.ds(pl.multiple_of(t * SUBLANES, SUBLANES), SUBLANES)
            n_re = a_re * s_re - a_im * s_im + xs_ref[rows, re_sl]
            n_im = a_re * s_im + a_im * s_re + xs_ref[rows, im_sl]
            xs_ref[rows, re_sl] = n_re
            xs_ref[rows, im_sl] = n_im
            return n_re, n_im

        s_re, s_im = lax.fori_loop(0, tt, step, (cre_ref[:, re_sl], cim_ref[:, re_sl]), unroll=8)
        cre_ref[:, re_sl] = s_re
        cim_ref[:, re_sl] = s_im

    y = jnp.concatenate(
        [_mm(xs_ref[:, c * half:(c + 1) * half], cc_ref[c, :half, :], passes)
         + _mm(xs_ref[:, n_state + c * half:n_state + (c + 1) * half], cc_ref[c, half:, :], passes)
         for c in range(n_cl)], axis=1) + d_ref[...] * u
    y = _gelu_tanh(y)
    y = y * jax.nn.sigmoid(_mm(y, wglu_ref[...]))
    y_ref[...] = y.reshape(tt, SUBLANES, width)
    sre_ref[...] = cre_ref[...]
    sim_ref[...] = cim_ref[...]


def s5_mixer(u_tm, s0_re, s0_im, ab_re, ab_im, bb, cc, d_skip, w_glu, tt, passes):
    t, b, width = u_tm.shape
    n_state = ab_re.shape[-1]
    assert t % tt == 0 and b % SUBLANES == 0
    fixed = lambda bi, ti: (0, 0)
    st = lambda bi, ti: (bi, 0)
    return pl.pallas_call(
        functools.partial(_s5_kernel, tt=tt, n_state=n_state, passes=passes, lane_chunk=512),
        grid=(b // SUBLANES, t // tt),
        in_specs=[pl.BlockSpec((tt, SUBLANES, width), lambda bi, ti: (ti, bi, 0)),
                  pl.BlockSpec((SUBLANES, n_state), st), pl.BlockSpec((SUBLANES, n_state), st),
                  pl.BlockSpec((1, n_state), fixed), pl.BlockSpec((1, n_state), fixed),
                  pl.BlockSpec(bb.shape, lambda bi, ti: (0, 0, 0)), pl.BlockSpec(cc.shape, lambda bi, ti: (0, 0, 0)),
                  pl.BlockSpec((1, width), fixed), pl.BlockSpec((width, width), fixed)],
        out_specs=[pl.BlockSpec((tt, SUBLANES, width), lambda bi, ti: (ti, bi, 0)),
                   pl.BlockSpec((SUBLANES, n_state), st), pl.BlockSpec((SUBLANES, n_state), st)],
        out_shape=[jax.ShapeDtypeStruct((t, b, width), F32),
                   jax.ShapeDtypeStruct((b, n_state), F32), jax.ShapeDtypeStruct((b, n_state), F32)],
        scratch_shapes=[pltpu.VMEM((tt * SUBLANES, 2 * n_state), F32),
                        pltpu.VMEM((SUBLANES, n_state), F32), pltpu.VMEM((SUBLANES, n_state), F32)],
        compiler_params=_cparams("parallel", "arbitrary"),
        name="s5_mixer",
    )(u_tm, s0_re, s0_im, ab_re.reshape(1, n_state), ab_im.reshape(1, n_state), bb, cc,
      d_skip.reshape(1, width), w_glu)


def s5_operands(a_re, a_im, log_dt, b_re, b_im, c_re, c_im):
    g, p = a_re.shape
    gc = b_re.shape[-1]
    ab_re, ab_im, bb_re, bb_im = s5_discretize(a_re, a_im, log_dt, b_re.transpose(0, 2, 1),
                                               b_im.transpose(0, 2, 1))
    cl = LANES // gc
    assert g % cl == 0
    eye = jnp.eye(cl, dtype=F32)
    split = lambda m: m.reshape(g // cl, cl, gc, p)
    bd = lambda m: jnp.einsum('kacp,ab->kacbp', split(m), eye).reshape(g // cl, cl * gc, cl * p)
    bb = jnp.concatenate([bd(bb_re), bd(bb_im)], axis=2)
    bdc = lambda m: jnp.einsum('kacp,ab->kbpac', split(m), eye).reshape(g // cl, cl * p, cl * gc)
    cc = jnp.concatenate([bdc(c_re), -bdc(c_im)], axis=1)
    return ab_re.reshape(g * p), ab_im.reshape(g * p), bb, cc


def _softplus(x):
    return jnp.maximum(x, 0.0) + jnp.log(1.0 + jnp.exp(-jnp.abs(x)))


def _softplus2(x):
    return jnp.maximum(x, 0.0) + jnp.log2(1.0 + jnp.exp2(-jnp.abs(x)))


LOG2_E = math.log2(math.e)


def _rwkv_kernel(u_ref, shift0_ref, wkv0_ref, mu_ref, w0_ref, wup_ref, a0_ref, aup_ref, gup_ref, kk_ref,
                 ka_ref, rk_ref, lnw_ref, lnb_ref, hsum_ref, y_ref, wkv_ref, shift_ref,
                 prev_ref, s_ref, yt_ref, *, tt, chunk, width, passes, t_valid):
    ti = pl.program_id(1)
    n_heads = width // HEAD_DIM

    @pl.when(ti == 0)
    def _():
        prev_ref[...] = shift0_ref[...]
        s_ref[...] = wkv0_ref[...]

    u = u_ref[...]
    row = lax.broadcasted_iota(jnp.int32, (tt, 1), 0)
    prev = jnp.where(row == 0, prev_ref[...], pltpu.roll(u, 1, axis=0))
    xs = u + mu_ref[...] * (prev - u)
    prev_ref[...] = u[t_valid - 1:t_valid, :]
    shift_ref[...] = u[t_valid - 1:t_valid, :]

    w3 = 3 * width
    o1 = w3 + RWKV_DECAY_LORA
    o2 = o1 + RWKV_ICLR_LORA
    r = xs[:, :width]
    k = xs[:, width:2 * width]
    v = xs[:, 2 * width:w3]
    w_log = -_softplus(-(w0_ref[...] + _mm(jnp.tanh(xs[:, w3:o1]), wup_ref[...], 3))) - 0.5
    log_decay = -jnp.exp(w_log)
    a = jax.nn.sigmoid(a0_ref[...] + _mm(xs[:, o1:o2], aup_ref[...], 3))
    g = _mm(jax.nn.sigmoid(xs[:, o2:]), gup_ref[...], 3)
    hsum = hsum_ref[...]
    kk = k * kk_ref[...]
    kk = kk / jnp.maximum(jnp.sqrt(_mm(kk * kk, hsum, 2)), 1e-12)
    k2 = k * (1.0 + (a - 1.0) * ka_ref[...])
    a_vec = -kk
    b_vec = kk * a
    if t_valid < tt:
        live = row < t_valid
        log_decay = jnp.where(live, log_decay, 0.0)
        a_vec = jnp.where(live, a_vec, 0.0)
        b_vec = jnp.where(live, b_vec, 0.0)
        k2 = jnp.where(live, k2, 0.0)
    vt = v.T

    ri = lax.broadcasted_iota(jnp.int32, (chunk, chunk), 0)
    ci = lax.broadcasted_iota(jnp.int32, (chunk, chunk), 1)
    upper_strict = ri < ci
    upper_incl = ri <= ci
    lower_incl = (ri >= ci).astype(F32)
    n_double = (min(chunk, t_valid) - 1).bit_length()

    heads = range(n_heads)
    hsl = [slice(h * HEAD_DIM, (h + 1) * HEAD_DIM) for h in heads]
    eye = (ri == ci).astype(F32)

    pre = []
    for c in range(tt // chunk):
        rows = slice(c * chunk, (c + 1) * chunk)
        ld = log_decay[rows]
        cl = _mm(lower_incl, ld, 3)
        e_incl = jnp.exp(cl)
        e_inv = jnp.exp(-cl)
        rt = r[rows] * e_incl
        at = a_vec[rows] * jnp.exp(cl - ld)
        bt = b_vec[rows] * e_inv
        kt = k2[rows] * e_inv
        bk = [jnp.concatenate([bt[:, hs], kt[:, hs]], axis=0) for hs in hsl]
        ar = [jnp.concatenate([at[:, hs], rt[:, hs]], axis=0) for hs in hsl]
        gram = [_mm(bk[h], ar[h], passes, _NT) for h in heads]
        n_t = [jnp.where(upper_strict, gm[:chunk, :chunk], 0.0) for gm in gram]
        m2_t = [jnp.where(upper_strict, gm[chunk:, :chunk], 0.0) for gm in gram]
        m34_t = [jnp.concatenate([jnp.where(upper_incl, gm[:chunk, chunk:], 0.0),
                                  jnp.where(upper_incl, gm[chunk:, chunk:], 0.0)], axis=0) for gm in gram]
        vt_c = [vt[hs, rows] for hs in hsl]
        vm = [_mm(vt_c[h], m2_t[h], passes) for h in heads]
        t_inv = [eye + n for n in n_t]
        nk = n_t
        for _ in range(1, n_double):
            nk = [_mm(n, n, passes) for n in nk]
            t_inv = [t_inv[h] + _mm(t_inv[h], nk[h], passes) for h in heads]
        pre.append((rows, bk, ar, m34_t, vt_c, vm, t_inv, e_incl[chunk - 1:chunk, :]))

    state = [s_ref[h] for h in heads]
    for rows, bk, ar, m34_t, vt_c, vm, t_inv, p_end in pre:
        sa = [_mm(state[h], ar[h], passes, _NT) for h in heads]
        x = [_mm(sa[h][:, :chunk] + vm[h], t_inv[h], passes) for h in heads]
        uv = [jnp.concatenate([x[h], vt_c[h]], axis=1) for h in heads]
        for h in heads:
            yt_ref[hsl[h], rows] = sa[h][:, chunk:] + _mm(uv[h], m34_t[h], passes)
        state = [(state[h] + _mm(uv[h], bk[h], passes)) * p_end[:, hsl[h]] for h in heads]
    for h in heads:
        s_ref[h] = state[h]

    y = yt_ref[...].T
    hmean = hsum * (1.0 / HEAD_DIM)
    yc = y - _mm(y, hmean, 2)
    var = _mm(yc * yc, hmean, 2)
    yn = yc * lax.rsqrt(var + RWKV_GN_EPS) * lnw_ref[...] + lnb_ref[...]
    bonus = _mm(r * k2 * rk_ref[...], hsum, 2) * v
    y_ref[...] = (yn + bonus) * g
    wkv_ref[...] = s_ref[...]


def rwkv_mixer(u, shift0, wkv0, p, tt, chunk, passes):
    b, t_valid, n_in = u.shape
    n_heads = wkv0.shape[1]
    width = n_heads * HEAD_DIM
    if t_valid < tt:
        assert t_valid <= RWKV_MAX_CHUNK
        u = _pad_rows(u, tt)
        chunk = tt
    else:
        assert chunk <= RWKV_MAX_CHUNK
        t_valid = tt
    t = u.shape[1]
    assert t % tt == 0 and tt % chunk == 0
    fixed = lambda bi, ti: (0, 0)
    vec = lambda n: pl.BlockSpec((1, n), fixed)
    args = [p['rwkv_mu'].reshape(1, n_in), p['rwkv_w0'].reshape(1, width), p['rwkv_w_up'],
            p['rwkv_a0'].reshape(1, width), p['rwkv_a_up'], p['rwkv_g_up'], p['rwkv_k_k'].reshape(1, width),
            p['rwkv_k_a'].reshape(1, width), p['rwkv_r_k'].reshape(1, width), p['rwkv_ln_w'].reshape(1, width),
            p['rwkv_ln_b'].reshape(1, width), _head_block_matrix(width, 1.0)]
    specs = [vec(n_in), vec(width), pl.BlockSpec(p['rwkv_w_up'].shape, fixed), vec(width),
             pl.BlockSpec(p['rwkv_a_up'].shape, fixed), pl.BlockSpec(p['rwkv_g_up'].shape, fixed),
             vec(width), vec(width), vec(width), vec(width), vec(width), pl.BlockSpec((width, width), fixed)]
    y, wkv, shift = pl.pallas_call(
        functools.partial(_rwkv_kernel, tt=tt, chunk=chunk, width=width, passes=passes, t_valid=t_valid),
        grid=(b, t // tt),
        in_specs=[pl.BlockSpec((None, tt, n_in), lambda bi, ti: (bi, ti, 0)),
                  pl.BlockSpec((None, 1, n_in), lambda bi, ti: (bi, 0, 0)),
                  pl.BlockSpec((None, n_heads, HEAD_DIM, HEAD_DIM), lambda bi, ti: (bi, 0, 0, 0))] + specs,
        out_specs=[pl.BlockSpec((None, tt, width), lambda bi, ti: (bi, ti, 0)),
                   pl.BlockSpec((None, n_heads, HEAD_DIM, HEAD_DIM), lambda bi, ti: (bi, 0, 0, 0)),
                   pl.BlockSpec((None, 1, n_in), lambda bi, ti: (bi, 0, 0))],
        out_shape=[jax.ShapeDtypeStruct((b, t, width), F32),
                   jax.ShapeDtypeStruct((b, n_heads, HEAD_DIM, HEAD_DIM), F32),
                   jax.ShapeDtypeStruct((b, 1, n_in), F32)],
        scratch_shapes=[pltpu.VMEM((1, n_in), F32), pltpu.VMEM((n_heads, HEAD_DIM, HEAD_DIM), F32),
                        pltpu.VMEM((width, tt), F32)],
        compiler_params=_cparams("parallel", "arbitrary"),
        name="rwkv_mixer",
    )(u, shift0.reshape(b, 1, n_in), wkv0, *args)
    return y[:, :t_valid] if t_valid < tt else y, wkv, shift.reshape(b, n_in)


NEG_BIG = -1e30
_TN = (((0,), (0,)), ((), ()))
HEAD_PAIR = 2 * HEAD_DIM


def _proj_qkv_t_kernel(x_ref, g_ref, wt_ref, gq_ref, gk_ref, qt_ref, kt_ref, vt_ref, krow_ref, vtb_ref, kmean_ref,
                       *, width, q_scale, block):
    ti = pl.program_id(1)
    tm = x_ref.shape[0]
    n_heads = width // HEAD_DIM
    h = _rms(x_ref[...], g_ref[...]).astype(BF16)
    ut = _dg(wt_ref[...], h, _NT)

    def head_rms_t(xt, g_col):
        x3 = xt.reshape(n_heads, HEAD_DIM, tm)
        ms = jnp.mean(x3 * x3, axis=1, keepdims=True)
        return (x3 * lax.rsqrt(ms + EPS) * g_col[None, :, :]).reshape(width, tm)

    qt_ref[...] = head_rms_t(ut[:width], gq_ref[...]) * q_scale
    kt = head_rms_t(ut[width:2 * width], gk_ref[...])
    kt_ref[...] = kt
    vt = ut[2 * width:]
    vt_ref[...] = vt
    vtb_ref[...] = vt.astype(BF16)
    k_rows = kt.T
    krow_ref[...] = k_rows.astype(BF16)
    @pl.when(ti == 0)
    def _():
        kmean_ref[...] = jnp.zeros_like(kmean_ref)

    per_tile = tm // block
    blk = lax.broadcasted_iota(jnp.int32, kmean_ref.shape, 0)
    kmean = kmean_ref[...]
    for n in range(per_tile):
        mean = jnp.sum(k_rows[n * block:(n + 1) * block], axis=0, keepdims=True) * (1.0 / block)
        kmean = jnp.where(blk == ti * per_tile + n, mean, kmean)
    kmean_ref[...] = kmean


def proj_qkv_t(x, g, w_t, g_q, g_k, tm, q_scale, layer, depth, prev_kv):
    b, t, d = x.shape
    width = w_t.shape[0] // 3
    nb = t // MOBA_BLOCK
    assert t % tm == 0 and tm % MOBA_BLOCK == 0
    fixed = lambda bi, ti: (0, 0)
    tr = pl.BlockSpec((None, width, tm), lambda bi, ti: (bi, 0, ti))
    slab = pl.BlockSpec((None, None, width, tm), lambda bi, ti: (layer, bi, 0, ti))
    kernel_fn = functools.partial(_proj_qkv_t_kernel, width=width, q_scale=q_scale, block=MOBA_BLOCK)
    in_specs = [pl.BlockSpec((None, tm, d), lambda bi, ti: (bi, ti, 0)), pl.BlockSpec((1, d), fixed),
                pl.BlockSpec((3 * width, d), fixed), pl.BlockSpec((HEAD_DIM, 1), fixed),
                pl.BlockSpec((HEAD_DIM, 1), fixed)]
    args = [x, g.reshape(1, d), w_t, g_q.reshape(HEAD_DIM, 1), g_k.reshape(HEAD_DIM, 1)]
    aliases = {}
    if prev_kv is not None:
        n_in = len(args)
        args += list(prev_kv)
        in_specs += [pl.BlockSpec(memory_space=pl.ANY)] * 2
        aliases = {n_in: 1, n_in + 1: 2}
        body = kernel_fn
        kernel_fn = lambda *refs: body(*refs[:n_in], *refs[n_in + 2:])
    return pl.pallas_call(
        kernel_fn,
        grid=(b, t // tm),
        in_specs=in_specs,
        out_specs=[tr, slab, slab, pl.BlockSpec((None, tm, width), lambda bi, ti: (bi, ti, 0)), tr,
                   pl.BlockSpec((None, nb, width), lambda bi, ti: (bi, 0, 0))],
        out_shape=[jax.ShapeDtypeStruct((b, width, t), F32)] + [jax.ShapeDtypeStruct((depth, b, width, t), F32)] * 2
                  + [jax.ShapeDtypeStruct((b, t, width), BF16), jax.ShapeDtypeStruct((b, width, t), BF16),
                     jax.ShapeDtypeStruct((b, nb, width), F32)],
        input_output_aliases=aliases,
        compiler_params=_cparams("parallel", "arbitrary"),
        name="proj_qkv_t",
    )(*args)


def _head_pair_queries(qt_ref, h, tq):
    p0 = (h // 2) * HEAD_PAIR
    rows = lax.broadcasted_iota(jnp.int32, (HEAD_PAIR, tq), 0) // HEAD_DIM
    return jnp.where(rows == h % 2, qt_ref[p0:p0 + HEAD_PAIR, :], 0.0)


def _sb_prompt_kernel(qt_ref, k_ref, vt_ref, o_ref, qm_ref, acc_ref, run_ref, *, tq, width):
    qi = pl.program_id(1)
    n_heads = width // HEAD_DIM
    krow = lax.broadcasted_iota(jnp.int32, (tq, tq), 0)
    qcol = lax.broadcasted_iota(jnp.int32, (tq, tq), 1)
    from_here = jnp.where(qcol >= krow, 1.0, 0.0).astype(BF16)
    from_here = jnp.concatenate([from_here, from_here], axis=1)
    strict = krow < qcol
    for h in range(n_heads):
        qm_ref[h] = _head_pair_queries(qt_ref, h, tq).astype(BF16)
    acc_ref[...] = jnp.zeros_like(acc_ref)
    run_ref[...] = jnp.zeros_like(run_ref)

    heads = range(n_heads)
    hsl = [slice(h * HEAD_DIM, (h + 1) * HEAD_DIM) for h in heads]

    def block(j, diag):
        ks = pl.ds(pl.multiple_of(j * tq, tq), tq)
        run = run_ref[...]
        z = [_dg(k_ref[ks, (h // 2) * HEAD_PAIR:(h // 2 + 1) * HEAD_PAIR], qm_ref[h], _NN) for h in heads]
        sp = [_softplus2(zh) for zh in z]
        l1m = [jnp.where(strict, -x, 0.0) if diag else -x for x in sp]
        parts = [jnp.concatenate(_split_bf16(x), axis=0) for x in l1m]
        rest = [_dg(from_here, parts[h], _NN) + run[h:h + 1, :] for h in heads]
        w = [jnp.exp2(z[h] + rest[h]) for h in heads]
        if diag:
            w = [jnp.where(strict, x, 0.0) for x in w]
        pv = [_dg(vt_ref[hsl[h], ks], w[h].astype(BF16), _NN) for h in heads]
        for h in heads:
            acc_ref[hsl[h], :] += pv[h]
        run_ref[...] = jnp.concatenate([rest[h][0:1, :] for h in heads], axis=0)

    block(qi, True)

    def body(jj, carry):
        block(qi - 1 - jj, False)
        return carry

    lax.fori_loop(0, qi, body, 0)
    o_ref[...] = acc_ref[...]


def _attn_prompt_call(kernel_fn, name, qt, k_rows, vt_b, extra, extra_specs, scratch, tq):
    b, width, t = qt.shape
    n_heads = width // HEAD_DIM
    tile = pl.BlockSpec((None, width, tq), lambda bi, qi: (bi, 0, qi))
    return pl.pallas_call(
        kernel_fn,
        grid=(b, t // tq),
        in_specs=[tile, pl.BlockSpec((None, t, width), lambda bi, qi: (bi, 0, 0)),
                  pl.BlockSpec((None, width, t), lambda bi, qi: (bi, 0, 0))] + extra_specs,
        out_specs=tile,
        out_shape=jax.ShapeDtypeStruct((b, width, t), F32),
        scratch_shapes=[pltpu.VMEM((n_heads, HEAD_PAIR, tq), BF16), pltpu.VMEM((width, tq), F32)] + scratch,
        compiler_params=_cparams("parallel", "arbitrary"),
        name=name,
    )(qt, k_rows, vt_b, *extra)


def sb_prompt(qt, k_rows, vt_b, tq):
    b, width, t = qt.shape
    assert t % tq == 0
    n_heads = width // HEAD_DIM
    return _attn_prompt_call(functools.partial(_sb_prompt_kernel, tq=tq, width=width), "sb_prompt", qt, k_rows,
                             vt_b, [], [], [pltpu.VMEM((n_heads, tq), F32)], tq)


def _topk_block_mask(gate, n_valid, topk, axis):
    nb = gate.shape[axis]
    idx = lax.broadcasted_iota(jnp.int32, gate.shape, axis)
    rank = jnp.zeros(gate.shape, jnp.int32)
    for m in range(nb):
        gm = gate[:, m:m + 1] if axis == 1 else gate[m:m + 1, :]
        beats = (gm > gate) | ((gm == gate) & (m < idx))
        rank = rank + jnp.where(beats & (m < n_valid), 1, 0)
    return (idx < n_valid) & (rank < topk)


def _moba_prompt_kernel(qt_ref, k_ref, vt_ref, kmean_ref, o_ref, qm_ref, acc_ref, sel_ref, m_ref, l_ref,
                        *, tq, width):
    qi = pl.program_id(1)
    n_heads = width // HEAD_DIM
    krow = lax.broadcasted_iota(jnp.int32, (tq, tq), 0)
    qcol = lax.broadcasted_iota(jnp.int32, (tq, tq), 1)
    causal = krow <= qcol
    for h in range(n_heads):
        p0 = (h // 2) * HEAD_PAIR
        q_pair = _head_pair_queries(qt_ref, h, tq)
        qm_ref[h] = q_pair.astype(BF16)
        gate = _mm(kmean_ref[:, p0:p0 + HEAD_PAIR], q_pair, 3)
        sel_ref[h] = jnp.where(_topk_block_mask(gate, qi, MOBA_TOPK, 0), 1.0, 0.0)
    acc_ref[...] = jnp.zeros_like(acc_ref)
    l_ref[...] = jnp.zeros_like(l_ref)
    m_ref[...] = jnp.full(m_ref.shape, NEG_BIG, F32)

    heads = range(n_heads)
    hsl = [slice(h * HEAD_DIM, (h + 1) * HEAD_DIM) for h in heads]

    def block(n, diag):
        ks = pl.ds(pl.multiple_of(n * tq, tq), tq)
        m_old = m_ref[...]
        l_old = l_ref[...]
        s = [_dg(k_ref[ks, (h // 2) * HEAD_PAIR:(h // 2 + 1) * HEAD_PAIR], qm_ref[h], _NN) for h in heads]
        if diag:
            mask = [causal] * n_heads
        else:
            mask = [jnp.broadcast_to(sel_ref[h, pl.ds(n, 1), :] > 0.5, (tq, tq)) for h in heads]
        s = [jnp.where(mask[h], s[h], NEG_BIG) for h in heads]
        m_new = [jnp.maximum(m_old[h:h + 1, :], jnp.max(s[h], axis=0, keepdims=True)) for h in heads]
        p = [jnp.where(mask[h], jnp.exp(s[h] - m_new[h]), 0.0) for h in heads]
        alpha = [jnp.exp(m_old[h:h + 1, :] - m_new[h]) for h in heads]
        l_new = [alpha[h] * l_old[h:h + 1, :] + jnp.sum(p[h], axis=0, keepdims=True) for h in heads]
        pv = [_dg(vt_ref[hsl[h], ks], p[h].astype(BF16), _NN) for h in heads]
        for h in heads:
            acc_ref[hsl[h], :] = alpha[h] * acc_ref[hsl[h], :] + pv[h]
        m_ref[...] = jnp.concatenate(m_new, axis=0)
        l_ref[...] = jnp.concatenate(l_new, axis=0)

    block(qi, True)

    def body(n, carry):
        block(n, False)
        return carry

    lax.fori_loop(0, qi, body, 0)
    for h in range(n_heads):
        hs = slice(h * HEAD_DIM, (h + 1) * HEAD_DIM)
        o_ref[hs, :] = acc_ref[hs, :] / l_ref[h:h + 1, :]


def moba_prompt(qt, k_rows, vt_b, kmean):
    b, width, t = qt.shape
    tq = MOBA_BLOCK
    nb = t // tq
    assert t % tq == 0
    n_heads = width // HEAD_DIM
    return _attn_prompt_call(functools.partial(_moba_prompt_kernel, tq=tq, width=width), "moba_prompt", qt, k_rows,
                             vt_b, [kmean], [pl.BlockSpec((None, nb, width), lambda bi, qi: (bi, 0, 0))],
                             [pltpu.VMEM((n_heads, nb, tq), F32), pltpu.VMEM((n_heads, tq), F32),
                              pltpu.VMEM((n_heads, tq), F32)], tq)


def _stack_heads(q, n_heads):
    t, width = q.shape
    rows = lax.broadcasted_iota(jnp.int32, (n_heads * t, width), 0) // t
    lanes = lax.broadcasted_iota(jnp.int32, (n_heads * t, width), 1) // HEAD_DIM
    return jnp.where(rows == lanes, jnp.concatenate([q] * n_heads, axis=0), 0.0)


def _unstack_heads(acc, n_heads):
    rows_total, width = acc.shape
    t = rows_total // n_heads
    rows = lax.broadcasted_iota(jnp.int32, (rows_total, width), 0) // t
    lanes = lax.broadcasted_iota(jnp.int32, (rows_total, width), 1) // HEAD_DIM
    kept = jnp.where(rows == lanes, acc, 0.0).reshape(n_heads, t, width)
    return jnp.sum(kept, axis=0)


def _from_here_matrix(tk):
    kj = lax.broadcasted_iota(jnp.int32, (2 * tk, tk), 0) % tk
    ks = lax.broadcasted_iota(jnp.int32, (2 * tk, tk), 1)
    return jnp.where(kj >= ks, 1.0, 0.0).astype(BF16)


def _sum_from_here(x, from_here):
    return _dg(jnp.concatenate(_split_bf16(x), axis=1), from_here, _NN)


def _sb_sample_kernel(pt_ref, q_ref, knew_ref, vnew_ref, *refs, n_pages_step, t_new, n_heads):
    k_refs = refs[:n_pages_step]
    v_refs = refs[n_pages_step:2 * n_pages_step]
    o_ref, qs_ref, acc_ref, run_ref = refs[2 * n_pages_step:]
    step = pl.program_id(1)
    tk = knew_ref.shape[0]
    from_here = _from_here_matrix(tk)

    def tile(k_tile, v_tile, mask, paged):
        z = _dg(qs_ref[...], k_tile.astype(BF16), _NN if paged else _NT)
        sp = _softplus2(z)
        l1m = -sp if mask is None else jnp.where(mask, -sp, 0.0)
        rest = _sum_from_here(l1m, from_here) + run_ref[...]
        w = jnp.exp2(z + rest)
        if mask is not None:
            w = jnp.where(mask, w, 0.0)
        acc_ref[...] += _mm(w, v_tile, 1, _NT if paged else _NN)
        run_ref[...] = rest[:, 0:1]

    @pl.when(step == 0)
    def _():
        qs_ref[...] = _stack_heads(q_ref[...], n_heads).astype(BF16)
        acc_ref[...] = jnp.zeros_like(acc_ref)
        run_ref[...] = jnp.zeros_like(run_ref)
        rows = lax.broadcasted_iota(jnp.int32, (n_heads * t_new, tk), 0) % t_new
        cols = lax.broadcasted_iota(jnp.int32, (n_heads * t_new, tk), 1)
        tile(knew_ref[...], vnew_ref[...], cols < rows, False)

    qs = qs_ref[...]
    z = [_dg(qs, k_refs[i][...].astype(BF16), _NN) for i in range(n_pages_step)]
    cum = [_sum_from_here(-_softplus2(x), from_here) for x in z]
    run = run_ref[...]
    acc = acc_ref[...]
    for i in range(n_pages_step):
        w = jnp.exp2(z[i] + cum[i] + run)
        acc = acc + _mm(w, v_refs[i][...], 1, _NT)
        run = run + cum[i][:, 0:1]
    acc_ref[...] = acc
    run_ref[...] = run

    @pl.when(step == pl.num_programs(1) - 1)
    def _():
        o_ref[...] = _unstack_heads(acc_ref[...], n_heads)


def _page_specs(n_pages, n_pages_step, width, page_size, reverse):
    def spec(i):
        def index(bi, si, pt):
            p = si * n_pages_step + i
            return (pt[bi, n_pages - 1 - p if reverse else p], 0, 0)
        return pl.BlockSpec((None, width, page_size), index)
    return [spec(i) for i in range(n_pages_step)]


def _pad_rows(x, rows):
    return jnp.pad(x, ((0, 0), (0, rows - x.shape[1]), (0, 0)))


def sb_sample(q, k_new, v_new, k_pool, v_pool, pages, n_pages_step):
    b, t_new, width = q.shape
    n_heads = width // HEAD_DIM
    page_size = k_pool.shape[2]
    n_pages = pages.shape[1]
    assert n_pages % n_pages_step == 0 and t_new <= page_size
    new = pl.BlockSpec((None, t_new, width), lambda bi, si, pt: (bi, 0, 0))
    new_pad = pl.BlockSpec((None, page_size, width), lambda bi, si, pt: (bi, 0, 0))
    page_specs = _page_specs(n_pages, n_pages_step, width, page_size, reverse=True)
    return pl.pallas_call(
        functools.partial(_sb_sample_kernel, n_pages_step=n_pages_step, t_new=t_new, n_heads=n_heads),
        grid_spec=pltpu.PrefetchScalarGridSpec(
            num_scalar_prefetch=1, grid=(b, n_pages // n_pages_step),
            in_specs=[new, new_pad, new_pad] + page_specs + page_specs,
            out_specs=new,
            scratch_shapes=[pltpu.VMEM((n_heads * t_new, width), BF16),
                            pltpu.VMEM((n_heads * t_new, width), F32),
                            pltpu.VMEM((n_heads * t_new, 1), F32)]),
        out_shape=jax.ShapeDtypeStruct((b, t_new, width), F32),
        compiler_params=_cparams("parallel", "arbitrary"),
        name="sb_sample",
    )(pages, q, _pad_rows(k_new, page_size), _pad_rows(v_new, page_size),
      *([k_pool] * n_pages_step), *([v_pool] * n_pages_step))


def _block_mean_kernel(pt_ref, *refs, n_pages_step, pages_per_block):
    k_refs = refs[:n_pages_step]
    o_ref = refs[n_pages_step]
    page_size = k_refs[0].shape[1]
    scale = 1.0 / (pages_per_block * page_size)
    ones = jnp.ones((SUBLANES, page_size), BF16)
    means = []
    for n in range(n_pages_step // pages_per_block):
        tot = None
        for i in range(pages_per_block):
            hi, lo = _split_bf16(k_refs[n * pages_per_block + i][...])
            part = _dg(ones, hi, _NT) + _dg(ones, lo, _NT)
            tot = part if tot is None else tot + part
        means.append(tot[0:1, :] * scale)
    o_ref[...] = jnp.concatenate(means, axis=0)


def paged_block_means(k_pool, pages, block):
    b, n_pages = pages.shape
    width, page_size = k_pool.shape[1:]
    pages_per_block = block // page_size
    n_pages_step = SUBLANES * pages_per_block
    assert block % page_size == 0 and n_pages % n_pages_step == 0
    return pl.pallas_call(
        functools.partial(_block_mean_kernel, n_pages_step=n_pages_step, pages_per_block=pages_per_block),
        grid_spec=pltpu.PrefetchScalarGridSpec(
            num_scalar_prefetch=1, grid=(b, n_pages // n_pages_step),
            in_specs=_page_specs(n_pages, n_pages_step, width, page_size, reverse=False),
            out_specs=pl.BlockSpec((None, SUBLANES, width), lambda bi, si, pt: (bi, si, 0))),
        out_shape=jax.ShapeDtypeStruct((b, n_pages // pages_per_block, width), F32),
        compiler_params=_cparams("parallel", "arbitrary"),
        name="paged_block_means",
    )(pages, *([k_pool] * n_pages_step))


def _moba_sample_kernel(pt_ref, q_ref, kmean_ref, knew_ref, vnew_ref, *refs, n_pages_step, pages_per_block,
                        t_new, n_heads):
    k_refs = refs[:n_pages_step]
    v_refs = refs[n_pages_step:2 * n_pages_step]
    o_ref, qs_ref, sel_ref, acc_ref, m_ref, l_ref = refs[2 * n_pages_step:]
    step = pl.program_id(1)
    nb = kmean_ref.shape[0]
    rows_total = n_heads * t_new

    def tile(k_tile, v_tile, mask, paged):
        s = jnp.where(mask, _dg(qs_ref[...], k_tile.astype(BF16), _NN if paged else _NT), NEG_BIG)
        m_new = jnp.maximum(m_ref[...], jnp.max(s, axis=1, keepdims=True))
        p = jnp.where(mask, jnp.exp(s - m_new), 0.0)
        alpha = jnp.exp(m_ref[...] - m_new)
        l_ref[...] = alpha * l_ref[...] + jnp.sum(p, axis=1, keepdims=True)
        acc_ref[...] = alpha * acc_ref[...] + _mm(p, v_tile, 1, _NT if paged else _NN)
        m_ref[...] = m_new

    @pl.when(step == 0)
    def _():
        qs = _stack_heads(q_ref[...], n_heads)
        qs_ref[...] = qs.astype(BF16)
        gate = _mm(qs, kmean_ref[...], 3, _NT)
        sel_ref[...] = jnp.where(_topk_block_mask(gate, nb, MOBA_TOPK, 1), 1.0, 0.0)
        acc_ref[...] = jnp.zeros_like(acc_ref)
        l_ref[...] = jnp.zeros_like(l_ref)
        m_ref[...] = jnp.full(m_ref.shape, NEG_BIG, F32)
        tk = knew_ref.shape[0]
        rows = lax.broadcasted_iota(jnp.int32, (rows_total, tk), 0) % t_new
        cols = lax.broadcasted_iota(jnp.int32, (rows_total, tk), 1)
        tile(knew_ref[...], vnew_ref[...], cols <= rows, False)

    qs = qs_ref[...]
    page_size = k_refs[0].shape[1]
    n_keys = n_pages_step * page_size
    key_blk = (step * n_keys + lax.broadcasted_iota(jnp.int32, (nb, n_keys), 1)) // MOBA_BLOCK
    expand = jnp.where(key_blk == lax.broadcasted_iota(jnp.int32, (nb, n_keys), 0), 1.0, 0.0).astype(BF16)
    picked = _dg(sel_ref[...].astype(BF16), expand, _NN) > 0.5
    mask = [picked[:, i * page_size:(i + 1) * page_size] for i in range(n_pages_step)]
    s = [jnp.where(mask[i], _dg(qs, k_refs[i][...].astype(BF16), _NN), NEG_BIG) for i in range(n_pages_step)]
    s_max = s[0]
    for x in s[1:]:
        s_max = jnp.maximum(s_max, x)
    m_old = m_ref[...]
    m_new = jnp.maximum(m_old, jnp.max(s_max, axis=1, keepdims=True))
    p = [jnp.where(mask[i], jnp.exp(s[i] - m_new), 0.0) for i in range(n_pages_step)]
    alpha = jnp.exp(m_old - m_new)
    p_sum = p[0]
    acc = alpha * acc_ref[...] + _mm(p[0], v_refs[0][...], 1, _NT)
    for i in range(1, n_pages_step):
        p_sum = p_sum + p[i]
        acc = acc + _mm(p[i], v_refs[i][...], 1, _NT)
    l_ref[...] = alpha * l_ref[...] + jnp.sum(p_sum, axis=1, keepdims=True)
    acc_ref[...] = acc
    m_ref[...] = m_new

    @pl.when(step == pl.num_programs(1) - 1)
    def _():
        o_ref[...] = _unstack_heads(acc_ref[...] / l_ref[...], n_heads)


def moba_sample(q, k_new, v_new, k_pool, v_pool, pages, kmean, n_pages_step):
    b, t_new, width = q.shape
    n_heads = width // HEAD_DIM
    page_size = k_pool.shape[2]
    n_pages = pages.shape[1]
    nb = kmean.shape[1]
    pages_per_block = MOBA_BLOCK // page_size
    assert n_pages % n_pages_step == 0 and t_new <= page_size and n_pages == nb * pages_per_block
    assert (n_pages * page_size) % MOBA_BLOCK == 0 and t_new <= MOBA_BLOCK
    new = pl.BlockSpec((None, t_new, width), lambda bi, si, pt: (bi, 0, 0))
    new_pad = pl.BlockSpec((None, page_size, width), lambda bi, si, pt: (bi, 0, 0))
    page_specs = _page_specs(n_pages, n_pages_step, width, page_size, reverse=False)
    rows_total = n_heads * t_new
    return pl.pallas_call(
        functools.partial(_moba_sample_kernel, n_pages_step=n_pages_step, pages_per_block=pages_per_block,
                          t_new=t_new, n_heads=n_heads),
        grid_spec=pltpu.PrefetchScalarGridSpec(
            num_scalar_prefetch=1, grid=(b, n_pages // n_pages_step),
            in_specs=[new, pl.BlockSpec((None, nb, width), lambda bi, si, pt: (bi, 0, 0)), new_pad, new_pad]
                     + page_specs + page_specs,
            out_specs=new,
            scratch_shapes=[pltpu.VMEM((rows_total, width), BF16), pltpu.VMEM((rows_total, nb), F32),
                            pltpu.VMEM((rows_total, width), F32), pltpu.VMEM((rows_total, 1), F32),
                            pltpu.VMEM((rows_total, 1), F32)]),
        out_shape=jax.ShapeDtypeStruct((b, t_new, width), F32),
        compiler_params=_cparams("parallel", "arbitrary"),
        name="moba_sample",
    )(pages, q, kmean, _pad_rows(k_new, page_size), _pad_rows(v_new, page_size),
      *([k_pool] * n_pages_step), *([v_pool] * n_pages_step))


def _merge_kernel(x_ref, g_ref, wg_ref, y0_ref, y1_ref, y2_ref, y3_ref, wb_ref, wo_ref, o_ref, *, d_model, layouts):
    x = x_ref[...]
    h = _rms(x, g_ref[...]).astype(BF16)
    merged = None
    for n, (y_ref, layout) in enumerate(zip((y0_ref, y1_ref, y2_ref, y3_ref), layouts)):
        gate = jax.nn.sigmoid(_dg(h, wg_ref[:, n * d_model:(n + 1) * d_model], _NN))
        term = gate * _mm(y_ref[...], wb_ref[n], 1, _TN if layout == "transposed" else _NN)
        merged = term if merged is None else merged + term
    o_ref[...] = x + _mm(merged, wo_ref[...])


def merge(x, g, w_gate, ys, layouts, w_branch, w_out, tm):
    b, t, d = x.shape
    width = w_branch.shape[1]
    tm = min(tm, t)
    assert t % tm == 0
    fixed = lambda bi, ti: (0, 0)
    seq = lambda bi, ti: (bi, ti, 0)
    y_spec = {"rows": pl.BlockSpec((None, tm, width), seq),
              "time_major": pl.BlockSpec((tm, width), lambda bi, ti: (ti, bi)),
              "transposed": pl.BlockSpec((None, width, tm), lambda bi, ti: (bi, 0, ti))}
    return pl.pallas_call(
        functools.partial(_merge_kernel, d_model=d, layouts=tuple(layouts)),
        grid=(b, t // tm),
        in_specs=[pl.BlockSpec((None, tm, d), seq), pl.BlockSpec((1, d), fixed), _resident((d, N_BRANCH * d))]
                 + [y_spec[k] for k in layouts]
                 + [_resident((N_BRANCH, width, d)), _resident((d, d))],
        out_specs=pl.BlockSpec((None, tm, d), seq),
        out_shape=jax.ShapeDtypeStruct((b, t, d), F32),
        compiler_params=_cparams("parallel", "parallel"),
        name="merge",
    )(x, g.reshape(1, d), w_gate, *ys, w_branch, w_out)


def _ffn_kernel(x_ref, g_ref, wup_ref, cw_ref, cb_ref, wdn_ref, conv0_ref, o_ref, conv_ref, st_ref,
                *, nseq, tt, d_ff, col_chunk):
    ti = pl.program_id(1)

    @pl.when(ti == 0)
    def _():
        st_ref[...] = conv0_ref[...]

    d_model = x_ref.shape[-1]
    x = x_ref[...].reshape(nseq * tt, d_model)
    h = _rms(x, g_ref[...]).astype(BF16)
    t_idx = lax.broadcasted_iota(jnp.int32, (1, tt, 1), 1)

    def conv_cols(c0):
        cols = slice(c0, c0 + col_chunk)
        up = _dg(h, wup_ref[:, cols], _NN).reshape(nseq, tt, col_chunk)
        st = st_ref[:, :, cols]
        p1 = jnp.where(t_idx == 0, st[:, 1:2, :], pltpu.roll(up, 1, axis=1))
        p2 = jnp.where(t_idx == 0, st[:, 0:1, :], jnp.where(t_idx == 1, st[:, 1:2, :], pltpu.roll(up, 2, axis=1)))
        st_ref[:, :, cols] = up[:, tt - 2:, :]
        cw = cw_ref[:, cols]
        c = cb_ref[:, cols] + cw[2:3, :] * up + cw[1:2, :] * p1 + cw[0:1, :] * p2
        return c.reshape(nseq * tt, col_chunk)

    acc = x
    for c0 in range(0, d_ff, col_chunk):
        a = conv_cols(c0)
        b = conv_cols(d_ff + c0)
        acc = acc + _mm(a * jax.nn.sigmoid(a) * b, wdn_ref[c0:c0 + col_chunk, :])
    o_ref[...] = acc.reshape(nseq, tt, d_model)
    conv_ref[...] = st_ref[...]


def conv_ffn(x, g, w_up, conv_w, conv_b, w_down, conv0, nseq, tt, col_chunk):
    b, t, d = x.shape
    d_ff = w_down.shape[0]
    assert b % nseq == 0 and t % tt == 0 and d_ff % col_chunk == 0 and tt >= CONV_W - 1
    fixed = lambda bi, ti: (0, 0)
    seq = lambda bi, ti: (bi, ti, 0)
    st = lambda bi, ti: (bi, 0, 0)
    return pl.pallas_call(
        functools.partial(_ffn_kernel, nseq=nseq, tt=tt, d_ff=d_ff, col_chunk=col_chunk),
        grid=(b // nseq, t // tt),
        in_specs=[pl.BlockSpec((nseq, tt, d), seq), pl.BlockSpec((1, d), fixed),
                  _resident((d, 2 * d_ff)), pl.BlockSpec((CONV_W, 2 * d_ff), fixed),
                  pl.BlockSpec((1, 2 * d_ff), fixed), _resident((d_ff, d)),
                  pl.BlockSpec((nseq, CONV_W - 1, 2 * d_ff), st)],
        out_specs=[pl.BlockSpec((nseq, tt, d), seq), pl.BlockSpec((nseq, CONV_W - 1, 2 * d_ff), st)],
        out_shape=[jax.ShapeDtypeStruct((b, t, d), F32), jax.ShapeDtypeStruct((b, CONV_W - 1, 2 * d_ff), F32)],
        scratch_shapes=[pltpu.VMEM((nseq, CONV_W - 1, 2 * d_ff), F32)],
        compiler_params=_cparams("parallel", "arbitrary"),
        name="conv_ffn",
    )(x, g.reshape(1, d), w_up, conv_w, conv_b.reshape(1, 2 * d_ff), w_down, conv0)


MOBA_Q_SCALE = HEAD_DIM ** -0.5
SB_Q_SCALE = HEAD_DIM ** -0.5 * LOG2_E

PROMPT_TILES = dict(proj_tm=1024, qkv_tm=512, s5_tt=128, s5_passes=1, rwkv_tt=256, rwkv_chunk=64, rwkv_passes=1,
                    sb_tq=256, merge_tm=512, ffn_tt=512, ffn_cols=1408)
SAMPLE_TILES = dict(proj_tm=256, s5_tt=8, s5_passes=3, rwkv_tt=128, rwkv_chunk=64, rwkv_passes=1,
                    merge_tm=256, ffn_cols=1408, sb_pages=16, moba_pages=16)


def _layer(x, st, past, p, tiles):
    b, t, d = x.shape
    width = d // 2
    n_heads = width // HEAD_DIM
    n_rwkv = 3 * width + RWKV_DECAY_LORA + RWKV_ICLR_LORA + RWKV_GATE_LORA
    prompt = past is None
    n_state = p['s5_ab_re'].shape[0]
    s5_args = (st['s5_re'].reshape(b, n_state), st['s5_im'].reshape(b, n_state), p['s5_ab_re'], p['s5_ab_im'],
               p['s5_bb'][tiles['s5_passes']], p['s5_cc'][tiles['s5_passes']], p['s5_d'], p['s5_w_glu'],
               tiles['s5_tt'], tiles['s5_passes'])

    if prompt:
        u_s5, u_rw = proj_plain(x, p['norm1_g'], p['w_mix'], (width, n_rwkv), tiles['proj_tm'], True)
        y_s5, s5_re, s5_im = s5_mixer(u_s5.reshape(t, b, width), *s5_args)
        y_s5 = y_s5.reshape(t, b * width)
        qt, k_sb, v_sb, k_rows, vt_b, _ = proj_qkv_t(x, p['norm1_g'], p['w_sb'].T, p['sb_q_g'], p['sb_k_g'],
                                                     tiles['qkv_tm'], SB_Q_SCALE, p['layer'], p['depth'],
                                                     st['kv_sb'])
        y_sb = sb_prompt(qt, k_rows, vt_b, tiles['sb_tq'])
        qt, k_mb, v_mb, k_rows, vt_b, kmean = proj_qkv_t(x, p['norm1_g'], p['w_mb'].T, p['moba_q_g'],
                                                         p['moba_k_g'], tiles['qkv_tm'], MOBA_Q_SCALE,
                                                         p['layer'], p['depth'], st['kv_mb'])
        y_mb = moba_prompt(qt, k_rows, vt_b, kmean)
        layouts = ("time_major", "rows", "transposed", "transposed")
        x_m = x
        kv = lambda a: a
    else:
        x_m = x.reshape(1, b * t, d)
        u_s5, u_rw = proj_plain(x_m, p['norm1_g'], p['w_mix'], (width, n_rwkv), b * t, False)
        y_s5, s5_re, s5_im = s5_mixer(u_s5.reshape(b, t, width).transpose(1, 0, 2), *s5_args)
        y_s5 = y_s5.transpose(1, 0, 2).reshape(1, b * t, width)
        x2 = x.reshape(b * t, d)
        seq = lambda a: a.reshape(b, t, width)
        q_sb, k_sb, v_sb = proj_qkv(x2, p['norm1_g'], p['w_sb'], p['sb_q_g'], p['sb_k_g'], b * t,
                                    SB_Q_SCALE)
        q_mb, k_mb, v_mb = proj_qkv(x2, p['norm1_g'], p['w_mb'], p['moba_q_g'], p['moba_k_g'], b * t,
                                    MOBA_Q_SCALE)
        y_sb = sb_sample(seq(q_sb), seq(k_sb), seq(v_sb), past['sb_k'], past['sb_v'], past['pages'],
                         tiles['sb_pages'])
        kmean = paged_block_means(past['moba_k'], past['pages'], MOBA_BLOCK)
        y_mb = moba_sample(seq(q_mb), seq(k_mb), seq(v_mb), past['moba_k'], past['moba_v'], past['pages'], kmean,
                           tiles['moba_pages'])
        y_sb, y_mb = y_sb.reshape(1, b * t, width), y_mb.reshape(1, b * t, width)
        layouts = ("rows",) * N_BRANCH
        kv = lambda a: a.reshape(b, t, n_heads, HEAD_DIM)

    y_rw, wkv, shift = rwkv_mixer(u_rw.reshape(b, t, n_rwkv), st['shift'], st['wkv'], p, tiles['rwkv_tt'],
                                  tiles['rwkv_chunk'], tiles['rwkv_passes'])
    ys = [y_s5, y_rw.reshape(x_m.shape[0], x_m.shape[1], width), y_sb, y_mb]
    x1 = merge(x_m, p['norm1_g'], p['w_gate'], ys, layouts, p['w_branch'], p['w_out'], tiles['merge_tm'])
    nseq, tt = (1, tiles['ffn_tt']) if prompt else (b, t)
    x_out, conv = conv_ffn(x1.reshape(b, t, d), p['norm2_g'], p['ffn_w_up'], p['ffn_conv_w'], p['ffn_conv_b'],
                           p['ffn_w_down'], st['conv'], nseq, tt, tiles['ffn_cols'])
    new_st = (s5_re.reshape(st['s5_re'].shape), s5_im.reshape(st['s5_im'].shape), wkv, shift, conv,
              kv(k_sb), kv(v_sb), kv(k_mb), kv(v_mb))
    return x_out, new_st


def kernel(x_prompt, x_sample, state_s5_re, state_s5_im, state_rwkv_wkv, state_rwkv_shift, state_ffn_conv, cache_sb_k, cache_sb_v, cache_moba_k, cache_moba_v, page_table, norm1_g, w_in, s5_a_re, s5_a_im, s5_log_dt, s5_b_re, s5_b_im, s5_c_re, s5_c_im, s5_d, s5_w_glu, rwkv_mu, rwkv_w0, rwkv_w_up, rwkv_a0, rwkv_a_up, rwkv_g_up, rwkv_k_k, rwkv_k_a, rwkv_r_k, rwkv_ln_w, rwkv_ln_b, sb_q_g, sb_k_g, moba_q_g, moba_k_g, w_branch, w_out, norm2_g, ffn_w_up, ffn_conv_w, ffn_conv_b, ffn_w_down):
    depth = w_in.shape[0]
    bp = x_prompt.shape[0]
    n_pool, page_size, n_heads, head_dim = cache_sb_k.shape[1:]
    width = n_heads * head_dim
    pool = lambda c: c.transpose(0, 1, 3, 4, 2).reshape(depth * n_pool, width, page_size)
    pools = dict(sb_k=pool(cache_sb_k), sb_v=pool(cache_sb_v), moba_k=pool(cache_moba_k), moba_v=pool(cache_moba_v))
    g_s5, p_s5 = s5_a_re.shape[1:]
    n_rwkv = state_rwkv_shift.shape[-1]
    d_ff2 = state_ffn_conv.shape[-1]
    zero_st = dict(s5_re=jnp.zeros((bp, g_s5, p_s5), F32), s5_im=jnp.zeros((bp, g_s5, p_s5), F32),
                   shift=jnp.zeros((bp, n_rwkv), F32), wkv=jnp.zeros((bp, n_heads, head_dim, head_dim), F32),
                   conv=jnp.zeros((bp, CONV_W - 1, d_ff2), F32))
    yp, ys = x_prompt, x_sample
    states_p, states_s = [], []
    kv_sb = kv_mb = None
    for l in range(depth):
        ab_re, ab_im, bb, cc = s5_operands(s5_a_re[l], s5_a_im[l], s5_log_dt[l], s5_b_re[l], s5_b_im[l],
                                           s5_c_re[l], s5_c_im[l])
        width_l = w_in.shape[1] // 2
        off_sb = width_l + state_rwkv_shift.shape[-1]
        off_mb, off_gate = off_sb + 3 * width_l, off_sb + 6 * width_l
        w_piece = lambda a, b: w_in[l, :, a:b].astype(BF16)
        p = dict(layer=l, depth=depth, norm1_g=norm1_g[l], w_mix=w_piece(0, off_sb), w_sb=w_piece(off_sb, off_mb),
                 w_mb=w_piece(off_mb, off_gate), w_gate=w_piece(off_gate, w_in.shape[2]),
                 s5_ab_re=ab_re, s5_ab_im=ab_im,
                 s5_bb={1: bb.astype(BF16), 3: bb}, s5_cc={1: cc.astype(BF16), 3: cc}, s5_d=s5_d[l],
                 s5_w_glu=s5_w_glu[l].astype(BF16), rwkv_mu=rwkv_mu[l], rwkv_w0=rwkv_w0[l],
                 rwkv_w_up=rwkv_w_up[l], rwkv_a0=rwkv_a0[l], rwkv_a_up=rwkv_a_up[l], rwkv_g_up=rwkv_g_up[l],
                 rwkv_k_k=rwkv_k_k[l], rwkv_k_a=rwkv_k_a[l], rwkv_r_k=rwkv_r_k[l].reshape(-1),
                 rwkv_ln_w=rwkv_ln_w[l], rwkv_ln_b=rwkv_ln_b[l], sb_q_g=sb_q_g[l], sb_k_g=sb_k_g[l],
                 moba_q_g=moba_q_g[l], moba_k_g=moba_k_g[l], w_branch=w_branch[l].astype(BF16),
                 w_out=w_out[l].astype(BF16), norm2_g=norm2_g[l], ffn_w_up=ffn_w_up[l].astype(BF16),
                 ffn_conv_w=ffn_conv_w[l], ffn_conv_b=ffn_conv_b[l], ffn_w_down=ffn_w_down[l].astype(BF16))
        yp, st_p = _layer(yp, dict(zero_st, kv_sb=kv_sb, kv_mb=kv_mb), None, p, PROMPT_TILES)
        kv_sb, kv_mb = st_p[5:7], st_p[7:9]
        st_s = dict(s5_re=state_s5_re[l], s5_im=state_s5_im[l], shift=state_rwkv_shift[l], wkv=state_rwkv_wkv[l],
                    conv=state_ffn_conv[l])
        past = dict(pools, pages=page_table + l * n_pool)
        ys, st_s = _layer(ys, st_s, past, p, SAMPLE_TILES)
        states_p.append(st_p[:5])
        states_s.append(st_s)
    t_p = x_prompt.shape[1]
    kv_out = lambda a: a.reshape(depth, bp, n_heads, head_dim, t_p).transpose(0, 1, 4, 2, 3)
    stacked_p = [jnp.stack(z, axis=0) for z in zip(*states_p)] + [kv_out(a) for a in (*kv_sb, *kv_mb)]
    stacked_s = [jnp.stack(z, axis=0) for z in zip(*states_s)]
    out = [yp, ys]
    for a, c in zip(stacked_p, stacked_s):
        out += [a, c]
    return tuple(out)
```

```python
import functools
import math

import jax
import jax.numpy as jnp
from jax import lax
from jax.experimental import pallas as pl
from jax.experimental.pallas import tpu as pltpu

F32 = jnp.float32
BF16 = jnp.bfloat16

HEAD_DIM = 64
S5_GROUP = 16
S5_STATE = 64
RWKV_DECAY_LORA = 64
RWKV_ICLR_LORA = 64
RWKV_GATE_LORA = 128
N_BRANCH = 4
MOBA_BLOCK = 256
MOBA_TOPK = 3
CONV_W = 3
EPS = 1e-6
RWKV_GN_EPS = 64e-5
RWKV_MAX_CHUNK = 64

V7X_VMEM_LIMIT_BYTES = 56 * 1024 * 1024
SUBLANES = 8
LANES = 128


def _resident(shape):
    zeros = (0,) * len(shape)
    return pl.BlockSpec(shape, lambda *_: zeros, pipeline_mode=pl.Buffered(1))


def _cparams(*sem):
    return pltpu.CompilerParams(dimension_semantics=sem, vmem_limit_bytes=V7X_VMEM_LIMIT_BYTES)


def _split_bf16(x):
    hi = x.astype(BF16)
    lo = (x - hi.astype(F32)).astype(BF16)
    return hi, lo


_NN = (((1,), (0,)), ((), ()))
_NT = (((1,), (1,)), ((), ()))


def _dg(a, b, dims):
    return lax.dot_general(a, b, dims, preferred_element_type=F32)


def _mm(a, b, passes=1, dims=_NN):
    if passes == 1:
        return _dg(a.astype(BF16), b.astype(BF16), dims)
    ah, al = _split_bf16(a)
    if b.dtype == BF16:
        return _dg(ah, b, dims) + _dg(al, b, dims)
    bh, bl = _split_bf16(b)
    if passes == 2:
        return _dg(ah, bh, dims) + _dg(al, bh, dims)
    return _dg(ah, bh, dims) + (_dg(al, bh, dims) + _dg(ah, bl, dims))


def _head_block_matrix(width, value):
    r = jnp.arange(width) // HEAD_DIM
    return jnp.where(r[:, None] == r[None, :], value, 0.0).astype(BF16)


def _rms(x, g):
    return x * lax.rsqrt(jnp.mean(x * x, axis=-1, keepdims=True) + EPS) * g


def _proj_plain_kernel(x_ref, g_ref, w_ref, *out_refs, widths):
    h = _rms(x_ref[...], g_ref[...]).astype(BF16)
    u = _dg(h, w_ref[...], _NN)
    off = 0
    for o_ref, n in zip(out_refs, widths):
        o_ref[...] = u[:, off:off + n]
        off += n


def proj_plain(x, g, w, widths, tm, first_time_major):
    b, t, d = x.shape
    n = w.shape[1]
    assert sum(widths) == n and t % tm == 0
    fixed = lambda bi, ti: (0, 0)
    seq = lambda bi, ti: (bi, ti, 0)
    out_specs = [pl.BlockSpec((None, tm, k), seq) for k in widths]
    out_shape = [jax.ShapeDtypeStruct((b, t, k), F32) for k in widths]
    if first_time_major:
        out_specs[0] = pl.BlockSpec((tm, widths[0]), lambda bi, ti: (ti, bi))
        out_shape[0] = jax.ShapeDtypeStruct((t, b * widths[0]), F32)
    return pl.pallas_call(
        functools.partial(_proj_plain_kernel, widths=widths),
        grid=(b, t // tm),
        in_specs=[pl.BlockSpec((None, tm, d), seq), pl.BlockSpec((1, d), fixed), _resident((d, n))],
        out_specs=out_specs,
        out_shape=out_shape,
        compiler_params=_cparams("parallel", "parallel"),
        name="proj_plain",
    )(x, g.reshape(1, d), w)


def _head_rms(x, hm, g):
    ms = _mm(x * x, hm, passes=2)
    return x * lax.rsqrt(ms + EPS) * g


def _proj_qkv_kernel(x_ref, g_ref, w_ref, hm_ref, gq_ref, gk_ref, q_ref, k_ref, v_ref, *, width, q_scale):
    h = _rms(x_ref[...], g_ref[...]).astype(BF16)
    u = _dg(h, w_ref[...], _NN)
    hm = hm_ref[...]
    q_ref[...] = _head_rms(u[:, :width], hm, gq_ref[...]) * q_scale
    k_ref[...] = _head_rms(u[:, width:2 * width], hm, gk_ref[...])
    v_ref[...] = u[:, 2 * width:]


def proj_qkv(x2d, g, w, g_q, g_k, tm, q_scale):
    m, d = x2d.shape
    width = w.shape[1] // 3
    n_heads = width // HEAD_DIM
    hm = _head_block_matrix(width, 1.0 / HEAD_DIM)
    row = lambda i: (i, 0)
    fixed = lambda i: (0, 0)
    return pl.pallas_call(
        functools.partial(_proj_qkv_kernel, width=width, q_scale=q_scale),
        grid=(m // tm,),
        in_specs=[pl.BlockSpec((tm, d), row), pl.BlockSpec((1, d), fixed),
                  pl.BlockSpec((d, 3 * width), fixed), pl.BlockSpec((width, width), fixed),
                  pl.BlockSpec((1, width), fixed), pl.BlockSpec((1, width), fixed)],
        out_specs=[pl.BlockSpec((tm, width), row)] * 3,
        out_shape=[jax.ShapeDtypeStruct((m, width), F32)] * 3,
        compiler_params=_cparams("parallel"),
        name="proj_qkv",
    )(x2d, g.reshape(1, d), w, hm, jnp.tile(g_q, n_heads).reshape(1, width),
      jnp.tile(g_k, n_heads).reshape(1, width))


def _s5_disc_kernel(are_ref, aim_ref, ldt_ref, bre_ref, bim_ref, abre_ref, abim_ref, bbre_ref, bbim_ref):
    lr = are_ref[...]
    li = aim_ref[...]
    dt = jnp.exp(ldt_ref[...])
    mag = jnp.exp(lr * dt)
    ab_re = mag * jnp.cos(li * dt)
    ab_im = mag * jnp.sin(li * dt)
    den = lr * lr + li * li
    zr = ((ab_re - 1.0) * lr + ab_im * li) / den
    zi = (ab_im * lr - (ab_re - 1.0) * li) / den
    abre_ref[...] = ab_re
    abim_ref[...] = ab_im
    br = bre_ref[...]
    bi = bim_ref[...]
    bbre_ref[...] = zr[:, None, :] * br - zi[:, None, :] * bi
    bbim_ref[...] = zr[:, None, :] * bi + zi[:, None, :] * br


def s5_discretize(a_re, a_im, log_dt, b_re, b_im):
    g, p = a_re.shape
    gc = b_re.shape[1]
    return pl.pallas_call(
        _s5_disc_kernel,
        out_shape=[jax.ShapeDtypeStruct((g, p), F32)] * 2 + [jax.ShapeDtypeStruct((g, gc, p), F32)] * 2,
        name="s5_discretize",
    )(a_re, a_im, log_dt.reshape(g, 1), b_re, b_im)


def _gelu_tanh(x):
    c = math.sqrt(2.0 / math.pi)
    return 0.5 * x * (1.0 + jnp.tanh(c * (x + 0.044715 * (x * x * x))))


def _s5_kernel(u_ref, s0re_ref, s0im_ref, abre_ref, abim_ref, bb_ref, cc_ref, d_ref, wglu_ref,
               y_ref, sre_ref, sim_ref, xs_ref, cre_ref, cim_ref, *, tt, n_state, passes, lane_chunk):
    ti = pl.program_id(1)

    @pl.when(ti == 0)
    def _():
        cre_ref[...] = s0re_ref[...]
        cim_ref[...] = s0im_ref[...]

    width = u_ref.shape[-1]
    n_cl, c_in, c_state = bb_ref.shape
    half = c_state // 2
    u = u_ref[...].reshape(tt * SUBLANES, width)
    for c in range(n_cl):
        xc = _mm(u[:, c * c_in:(c + 1) * c_in], bb_ref[c], passes)
        xs_ref[:, c * half:(c + 1) * half] = xc[:, :half]
        xs_ref[:, n_state + c * half:n_state + (c + 1) * half] = xc[:, half:]

    for c0 in range(0, n_state, lane_chunk):
        re_sl = pl.ds(c0, lane_chunk)
        im_sl = pl.ds(n_state + c0, lane_chunk)
        a_re = jnp.broadcast_to(abre_ref[:, re_sl], (SUBLANES, lane_chunk))
        a_im = jnp.broadcast_to(abim_ref[:, re_sl], (SUBLANES, lane_chunk))

        def step(t, carry):
            s_re, s_im = carry
            rows = pl.ds(pl.multiple_of(t * SUBLANES, SUBLANES), SUBLANES)
            n_re = a_re * s_re - a_im * s_im + xs_ref[rows, re_sl]
            n_im = a_re * s_im + a_im * s_re + xs_ref[rows, im_sl]
            xs_ref[rows, re_sl] = n_re
            xs_ref[rows, im_sl] = n_im
            return n_re, n_im

        s_re, s_im = lax.fori_loop(0, tt, step, (cre_ref[:, re_sl], cim_ref[:, re_sl]), unroll=8)
        cre_ref[:, re_sl] = s_re
        cim_ref[:, re_sl] = s_im

    y = jnp.concatenate(
        [_mm(xs_ref[:, c * half:(c + 1) * half], cc_ref[c, :half, :], passes)
         + _mm(xs_ref[:, n_state + c * half:n_state + (c + 1) * half], cc_ref[c, half:, :], passes)
         for c in range(n_cl)], axis=1) + d_ref[...] * u
    y = _gelu_tanh(y)
    y = y * jax.nn.sigmoid(_mm(y, wglu_ref[...]))
    y_ref[...] = y.reshape(tt, SUBLANES, width)
    sre_ref[...] = cre_ref[...]
    sim_ref[...] = cim_ref[...]


def s5_mixer(u_tm, s0_re, s0_im, ab_re, ab_im, bb, cc, d_skip, w_glu, tt, passes):
    t, b, width = u_tm.shape
    n_state = ab_re.shape[-1]
    assert t % tt == 0 and b % SUBLANES == 0
    fixed = lambda bi, ti: (0, 0)
    st = lambda bi, ti: (bi, 0)
    return pl.pallas_call(
        functools.partial(_s5_kernel, tt=tt, n_state=n_state, passes=passes, lane_chunk=512),
        grid=(b // SUBLANES, t // tt),
        in_specs=[pl.BlockSpec((tt, SUBLANES, width), lambda bi, ti: (ti, bi, 0)),
                  pl.BlockSpec((SUBLANES, n_state), st), pl.BlockSpec((SUBLANES, n_state), st),
                  pl.BlockSpec((1, n_state), fixed), pl.BlockSpec((1, n_state), fixed),
                  pl.BlockSpec(bb.shape, lambda bi, ti: (0, 0, 0)), pl.BlockSpec(cc.shape, lambda bi, ti: (0, 0, 0)),
                  pl.BlockSpec((1, width), fixed), pl.BlockSpec((width, width), fixed)],
        out_specs=[pl.BlockSpec((tt, SUBLANES, width), lambda bi, ti: (ti, bi, 0)),
                   pl.BlockSpec((SUBLANES, n_state), st), pl.BlockSpec((SUBLANES, n_state), st)],
        out_shape=[jax.ShapeDtypeStruct((t, b, width), F32),
                   jax.ShapeDtypeStruct((b, n_state), F32), jax.ShapeDtypeStruct((b, n_state), F32)],
        scratch_shapes=[pltpu.VMEM((tt * SUBLANES, 2 * n_state), F32),
                        pltpu.VMEM((SUBLANES, n_state), F32), pltpu.VMEM((SUBLANES, n_state), F32)],
        compiler_params=_cparams("parallel", "arbitrary"),
        name="s5_mixer",
    )(u_tm, s0_re, s0_im, ab_re.reshape(1, n_state), ab_im.reshape(1, n_state), bb, cc,
      d_skip.reshape(1, width), w_glu)


def s5_operands(a_re, a_im, log_dt, b_re, b_im, c_re, c_im):
    g, p = a_re.shape
    gc = b_re.shape[-1]
    ab_re, ab_im, bb_re, bb_im = s5_discretize(a_re, a_im, log_dt, b_re.transpose(0, 2, 1),
                                               b_im.transpose(0, 2, 1))
    cl = LANES // gc
    assert g % cl == 0
    eye = jnp.eye(cl, dtype=F32)
    split = lambda m: m.reshape(g // cl, cl, gc, p)
    bd = lambda m: jnp.einsum('kacp,ab->kacbp', split(m), eye).reshape(g // cl, cl * gc, cl * p)
    bb = jnp.concatenate([bd(bb_re), bd(bb_im)], axis=2)
    bdc = lambda m: jnp.einsum('kacp,ab->kbpac', split(m), eye).reshape(g // cl, cl * p, cl * gc)
    cc = jnp.concatenate([bdc(c_re), -bdc(c_im)], axis=1)
    return ab_re.reshape(g * p), ab_im.reshape(g * p), bb, cc


def _softplus(x):
    return jnp.maximum(x, 0.0) + jnp.log(1.0 + jnp.exp(-jnp.abs(x)))


def _log2_one_minus_sigmoid(x):
    nx = -x
    return jnp.minimum(nx, 0.0) - jnp.log2(1.0 + jnp.exp2(jnp.minimum(x, nx)))


LOG2_E = math.log2(math.e)


def _rwkv_kernel(u_ref, shift0_ref, wkv0_ref, mu_ref, w0_ref, wup_ref, a0_ref, aup_ref, gup_ref, kk_ref,
                 ka_ref, rk_ref, lnw_ref, lnb_ref, hsum_ref, y_ref, wkv_ref, shift_ref,
                 prev_ref, s_ref, yt_ref, *, tt, chunk, width, passes, t_valid):
    ti = pl.program_id(1)
    n_heads = width // HEAD_DIM

    @pl.when(ti == 0)
    def _():
        prev_ref[...] = shift0_ref[...]
        s_ref[...] = wkv0_ref[...]

    u = u_ref[...]
    row = lax.broadcasted_iota(jnp.int32, (tt, 1), 0)
    prev = jnp.where(row == 0, prev_ref[...], pltpu.roll(u, 1, axis=0))
    xs = u + mu_ref[...] * (prev - u)
    prev_ref[...] = u[t_valid - 1:t_valid, :]
    shift_ref[...] = u[t_valid - 1:t_valid, :]

    w3 = 3 * width
    o1 = w3 + RWKV_DECAY_LORA
    o2 = o1 + RWKV_ICLR_LORA
    r = xs[:, :width]
    k = xs[:, width:2 * width]
    v = xs[:, 2 * width:w3]
    w_log = -_softplus(-(w0_ref[...] + _mm(jnp.tanh(xs[:, w3:o1]), wup_ref[...], 3))) - 0.5
    log_decay = -jnp.exp(w_log)
    a = jax.nn.sigmoid(a0_ref[...] + _mm(xs[:, o1:o2], aup_ref[...], 3))
    g = _mm(jax.nn.sigmoid(xs[:, o2:]), gup_ref[...], 3)
    hsum = hsum_ref[...]
    kk = k * kk_ref[...]
    kk = kk / jnp.maximum(jnp.sqrt(_mm(kk * kk, hsum, 2)), 1e-12)
    k2 = k * (1.0 + (a - 1.0) * ka_ref[...])
    a_vec = -kk
    b_vec = kk * a
    if t_valid < tt:
        live = row < t_valid
        log_decay = jnp.where(live, log_decay, 0.0)
        a_vec = jnp.where(live, a_vec, 0.0)
        b_vec = jnp.where(live, b_vec, 0.0)
        k2 = jnp.where(live, k2, 0.0)
    vt = v.T

    ri = lax.broadcasted_iota(jnp.int32, (chunk, chunk), 0)
    ci = lax.broadcasted_iota(jnp.int32, (chunk, chunk), 1)
    upper_strict = ri < ci
    upper_incl = ri <= ci
    lower_incl = (ri >= ci).astype(F32)
    n_double = (min(chunk, t_valid) - 1).bit_length()

    heads = range(n_heads)
    hsl = [slice(h * HEAD_DIM, (h + 1) * HEAD_DIM) for h in heads]
    eye = (ri == ci).astype(F32)

    pre = []
    for c in range(tt // chunk):
        rows = slice(c * chunk, (c + 1) * chunk)
        ld = log_decay[rows]
        cl = _mm(lower_incl, ld, 3)
        e_incl = jnp.exp(cl)
        e_inv = jnp.exp(-cl)
        rt = r[rows] * e_incl
        at = a_vec[rows] * jnp.exp(cl - ld)
        bt = b_vec[rows] * e_inv
        kt = k2[rows] * e_inv
        bk = [jnp.concatenate([bt[:, hs], kt[:, hs]], axis=0) for hs in hsl]
        ar = [jnp.concatenate([at[:, hs], rt[:, hs]], axis=0) for hs in hsl]
        gram = [_mm(bk[h], ar[h], passes, _NT) for h in heads]
        n_t = [jnp.where(upper_strict, gm[:chunk, :chunk], 0.0) for gm in gram]
        m2_t = [jnp.where(upper_strict, gm[chunk:, :chunk], 0.0) for gm in gram]
        m34_t = [jnp.concatenate([jnp.where(upper_incl, gm[:chunk, chunk:], 0.0),
                                  jnp.where(upper_incl, gm[chunk:, chunk:], 0.0)], axis=0) for gm in gram]
        vt_c = [vt[hs, rows] for hs in hsl]
        vm = [_mm(vt_c[h], m2_t[h], passes) for h in heads]
        t_inv = [eye + n for n in n_t]
        nk = n_t
        for _ in range(1, n_double):
            nk = [_mm(n, n, passes) for n in nk]
            t_inv = [t_inv[h] + _mm(t_inv[h], nk[h], passes) for h in heads]
        pre.append((rows, bk, ar, m34_t, vt_c, vm, t_inv, e_incl[chunk - 1:chunk, :]))

    state = [s_ref[h] for h in heads]
    for rows, bk, ar, m34_t, vt_c, vm, t_inv, p_end in pre:
        sa = [_mm(state[h], ar[h], passes, _NT) for h in heads]
        x = [_mm(sa[h][:, :chunk] + vm[h], t_inv[h], passes) for h in heads]
        uv = [jnp.concatenate([x[h], vt_c[h]], axis=1) for h in heads]
        for h in heads:
            yt_ref[hsl[h], rows] = sa[h][:, chunk:] + _mm(uv[h], m34_t[h], passes)
        state = [(state[h] + _mm(uv[h], bk[h], passes)) * p_end[:, hsl[h]] for h in heads]
    for h in heads:
        s_ref[h] = state[h]

    y = yt_ref[...].T
    hmean = hsum * (1.0 / HEAD_DIM)
    yc = y - _mm(y, hmean, 2)
    var = _mm(yc * yc, hmean, 2)
    yn = yc * lax.rsqrt(var + RWKV_GN_EPS) * lnw_ref[...] + lnb_ref[...]
    bonus = _mm(r * k2 * rk_ref[...], hsum, 2) * v
    y_ref[...] = (yn + bonus) * g
    wkv_ref[...] = s_ref[...]


def rwkv_mixer(u, shift0, wkv0, p, tt, chunk, passes):
    b, t_valid, n_in = u.shape
    n_heads = wkv0.shape[1]
    width = n_heads * HEAD_DIM
    if t_valid < tt:
        assert t_valid <= RWKV_MAX_CHUNK
        u = _pad_rows(u, tt)
        chunk = tt
    else:
        assert chunk <= RWKV_MAX_CHUNK
        t_valid = tt
    t = u.shape[1]
    assert t % tt == 0 and tt % chunk == 0
    fixed = lambda bi, ti: (0, 0)
    vec = lambda n: pl.BlockSpec((1, n), fixed)
    args = [p['rwkv_mu'].reshape(1, n_in), p['rwkv_w0'].reshape(1, width), p['rwkv_w_up'],
            p['rwkv_a0'].reshape(1, width), p['rwkv_a_up'], p['rwkv_g_up'], p['rwkv_k_k'].reshape(1, width),
            p['rwkv_k_a'].reshape(1, width), p['rwkv_r_k'].reshape(1, width), p['rwkv_ln_w'].reshape(1, width),
            p['rwkv_ln_b'].reshape(1, width), _head_block_matrix(width, 1.0)]
    specs = [vec(n_in), vec(width), pl.BlockSpec(p['rwkv_w_up'].shape, fixed), vec(width),
             pl.BlockSpec(p['rwkv_a_up'].shape, fixed), pl.BlockSpec(p['rwkv_g_up'].shape, fixed),
             vec(width), vec(width), vec(width), vec(width), vec(width), pl.BlockSpec((width, width), fixed)]
    y, wkv, shift = pl.pallas_call(
        functools.partial(_rwkv_kernel, tt=tt, chunk=chunk, width=width, passes=passes, t_valid=t_valid),
        grid=(b, t // tt),
        in_specs=[pl.BlockSpec((None, tt, n_in), lambda bi, ti: (bi, ti, 0)),
                  pl.BlockSpec((None, 1, n_in), lambda bi, ti: (bi, 0, 0)),
                  pl.BlockSpec((None, n_heads, HEAD_DIM, HEAD_DIM), lambda bi, ti: (bi, 0, 0, 0))] + specs,
        out_specs=[pl.BlockSpec((None, tt, width), lambda bi, ti: (bi, ti, 0)),
                   pl.BlockSpec((None, n_heads, HEAD_DIM, HEAD_DIM), lambda bi, ti: (bi, 0, 0, 0)),
                   pl.BlockSpec((None, 1, n_in), lambda bi, ti: (bi, 0, 0))],
        out_shape=[jax.ShapeDtypeStruct((b, t, width), F32),
                   jax.ShapeDtypeStruct((b, n_heads, HEAD_DIM, HEAD_DIM), F32),
                   jax.ShapeDtypeStruct((b, 1, n_in), F32)],
        scratch_shapes=[pltpu.VMEM((1, n_in), F32), pltpu.VMEM((n_heads, HEAD_DIM, HEAD_DIM), F32),
                        pltpu.VMEM((width, tt), F32)],
        compiler_params=_cparams("parallel", "arbitrary"),
        name="rwkv_mixer",
    )(u, shift0.reshape(b, 1, n_in), wkv0, *args)
    return y[:, :t_valid] if t_valid < tt else y, wkv, shift.reshape(b, n_in)


NEG_BIG = -1e30
_TN = (((0,), (0,)), ((), ()))
HEAD_PAIR = 2 * HEAD_DIM


def _proj_qkv_t_kernel(x_ref, g_ref, wt_ref, gq_ref, gk_ref, qt_ref, kt_ref, vt_ref, krow_ref, vtb_ref, kmean_ref,
                       *, width, q_scale, block):
    ti = pl.program_id(1)
    tm = x_ref.shape[0]
    n_heads = width // HEAD_DIM
    h = _rms(x_ref[...], g_ref[...]).astype(BF16)
    ut = _dg(wt_ref[...], h, _NT)

    def head_rms_t(xt, g_col):
        x3 = xt.reshape(n_heads, HEAD_DIM, tm)
        ms = jnp.mean(x3 * x3, axis=1, keepdims=True)
        return (x3 * lax.rsqrt(ms + EPS) * g_col[None, :, :]).reshape(width, tm)

    qt_ref[...] = head_rms_t(ut[:width], gq_ref[...]) * q_scale
    kt = head_rms_t(ut[width:2 * width], gk_ref[...])
    kt_ref[...] = kt
    vt = ut[2 * width:]
    vt_ref[...] = vt
    vtb_ref[...] = vt.astype(BF16)
    k_rows = kt.T
    krow_ref[...] = k_rows.astype(BF16)
    @pl.when(ti == 0)
    def _():
        kmean_ref[...] = jnp.zeros_like(kmean_ref)

    per_tile = tm // block
    blk = lax.broadcasted_iota(jnp.int32, kmean_ref.shape, 0)
    kmean = kmean_ref[...]
    for n in range(per_tile):
        mean = jnp.sum(k_rows[n * block:(n + 1) * block], axis=0, keepdims=True) * (1.0 / block)
        kmean = jnp.where(blk == ti * per_tile + n, mean, kmean)
    kmean_ref[...] = kmean


def proj_qkv_t(x, g, w_t, g_q, g_k, tm, q_scale, layer, depth, prev_kv):
    b, t, d = x.shape
    width = w_t.shape[0] // 3
    nb = t // MOBA_BLOCK
    assert t % tm == 0 and tm % MOBA_BLOCK == 0
    fixed = lambda bi, ti: (0, 0)
    tr = pl.BlockSpec((None, width, tm), lambda bi, ti: (bi, 0, ti))
    slab = pl.BlockSpec((None, None, width, tm), lambda bi, ti: (layer, bi, 0, ti))
    kernel_fn = functools.partial(_proj_qkv_t_kernel, width=width, q_scale=q_scale, block=MOBA_BLOCK)
    in_specs = [pl.BlockSpec((None, tm, d), lambda bi, ti: (bi, ti, 0)), pl.BlockSpec((1, d), fixed),
                pl.BlockSpec((3 * width, d), fixed), pl.BlockSpec((HEAD_DIM, 1), fixed),
                pl.BlockSpec((HEAD_DIM, 1), fixed)]
    args = [x, g.reshape(1, d), w_t, g_q.reshape(HEAD_DIM, 1), g_k.reshape(HEAD_DIM, 1)]
    aliases = {}
    if prev_kv is not None:
        n_in = len(args)
        args += list(prev_kv)
        in_specs += [pl.BlockSpec(memory_space=pl.ANY)] * 2
        aliases = {n_in: 1, n_in + 1: 2}
        body = kernel_fn
        kernel_fn = lambda *refs: body(*refs[:n_in], *refs[n_in + 2:])
    return pl.pallas_call(
        kernel_fn,
        grid=(b, t // tm),
        in_specs=in_specs,
        out_specs=[tr, slab, slab, pl.BlockSpec((None, tm, width), lambda bi, ti: (bi, ti, 0)), tr,
                   pl.BlockSpec((None, nb, width), lambda bi, ti: (bi, 0, 0))],
        out_shape=[jax.ShapeDtypeStruct((b, width, t), F32)] + [jax.ShapeDtypeStruct((depth, b, width, t), F32)] * 2
                  + [jax.ShapeDtypeStruct((b, t, width), BF16), jax.ShapeDtypeStruct((b, width, t), BF16),
                     jax.ShapeDtypeStruct((b, nb, width), F32)],
        input_output_aliases=aliases,
        compiler_params=_cparams("parallel", "arbitrary"),
        name="proj_qkv_t",
    )(*args)


def _head_pair_queries(qt_ref, h, tq):
    p0 = (h // 2) * HEAD_PAIR
    rows = lax.broadcasted_iota(jnp.int32, (HEAD_PAIR, tq), 0) // HEAD_DIM
    return jnp.where(rows == h % 2, qt_ref[p0:p0 + HEAD_PAIR, :], 0.0)


def _sb_prompt_kernel(qt_ref, k_ref, vt_ref, o_ref, qm_ref, acc_ref, run_ref, *, tq, width):
    qi = pl.program_id(1)
    n_heads = width // HEAD_DIM
    krow = lax.broadcasted_iota(jnp.int32, (tq, tq), 0)
    qcol = lax.broadcasted_iota(jnp.int32, (tq, tq), 1)
    from_here = jnp.where(qcol >= krow, 1.0, 0.0).astype(BF16)
    from_here = jnp.concatenate([from_here, from_here], axis=1)
    strict = krow < qcol
    for h in range(n_heads):
        qm_ref[h] = _head_pair_queries(qt_ref, h, tq).astype(BF16)
    acc_ref[...] = jnp.zeros_like(acc_ref)
    run_ref[...] = jnp.zeros_like(run_ref)

    heads = range(n_heads)
    hsl = [slice(h * HEAD_DIM, (h + 1) * HEAD_DIM) for h in heads]

    def block(j, diag):
        ks = pl.ds(pl.multiple_of(j * tq, tq), tq)
        run = run_ref[...]
        z = [_dg(k_ref[ks, (h // 2) * HEAD_PAIR:(h // 2 + 1) * HEAD_PAIR], qm_ref[h], _NN) for h in heads]
        l1m = [_log2_one_minus_sigmoid(zh) for zh in z]
        if diag:
            l1m = [jnp.where(strict, x, 0.0) for x in l1m]
        parts = [jnp.concatenate(_split_bf16(x), axis=0) for x in l1m]
        rest = [_dg(from_here, parts[h], _NN) + run[h:h + 1, :] for h in heads]
        w = [jnp.exp2(z[h] + rest[h]) for h in heads]
        if diag:
            w = [jnp.where(strict, x, 0.0) for x in w]
        pv = [_dg(vt_ref[hsl[h], ks], w[h].astype(BF16), _NN) for h in heads]
        for h in heads:
            acc_ref[hsl[h], :] += pv[h]
        run_ref[...] = jnp.concatenate([rest[h][0:1, :] for h in heads], axis=0)

    block(qi, True)

    def body(jj, carry):
        block(qi - 1 - jj, False)
        return carry

    lax.fori_loop(0, qi, body, 0)
    o_ref[...] = acc_ref[...]


def _attn_prompt_call(kernel_fn, name, qt, k_rows, vt_b, extra, extra_specs, scratch, tq):
    b, width, t = qt.shape
    n_heads = width // HEAD_DIM
    tile = pl.BlockSpec((None, width, tq), lambda bi, qi: (bi, 0, qi))
    return pl.pallas_call(
        kernel_fn,
        grid=(b, t // tq),
        in_specs=[tile, pl.BlockSpec((None, t, width), lambda bi, qi: (bi, 0, 0)),
                  pl.BlockSpec((None, width, t), lambda bi, qi: (bi, 0, 0))] + extra_specs,
        out_specs=tile,
        out_shape=jax.ShapeDtypeStruct((b, width, t), F32),
        scratch_shapes=[pltpu.VMEM((n_heads, HEAD_PAIR, tq), BF16), pltpu.VMEM((width, tq), F32)] + scratch,
        compiler_params=_cparams("parallel", "arbitrary"),
        name=name,
    )(qt, k_rows, vt_b, *extra)


def sb_prompt(qt, k_rows, vt_b, tq):
    b, width, t = qt.shape
    assert t % tq == 0
    n_heads = width // HEAD_DIM
    return _attn_prompt_call(functools.partial(_sb_prompt_kernel, tq=tq, width=width), "sb_prompt", qt, k_rows,
                             vt_b, [], [], [pltpu.VMEM((n_heads, tq), F32)], tq)


def _topk_block_mask(gate, n_valid, topk, axis):
    nb = gate.shape[axis]
    idx = lax.broadcasted_iota(jnp.int32, gate.shape, axis)
    rank = jnp.zeros(gate.shape, jnp.int32)
    for m in range(nb):
        gm = gate[:, m:m + 1] if axis == 1 else gate[m:m + 1, :]
        beats = (gm > gate) | ((gm == gate) & (m < idx))
        rank = rank + jnp.where(beats & (m < n_valid), 1, 0)
    return (idx < n_valid) & (rank < topk)


def _moba_prompt_kernel(qt_ref, k_ref, vt_ref, kmean_ref, o_ref, qm_ref, acc_ref, sel_ref, m_ref, l_ref,
                        *, tq, width):
    qi = pl.program_id(1)
    n_heads = width // HEAD_DIM
    krow = lax.broadcasted_iota(jnp.int32, (tq, tq), 0)
    qcol = lax.broadcasted_iota(jnp.int32, (tq, tq), 1)
    causal = krow <= qcol
    for h in range(n_heads):
        p0 = (h // 2) * HEAD_PAIR
        q_pair = _head_pair_queries(qt_ref, h, tq)
        qm_ref[h] = q_pair.astype(BF16)
        gate = _mm(kmean_ref[:, p0:p0 + HEAD_PAIR], q_pair, 3)
        sel_ref[h] = jnp.where(_topk_block_mask(gate, qi, MOBA_TOPK, 0), 1.0, 0.0)
    acc_ref[...] = jnp.zeros_like(acc_ref)
    l_ref[...] = jnp.zeros_like(l_ref)
    m_ref[...] = jnp.full(m_ref.shape, NEG_BIG, F32)

    heads = range(n_heads)
    hsl = [slice(h * HEAD_DIM, (h + 1) * HEAD_DIM) for h in heads]

    def block(n, diag):
        ks = pl.ds(pl.multiple_of(n * tq, tq), tq)
        m_old = m_ref[...]
        l_old = l_ref[...]
        s = [_dg(k_ref[ks, (h // 2) * HEAD_PAIR:(h // 2 + 1) * HEAD_PAIR], qm_ref[h], _NN) for h in heads]
        if diag:
            mask = [causal] * n_heads
        else:
            mask = [jnp.broadcast_to(sel_ref[h, pl.ds(n, 1), :] > 0.5, (tq, tq)) for h in heads]
        s = [jnp.where(mask[h], s[h], NEG_BIG) for h in heads]
        m_new = [jnp.maximum(m_old[h:h + 1, :], jnp.max(s[h], axis=0, keepdims=True)) for h in heads]
        p = [jnp.where(mask[h], jnp.exp(s[h] - m_new[h]), 0.0) for h in heads]
        alpha = [jnp.exp(m_old[h:h + 1, :] - m_new[h]) for h in heads]
        l_new = [alpha[h] * l_old[h:h + 1, :] + jnp.sum(p[h], axis=0, keepdims=True) for h in heads]
        pv = [_dg(vt_ref[hsl[h], ks], p[h].astype(BF16), _NN) for h in heads]
        for h in heads:
            acc_ref[hsl[h], :] = alpha[h] * acc_ref[hsl[h], :] + pv[h]
        m_ref[...] = jnp.concatenate(m_new, axis=0)
        l_ref[...] = jnp.concatenate(l_new, axis=0)

    block(qi, True)

    def body(n, carry):
        block(n, False)
        return carry

    lax.fori_loop(0, qi, body, 0)
    for h in range(n_heads):
        hs = slice(h * HEAD_DIM, (h + 1) * HEAD_DIM)
        o_ref[hs, :] = acc_ref[hs, :] / l_ref[h:h + 1, :]


def moba_prompt(qt, k_rows, vt_b, kmean):
    b, width, t = qt.shape
    tq = MOBA_BLOCK
    nb = t // tq
    assert t % tq == 0
    n_heads = width // HEAD_DIM
    return _attn_prompt_call(functools.partial(_moba_prompt_kernel, tq=tq, width=width), "moba_prompt", qt, k_rows,
                             vt_b, [kmean], [pl.BlockSpec((None, nb, width), lambda bi, qi: (bi, 0, 0))],
                             [pltpu.VMEM((n_heads, nb, tq), F32), pltpu.VMEM((n_heads, tq), F32),
                              pltpu.VMEM((n_heads, tq), F32)], tq)


def _stack_heads(q, n_heads):
    t, width = q.shape
    rows = lax.broadcasted_iota(jnp.int32, (n_heads * t, width), 0) // t
    lanes = lax.broadcasted_iota(jnp.int32, (n_heads * t, width), 1) // HEAD_DIM
    return jnp.where(rows == lanes, jnp.concatenate([q] * n_heads, axis=0), 0.0)


def _unstack_heads(acc, n_heads):
    rows_total, width = acc.shape
    t = rows_total // n_heads
    rows = lax.broadcasted_iota(jnp.int32, (rows_total, width), 0) // t
    lanes = lax.broadcasted_iota(jnp.int32, (rows_total, width), 1) // HEAD_DIM
    kept = jnp.where(rows == lanes, acc, 0.0).reshape(n_heads, t, width)
    return jnp.sum(kept, axis=0)


def _from_here_matrix(tk):
    kj = lax.broadcasted_iota(jnp.int32, (2 * tk, tk), 0) % tk
    ks = lax.broadcasted_iota(jnp.int32, (2 * tk, tk), 1)
    return jnp.where(kj >= ks, 1.0, 0.0).astype(BF16)


def _sum_from_here(x, from_here):
    return _dg(jnp.concatenate(_split_bf16(x), axis=1), from_here, _NN)


def _sb_sample_kernel(pt_ref, q_ref, knew_ref, vnew_ref, *refs, n_pages_step, t_new, n_heads):
    k_refs = refs[:n_pages_step]
    v_refs = refs[n_pages_step:2 * n_pages_step]
    o_ref, qs_ref, acc_ref, run_ref = refs[2 * n_pages_step:]
    step = pl.program_id(1)
    tk = knew_ref.shape[0]
    from_here = _from_here_matrix(tk)

    def tile(k_tile, v_tile, mask, paged):
        z = _dg(qs_ref[...], k_tile.astype(BF16), _NN if paged else _NT)
        l1m = _log2_one_minus_sigmoid(z)
        if mask is not None:
            l1m = jnp.where(mask, l1m, 0.0)
        rest = _sum_from_here(l1m, from_here) + run_ref[...]
        w = jnp.exp2(z + rest)
        if mask is not None:
            w = jnp.where(mask, w, 0.0)
        acc_ref[...] += _mm(w, v_tile, 1, _NT if paged else _NN)
        run_ref[...] = rest[:, 0:1]

    @pl.when(step == 0)
    def _():
        qs_ref[...] = _stack_heads(q_ref[...], n_heads).astype(BF16)
        acc_ref[...] = jnp.zeros_like(acc_ref)
        run_ref[...] = jnp.zeros_like(run_ref)
        rows = lax.broadcasted_iota(jnp.int32, (n_heads * t_new, tk), 0) % t_new
        cols = lax.broadcasted_iota(jnp.int32, (n_heads * t_new, tk), 1)
        tile(knew_ref[...], vnew_ref[...], cols < rows, False)

    qs = qs_ref[...]
    z = [_dg(qs, k_refs[i][...].astype(BF16), _NN) for i in range(n_pages_step)]
    cum = [_sum_from_here(_log2_one_minus_sigmoid(x), from_here) for x in z]
    run = run_ref[...]
    acc = acc_ref[...]
    for i in range(n_pages_step):
        w = jnp.exp2(z[i] + cum[i] + run)
        acc = acc + _mm(w, v_refs[i][...], 1, _NT)
        run = run + cum[i][:, 0:1]
    acc_ref[...] = acc
    run_ref[...] = run

    @pl.when(step == pl.num_programs(1) - 1)
    def _():
        o_ref[...] = _unstack_heads(acc_ref[...], n_heads)


def _page_specs(n_pages, n_pages_step, width, page_size, reverse):
    def spec(i):
        def index(bi, si, pt):
            p = si * n_pages_step + i
            return (pt[bi, n_pages - 1 - p if reverse else p], 0, 0)
        return pl.BlockSpec((None, width, page_size), index)
    return [spec(i) for i in range(n_pages_step)]


def _pad_rows(x, rows):
    return jnp.pad(x, ((0, 0), (0, rows - x.shape[1]), (0, 0)))


def sb_sample(q, k_new, v_new, k_pool, v_pool, pages, n_pages_step):
    b, t_new, width = q.shape
    n_heads = width // HEAD_DIM
    page_size = k_pool.shape[2]
    n_pages = pages.shape[1]
    assert n_pages % n_pages_step == 0 and t_new <= page_size
    new = pl.BlockSpec((None, t_new, width), lambda bi, si, pt: (bi, 0, 0))
    new_pad = pl.BlockSpec((None, page_size, width), lambda bi, si, pt: (bi, 0, 0))
    page_specs = _page_specs(n_pages, n_pages_step, width, page_size, reverse=True)
    return pl.pallas_call(
        functools.partial(_sb_sample_kernel, n_pages_step=n_pages_step, t_new=t_new, n_heads=n_heads),
        grid_spec=pltpu.PrefetchScalarGridSpec(
            num_scalar_prefetch=1, grid=(b, n_pages // n_pages_step),
            in_specs=[new, new_pad, new_pad] + page_specs + page_specs,
            out_specs=new,
            scratch_shapes=[pltpu.VMEM((n_heads * t_new, width), BF16),
                            pltpu.VMEM((n_heads * t_new, width), F32),
                            pltpu.VMEM((n_heads * t_new, 1), F32)]),
        out_shape=jax.ShapeDtypeStruct((b, t_new, width), F32),
        compiler_params=_cparams("parallel", "arbitrary"),
        name="sb_sample",
    )(pages, q, _pad_rows(k_new, page_size), _pad_rows(v_new, page_size),
      *([k_pool] * n_pages_step), *([v_pool] * n_pages_step))


def _moba_sample_kernel(pt_ref, q_ref, knew_ref, vnew_ref, *refs, n_pages_step, pages_per_block, t_new, n_heads,
                        key_steps):
    k_refs = refs[:n_pages_step]
    v_refs = refs[n_pages_step:2 * n_pages_step]
    o_ref, qs_ref, kb_ref, kmean_ref, sel_ref, acc_ref, m_ref, l_ref = refs[2 * n_pages_step:]
    step = pl.program_id(1)
    nb = kmean_ref.shape[1]
    rows_total = n_heads * t_new
    page_size = k_refs[0].shape[1]

    def tile(k_tile, v_tile, mask, paged):
        s = jnp.where(mask, _dg(qs_ref[...], k_tile.astype(BF16), _NN if paged else _NT), NEG_BIG)
        m_new = jnp.maximum(m_ref[...], jnp.max(s, axis=1, keepdims=True))
        p = jnp.where(mask, jnp.exp(s - m_new), 0.0)
        alpha = jnp.exp(m_ref[...] - m_new)
        l_ref[...] = alpha * l_ref[...] + jnp.sum(p, axis=1, keepdims=True)
        acc_ref[...] = alpha * acc_ref[...] + _mm(p, v_tile, 1, _NT if paged else _NN)
        m_ref[...] = m_new

    @pl.when(step == 0)
    def _():
        qs_ref[...] = _stack_heads(q_ref[...], n_heads).astype(BF16)
        kmean_ref[...] = jnp.zeros_like(kmean_ref)
        acc_ref[...] = jnp.zeros_like(acc_ref)
        l_ref[...] = jnp.zeros_like(l_ref)
        m_ref[...] = jnp.full(m_ref.shape, NEG_BIG, F32)
        tk = knew_ref.shape[0]
        rows = lax.broadcasted_iota(jnp.int32, (rows_total, tk), 0) % t_new
        cols = lax.broadcasted_iota(jnp.int32, (rows_total, tk), 1)
        tile(knew_ref[...], vnew_ref[...], cols <= rows, False)

    @pl.when(step < key_steps)
    def _():
        blk_lane = lax.broadcasted_iota(jnp.int32, kmean_ref.shape, 1)
        kmean = kmean_ref[...]
        for n in range(n_pages_step // pages_per_block):
            tot = None
            for i in range(pages_per_block):
                page = n * pages_per_block + i
                k_page = k_refs[page][...]
                kb_ref[step * n_pages_step + page] = k_page.astype(BF16)
                tot = k_page if tot is None else tot + k_page
            mean = jnp.sum(tot, axis=1, keepdims=True) * (1.0 / MOBA_BLOCK)
            kmean = jnp.where(blk_lane == step * (n_pages_step // pages_per_block) + n, mean, kmean)
        kmean_ref[...] = kmean

    @pl.when(step == key_steps)
    def _():
        gate = _mm(_stack_heads(q_ref[...], n_heads), kmean_ref[...], 3)
        sel_ref[...] = jnp.where(_topk_block_mask(gate, nb, MOBA_TOPK, 1), 1.0, 0.0)

    @pl.when(step >= key_steps)
    def _():
        first = (step - key_steps) * n_pages_step
        qs = qs_ref[...]
        n_keys = n_pages_step * page_size
        key_blk = (first * page_size + lax.broadcasted_iota(jnp.int32, (nb, n_keys), 1)) // MOBA_BLOCK
        expand = jnp.where(key_blk == lax.broadcasted_iota(jnp.int32, (nb, n_keys), 0), 1.0, 0.0).astype(BF16)
        picked = _dg(sel_ref[...].astype(BF16), expand, _NN) > 0.5
        mask = [picked[:, i * page_size:(i + 1) * page_size] for i in range(n_pages_step)]
        s = [jnp.where(mask[i], _dg(qs, kb_ref[first + i], _NN), NEG_BIG) for i in range(n_pages_step)]
        s_max = s[0]
        for x in s[1:]:
            s_max = jnp.maximum(s_max, x)
        m_old = m_ref[...]
        m_new = jnp.maximum(m_old, jnp.max(s_max, axis=1, keepdims=True))
        p = [jnp.where(mask[i], jnp.exp(s[i] - m_new), 0.0) for i in range(n_pages_step)]
        alpha = jnp.exp(m_old - m_new)
        p_sum = p[0]
        acc = alpha * acc_ref[...] + _mm(p[0], v_refs[0][...], 1, _NT)
        for i in range(1, n_pages_step):
            p_sum = p_sum + p[i]
            acc = acc + _mm(p[i], v_refs[i][...], 1, _NT)
        l_ref[...] = alpha * l_ref[...] + jnp.sum(p_sum, axis=1, keepdims=True)
        acc_ref[...] = acc
        m_ref[...] = m_new

    @pl.when(step == pl.num_programs(1) - 1)
    def _():
        o_ref[...] = _unstack_heads(acc_ref[...] / l_ref[...], n_heads)


def moba_sample(q, k_new, v_new, k_pool, v_pool, pages, n_pages_step):
    b, t_new, width = q.shape
    n_heads = width // HEAD_DIM
    page_size = k_pool.shape[2]
    n_pages = pages.shape[1]
    pages_per_block = MOBA_BLOCK // page_size
    nb = n_pages // pages_per_block
    key_steps = n_pages // n_pages_step
    assert n_pages % n_pages_step == 0 and n_pages_step % pages_per_block == 0 and t_new <= page_size
    assert (n_pages * page_size) % MOBA_BLOCK == 0 and t_new <= MOBA_BLOCK
    new = pl.BlockSpec((None, t_new, width), lambda bi, si, pt: (bi, 0, 0))
    new_pad = pl.BlockSpec((None, page_size, width), lambda bi, si, pt: (bi, 0, 0))

    def page_spec(i, keys):
        def index(bi, si, pt):
            s = jnp.minimum(si, key_steps - 1) if keys else jnp.maximum(si - key_steps, 0)
            return (pt[bi, s * n_pages_step + i], 0, 0)
        return pl.BlockSpec((None, width, page_size), index)

    rows_total = n_heads * t_new
    return pl.pallas_call(
        functools.partial(_moba_sample_kernel, n_pages_step=n_pages_step, pages_per_block=pages_per_block,
                          t_new=t_new, n_heads=n_heads, key_steps=key_steps),
        grid_spec=pltpu.PrefetchScalarGridSpec(
            num_scalar_prefetch=1, grid=(b, 2 * key_steps),
            in_specs=[new, new_pad, new_pad] + [page_spec(i, True) for i in range(n_pages_step)]
                     + [page_spec(i, False) for i in range(n_pages_step)],
            out_specs=new,
            scratch_shapes=[pltpu.VMEM((rows_total, width), BF16), pltpu.VMEM((n_pages, width, page_size), BF16),
                            pltpu.VMEM((width, nb), F32), pltpu.VMEM((rows_total, nb), F32),
                            pltpu.VMEM((rows_total, width), F32), pltpu.VMEM((rows_total, 1), F32),
                            pltpu.VMEM((rows_total, 1), F32)]),
        out_shape=jax.ShapeDtypeStruct((b, t_new, width), F32),
        compiler_params=_cparams("parallel", "arbitrary"),
        name="moba_sample",
    )(pages, q, _pad_rows(k_new, page_size), _pad_rows(v_new, page_size),
      *([k_pool] * n_pages_step), *([v_pool] * n_pages_step))


def _merge_kernel(x_ref, g_ref, wg_ref, y0_ref, y1_ref, y2_ref, y3_ref, wb_ref, wo_ref, o_ref, *, d_model, layouts):
    x = x_ref[...]
    h = _rms(x, g_ref[...]).astype(BF16)
    merged = None
    for n, (y_ref, layout) in enumerate(zip((y0_ref, y1_ref, y2_ref, y3_ref), layouts)):
        gate = jax.nn.sigmoid(_dg(h, wg_ref[:, n * d_model:(n + 1) * d_model], _NN))
        term = gate * _mm(y_ref[...], wb_ref[n], 1, _TN if layout == "transposed" else _NN)
        merged = term if merged is None else merged + term
    o_ref[...] = x + _mm(merged, wo_ref[...])


def merge(x, g, w_gate, ys, layouts, w_branch, w_out, tm):
    b, t, d = x.shape
    width = w_branch.shape[1]
    tm = min(tm, t)
    assert t % tm == 0
    fixed = lambda bi, ti: (0, 0)
    seq = lambda bi, ti: (bi, ti, 0)
    y_spec = {"rows": pl.BlockSpec((None, tm, width), seq),
              "time_major": pl.BlockSpec((tm, width), lambda bi, ti: (ti, bi)),
              "transposed": pl.BlockSpec((None, width, tm), lambda bi, ti: (bi, 0, ti))}
    return pl.pallas_call(
        functools.partial(_merge_kernel, d_model=d, layouts=tuple(layouts)),
        grid=(b, t // tm),
        in_specs=[pl.BlockSpec((None, tm, d), seq), pl.BlockSpec((1, d), fixed), _resident((d, N_BRANCH * d))]
                 + [y_spec[k] for k in layouts]
                 + [_resident((N_BRANCH, width, d)), _resident((d, d))],
        out_specs=pl.BlockSpec((None, tm, d), seq),
        out_shape=jax.ShapeDtypeStruct((b, t, d), F32),
        compiler_params=_cparams("parallel", "parallel"),
        name="merge",
    )(x, g.reshape(1, d), w_gate, *ys, w_branch, w_out)


def _ffn_kernel(x_ref, g_ref, wup_ref, cw_ref, cb_ref, wdn_ref, conv0_ref, o_ref, conv_ref, st_ref,
                *, nseq, tt, d_ff, col_chunk):
    ti = pl.program_id(1)

    @pl.when(ti == 0)
    def _():
        st_ref[...] = conv0_ref[...]

    d_model = x_ref.shape[-1]
    x = x_ref[...].reshape(nseq * tt, d_model)
    h = _rms(x, g_ref[...]).astype(BF16)
    t_idx = lax.broadcasted_iota(jnp.int32, (1, tt, 1), 1)

    def conv_cols(c0):
        cols = slice(c0, c0 + col_chunk)
        up = _dg(h, wup_ref[:, cols], _NN).reshape(nseq, tt, col_chunk)
        st = st_ref[:, :, cols]
        p1 = jnp.where(t_idx == 0, st[:, 1:2, :], pltpu.roll(up, 1, axis=1))
        p2 = jnp.where(t_idx == 0, st[:, 0:1, :], jnp.where(t_idx == 1, st[:, 1:2, :], pltpu.roll(up, 2, axis=1)))
        st_ref[:, :, cols] = up[:, tt - 2:, :]
        cw = cw_ref[:, cols]
        c = cb_ref[:, cols] + cw[2:3, :] * up + cw[1:2, :] * p1 + cw[0:1, :] * p2
        return c.reshape(nseq * tt, col_chunk)

    acc = x
    for c0 in range(0, d_ff, col_chunk):
        a = conv_cols(c0)
        b = conv_cols(d_ff + c0)
        acc = acc + _mm(a * jax.nn.sigmoid(a) * b, wdn_ref[c0:c0 + col_chunk, :])
    o_ref[...] = acc.reshape(nseq, tt, d_model)
    conv_ref[...] = st_ref[...]


def conv_ffn(x, g, w_up, conv_w, conv_b, w_down, conv0, nseq, tt, col_chunk):
    b, t, d = x.shape
    d_ff = w_down.shape[0]
    assert b % nseq == 0 and t % tt == 0 and d_ff % col_chunk == 0 and tt >= CONV_W - 1
    fixed = lambda bi, ti: (0, 0)
    seq = lambda bi, ti: (bi, ti, 0)
    st = lambda bi, ti: (bi, 0, 0)
    return pl.pallas_call(
        functools.partial(_ffn_kernel, nseq=nseq, tt=tt, d_ff=d_ff, col_chunk=col_chunk),
        grid=(b // nseq, t // tt),
        in_specs=[pl.BlockSpec((nseq, tt, d), seq), pl.BlockSpec((1, d), fixed),
                  _resident((d, 2 * d_ff)), pl.BlockSpec((CONV_W, 2 * d_ff), fixed),
                  pl.BlockSpec((1, 2 * d_ff), fixed), _resident((d_ff, d)),
                  pl.BlockSpec((nseq, CONV_W - 1, 2 * d_ff), st)],
        out_specs=[pl.BlockSpec((nseq, tt, d), seq), pl.BlockSpec((nseq, CONV_W - 1, 2 * d_ff), st)],
        out_shape=[jax.ShapeDtypeStruct((b, t, d), F32), jax.ShapeDtypeStruct((b, CONV_W - 1, 2 * d_ff), F32)],
        scratch_shapes=[pltpu.VMEM((nseq, CONV_W - 1, 2 * d_ff), F32)],
        compiler_params=_cparams("parallel", "arbitrary"),
        name="conv_ffn",
    )(x, g.reshape(1, d), w_up, conv_w, conv_b.reshape(1, 2 * d_ff), w_down, conv0)


MOBA_Q_SCALE = HEAD_DIM ** -0.5
SB_Q_SCALE = HEAD_DIM ** -0.5 * LOG2_E

PROMPT_TILES = dict(proj_tm=1024, qkv_tm=512, s5_tt=128, s5_passes=1, rwkv_tt=256, rwkv_chunk=64, rwkv_passes=1,
                    sb_tq=256, merge_tm=512, ffn_tt=512, ffn_cols=1408)
SAMPLE_TILES = dict(proj_tm=256, s5_tt=8, s5_passes=3, rwkv_tt=128, rwkv_chunk=64, rwkv_passes=1,
                    merge_tm=256, ffn_cols=1408, sb_pages=16, moba_pages=16)


def _layer(x, st, past, p, tiles):
    b, t, d = x.shape
    width = d // 2
    n_heads = width // HEAD_DIM
    n_rwkv = 3 * width + RWKV_DECAY_LORA + RWKV_ICLR_LORA + RWKV_GATE_LORA
    prompt = past is None
    n_state = p['s5_ab_re'].shape[0]
    s5_args = (st['s5_re'].reshape(b, n_state), st['s5_im'].reshape(b, n_state), p['s5_ab_re'], p['s5_ab_im'],
               p['s5_bb'][tiles['s5_passes']], p['s5_cc'][tiles['s5_passes']], p['s5_d'], p['s5_w_glu'],
               tiles['s5_tt'], tiles['s5_passes'])

    if prompt:
        u_s5, u_rw = proj_plain(x, p['norm1_g'], p['w_mix'], (width, n_rwkv), tiles['proj_tm'], True)
        y_s5, s5_re, s5_im = s5_mixer(u_s5.reshape(t, b, width), *s5_args)
        y_s5 = y_s5.reshape(t, b * width)
        qt, k_sb, v_sb, k_rows, vt_b, _ = proj_qkv_t(x, p['norm1_g'], p['w_sb'].T, p['sb_q_g'], p['sb_k_g'],
                                                     tiles['qkv_tm'], SB_Q_SCALE, p['layer'], p['depth'],
                                                     st['kv_sb'])
        y_sb = sb_prompt(qt, k_rows, vt_b, tiles['sb_tq'])
        qt, k_mb, v_mb, k_rows, vt_b, kmean = proj_qkv_t(x, p['norm1_g'], p['w_mb'].T, p['moba_q_g'],
                                                         p['moba_k_g'], tiles['qkv_tm'], MOBA_Q_SCALE,
                                                         p['layer'], p['depth'], st['kv_mb'])
        y_mb = moba_prompt(qt, k_rows, vt_b, kmean)
        layouts = ("time_major", "rows", "transposed", "transposed")
        x_m = x
        kv = lambda a: a
    else:
        x_m = x.reshape(1, b * t, d)
        u_s5, u_rw = proj_plain(x_m, p['norm1_g'], p['w_mix'], (width, n_rwkv), b * t, False)
        y_s5, s5_re, s5_im = s5_mixer(u_s5.reshape(b, t, width).transpose(1, 0, 2), *s5_args)
        y_s5 = y_s5.transpose(1, 0, 2).reshape(1, b * t, width)
        x2 = x.reshape(b * t, d)
        seq = lambda a: a.reshape(b, t, width)
        q_sb, k_sb, v_sb = proj_qkv(x2, p['norm1_g'], p['w_sb'], p['sb_q_g'], p['sb_k_g'], b * t,
                                    SB_Q_SCALE)
        q_mb, k_mb, v_mb = proj_qkv(x2, p['norm1_g'], p['w_mb'], p['moba_q_g'], p['moba_k_g'], b * t,
                                    MOBA_Q_SCALE)
        y_sb = sb_sample(seq(q_sb), seq(k_sb), seq(v_sb), past['sb_k'], past['sb_v'], past['pages'],
                         tiles['sb_pages'])
        y_mb = moba_sample(seq(q_mb), seq(k_mb), seq(v_mb), past['moba_k'], past['moba_v'], past['pages'],
                           tiles['moba_pages'])
        y_sb, y_mb = y_sb.reshape(1, b * t, width), y_mb.reshape(1, b * t, width)
        layouts = ("rows",) * N_BRANCH
        kv = lambda a: a.reshape(b, t, n_heads, HEAD_DIM)

    y_rw, wkv, shift = rwkv_mixer(u_rw.reshape(b, t, n_rwkv), st['shift'], st['wkv'], p, tiles['rwkv_tt'],
                                  tiles['rwkv_chunk'], tiles['rwkv_passes'])
    ys = [y_s5, y_rw.reshape(x_m.shape[0], x_m.shape[1], width), y_sb, y_mb]
    x1 = merge(x_m, p['norm1_g'], p['w_gate'], ys, layouts, p['w_branch'], p['w_out'], tiles['merge_tm'])
    nseq, tt = (1, tiles['ffn_tt']) if prompt else (b, t)
    x_out, conv = conv_ffn(x1.reshape(b, t, d), p['norm2_g'], p['ffn_w_up'], p['ffn_conv_w'], p['ffn_conv_b'],
                           p['ffn_w_down'], st['conv'], nseq, tt, tiles['ffn_cols'])
    new_st = (s5_re.reshape(st['s5_re'].shape), s5_im.reshape(st['s5_im'].shape), wkv, shift, conv,
              kv(k_sb), kv(v_sb), kv(k_mb), kv(v_mb))
    return x_out, new_st


def kernel(x_prompt, x_sample, state_s5_re, state_s5_im, state_rwkv_wkv, state_rwkv_shift, state_ffn_conv, cache_sb_k, cache_sb_v, cache_moba_k, cache_moba_v, page_table, norm1_g, w_in, s5_a_re, s5_a_im, s5_log_dt, s5_b_re, s5_b_im, s5_c_re, s5_c_im, s5_d, s5_w_glu, rwkv_mu, rwkv_w0, rwkv_w_up, rwkv_a0, rwkv_a_up, rwkv_g_up, rwkv_k_k, rwkv_k_a, rwkv_r_k, rwkv_ln_w, rwkv_ln_b, sb_q_g, sb_k_g, moba_q_g, moba_k_g, w_branch, w_out, norm2_g, ffn_w_up, ffn_conv_w, ffn_conv_b, ffn_w_down):
    depth = w_in.shape[0]
    bp = x_prompt.shape[0]
    n_pool, page_size, n_heads, head_dim = cache_sb_k.shape[1:]
    width = n_heads * head_dim
    pool = lambda c: c.transpose(0, 1, 3, 4, 2).reshape(depth * n_pool, width, page_size)
    pools = dict(sb_k=pool(cache_sb_k), sb_v=pool(cache_sb_v), moba_k=pool(cache_moba_k), moba_v=pool(cache_moba_v))
    g_s5, p_s5 = s5_a_re.shape[1:]
    n_rwkv = state_rwkv_shift.shape[-1]
    d_ff2 = state_ffn_conv.shape[-1]
    zero_st = dict(s5_re=jnp.zeros((bp, g_s5, p_s5), F32), s5_im=jnp.zeros((bp, g_s5, p_s5), F32),
                   shift=jnp.zeros((bp, n_rwkv), F32), wkv=jnp.zeros((bp, n_heads, head_dim, head_dim), F32),
                   conv=jnp.zeros((bp, CONV_W - 1, d_ff2), F32))
    yp, ys = x_prompt, x_sample
    states_p, states_s = [], []
    kv_sb = kv_mb = None
    for l in range(depth):
        ab_re, ab_im, bb, cc = s5_operands(s5_a_re[l], s5_a_im[l], s5_log_dt[l], s5_b_re[l], s5_b_im[l],
                                           s5_c_re[l], s5_c_im[l])
        width_l = w_in.shape[1] // 2
        off_sb = width_l + state_rwkv_shift.shape[-1]
        off_mb, off_gate = off_sb + 3 * width_l, off_sb + 6 * width_l
        w_piece = lambda a, b: w_in[l, :, a:b].astype(BF16)
        p = dict(layer=l, depth=depth, norm1_g=norm1_g[l], w_mix=w_piece(0, off_sb), w_sb=w_piece(off_sb, off_mb),
                 w_mb=w_piece(off_mb, off_gate), w_gate=w_piece(off_gate, w_in.shape[2]),
                 s5_ab_re=ab_re, s5_ab_im=ab_im,
                 s5_bb={1: bb.astype(BF16), 3: bb}, s5_cc={1: cc.astype(BF16), 3: cc}, s5_d=s5_d[l],
                 s5_w_glu=s5_w_glu[l].astype(BF16), rwkv_mu=rwkv_mu[l], rwkv_w0=rwkv_w0[l],
                 rwkv_w_up=rwkv_w_up[l], rwkv_a0=rwkv_a0[l], rwkv_a_up=rwkv_a_up[l], rwkv_g_up=rwkv_g_up[l],
                 rwkv_k_k=rwkv_k_k[l], rwkv_k_a=rwkv_k_a[l], rwkv_r_k=rwkv_r_k[l].reshape(-1),
                 rwkv_ln_w=rwkv_ln_w[l], rwkv_ln_b=rwkv_ln_b[l], sb_q_g=sb_q_g[l], sb_k_g=sb_k_g[l],
                 moba_q_g=moba_q_g[l], moba_k_g=moba_k_g[l], w_branch=w_branch[l].astype(BF16),
                 w_out=w_out[l].astype(BF16), norm2_g=norm2_g[l], ffn_w_up=ffn_w_up[l].astype(BF16),
                 ffn_conv_w=ffn_conv_w[l], ffn_conv_b=ffn_conv_b[l], ffn_w_down=ffn_w_down[l].astype(BF16))
        yp, st_p = _layer(yp, dict(zero_st, kv_sb=kv_sb, kv_mb=kv_mb), None, p, PROMPT_TILES)
        kv_sb, kv_mb = st_p[5:7], st_p[7:9]
        st_s = dict(s5_re=state_s5_re[l], s5_im=state_s5_im[l], shift=state_rwkv_shift[l], wkv=state_rwkv_wkv[l],
                    conv=state_ffn_conv[l])
        past = dict(pools, pages=page_table + l * n_pool)
        ys, st_s = _layer(ys, st_s, past, p, SAMPLE_TILES)
        states_p.append(st_p[:5])
        states_s.append(st_s)
    t_p = x_prompt.shape[1]
    kv_out = lambda a: a.reshape(depth, bp, n_heads, head_dim, t_p).transpose(0, 1, 4, 2, 3)
    stacked_p = [jnp.stack(z, axis=0) for z in zip(*states_p)] + [kv_out(a) for a in (*kv_sb, *kv_mb)]
    stacked_s = [jnp.stack(z, axis=0) for z in zip(*states_s)]
    out = [yp, ys]
    for a, c in zip(stacked_p, stacked_s):
        out += [a, c]
    return tuple(out)
```

```python
import functools
import math

import jax
import jax.numpy as jnp
from jax import lax
from jax.experimental import pallas as pl
from jax.experimental.pallas import tpu as pltpu

F32 = jnp.float32
BF16 = jnp.bfloat16

HEAD_DIM = 64
S5_GROUP = 16
S5_STATE = 64
RWKV_DECAY_LORA = 64
RWKV_ICLR_LORA = 64
RWKV_GATE_LORA = 128
N_BRANCH = 4
MOBA_BLOCK = 256
MOBA_TOPK = 3
CONV_W = 3
EPS = 1e-6
RWKV_GN_EPS = 64e-5
RWKV_MAX_CHUNK = 128

V7X_VMEM_LIMIT_BYTES = 56 * 1024 * 1024
SUBLANES = 8
LANES = 128


def _resident(shape):
    zeros = (0,) * len(shape)
    return pl.BlockSpec(shape, lambda *_: zeros, pipeline_mode=pl.Buffered(1))


def _cparams(*sem):
    return pltpu.CompilerParams(dimension_semantics=sem, vmem_limit_bytes=V7X_VMEM_LIMIT_BYTES)


def _split_bf16(x):
    hi = x.astype(BF16)
    lo = (x - hi.astype(F32)).astype(BF16)
    return hi, lo


_NN = (((1,), (0,)), ((), ()))
_NT = (((1,), (1,)), ((), ()))


def _dg(a, b, dims):
    return lax.dot_general(a, b, dims, preferred_element_type=F32)


def _mm(a, b, passes=1, dims=_NN):
    if passes == 1:
        return _dg(a.astype(BF16), b.astype(BF16), dims)
    ah, al = _split_bf16(a)
    if b.dtype == BF16:
        return _dg(ah, b, dims) + _dg(al, b, dims)
    bh, bl = _split_bf16(b)
    if passes == 2:
        return _dg(ah, bh, dims) + _dg(al, bh, dims)
    return _dg(ah, bh, dims) + (_dg(al, bh, dims) + _dg(ah, bl, dims))


def _head_block_matrix(width, value):
    r = jnp.arange(width) // HEAD_DIM
    return jnp.where(r[:, None] == r[None, :], value, 0.0).astype(BF16)


def _rms(x, g):
    return x * lax.rsqrt(jnp.mean(x * x, axis=-1, keepdims=True) + EPS) * g


def _proj_plain_kernel(x_ref, g_ref, w_ref, *out_refs, widths):
    h = _rms(x_ref[...], g_ref[...]).astype(BF16)
    u = _dg(h, w_ref[...], _NN)
    off = 0
    for o_ref, n in zip(out_refs, widths):
        o_ref[...] = u[:, off:off + n]
        off += n


def proj_plain(x, g, w, widths, tm, first_time_major):
    b, t, d = x.shape
    n = w.shape[1]
    assert sum(widths) == n and t % tm == 0
    fixed = lambda bi, ti: (0, 0)
    seq = lambda bi, ti: (bi, ti, 0)
    out_specs = [pl.BlockSpec((None, tm, k), seq) for k in widths]
    out_shape = [jax.ShapeDtypeStruct((b, t, k), F32) for k in widths]
    if first_time_major:
        out_specs[0] = pl.BlockSpec((tm, widths[0]), lambda bi, ti: (ti, bi))
        out_shape[0] = jax.ShapeDtypeStruct((t, b * widths[0]), F32)
    return pl.pallas_call(
        functools.partial(_proj_plain_kernel, widths=widths),
        grid=(b, t // tm),
        in_specs=[pl.BlockSpec((None, tm, d), seq), pl.BlockSpec((1, d), fixed), _resident((d, n))],
        out_specs=out_specs,
        out_shape=out_shape,
        compiler_params=_cparams("parallel", "parallel"),
        name="proj_plain",
    )(x, g.reshape(1, d), w)


def _head_rms(x, hm, g):
    ms = _mm(x * x, hm, passes=2)
    return x * lax.rsqrt(ms + EPS) * g


def _proj_qkv_kernel(x_ref, g_ref, w_ref, hm_ref, gq_ref, gk_ref, q_ref, k_ref, v_ref, *, width, q_scale):
    h = _rms(x_ref[...], g_ref[...]).astype(BF16)
    u = _dg(h, w_ref[...], _NN)
    hm = hm_ref[...]
    q_ref[...] = _head_rms(u[:, :width], hm, gq_ref[...]) * q_scale
    k_ref[...] = _head_rms(u[:, width:2 * width], hm, gk_ref[...])
    v_ref[...] = u[:, 2 * width:]


def proj_qkv(x2d, g, w, g_q, g_k, tm, q_scale):
    m, d = x2d.shape
    width = w.shape[1] // 3
    n_heads = width // HEAD_DIM
    hm = _head_block_matrix(width, 1.0 / HEAD_DIM)
    row = lambda i: (i, 0)
    fixed = lambda i: (0, 0)
    return pl.pallas_call(
        functools.partial(_proj_qkv_kernel, width=width, q_scale=q_scale),
        grid=(m // tm,),
        in_specs=[pl.BlockSpec((tm, d), row), pl.BlockSpec((1, d), fixed),
                  pl.BlockSpec((d, 3 * width), fixed), pl.BlockSpec((width, width), fixed),
                  pl.BlockSpec((1, width), fixed), pl.BlockSpec((1, width), fixed)],
        out_specs=[pl.BlockSpec((tm, width), row)] * 3,
        out_shape=[jax.ShapeDtypeStruct((m, width), F32)] * 3,
        compiler_params=_cparams("parallel"),
        name="proj_qkv",
    )(x2d, g.reshape(1, d), w, hm, jnp.tile(g_q, n_heads).reshape(1, width),
      jnp.tile(g_k, n_heads).reshape(1, width))


def _s5_disc_kernel(are_ref, aim_ref, ldt_ref, bre_ref, bim_ref, abre_ref, abim_ref, bbre_ref, bbim_ref):
    lr = are_ref[...]
    li = aim_ref[...]
    dt = jnp.exp(ldt_ref[...])
    mag = jnp.exp(lr * dt)
    ab_re = mag * jnp.cos(li * dt)
    ab_im = mag * jnp.sin(li * dt)
    den = lr * lr + li * li
    zr = ((ab_re - 1.0) * lr + ab_im * li) / den
    zi = (ab_im * lr - (ab_re - 1.0) * li) / den
    abre_ref[...] = ab_re
    abim_ref[...] = ab_im
    br = bre_ref[...]
    bi = bim_ref[...]
    bbre_ref[...] = zr[:, None, :] * br - zi[:, None, :] * bi
    bbim_ref[...] = zr[:, None, :] * bi + zi[:, None, :] * br


def s5_discretize(a_re, a_im, log_dt, b_re, b_im):
    g, p = a_re.shape
    gc = b_re.shape[1]
    return pl.pallas_call(
        _s5_disc_kernel,
        out_shape=[jax.ShapeDtypeStruct((g, p), F32)] * 2 + [jax.ShapeDtypeStruct((g, gc, p), F32)] * 2,
        name="s5_discretize",
    )(a_re, a_im, log_dt.reshape(g, 1), b_re, b_im)


def _gelu_tanh(x):
    c = math.sqrt(2.0 / math.pi)
    return 0.5 * x * (1.0 + jnp.tanh(c * (x + 0.044715 * (x * x * x))))


def _s5_kernel(u_ref, s0re_ref, s0im_ref, abre_ref, abim_ref, bb_ref, cc_ref, d_ref, wglu_ref,
               y_ref, sre_ref, sim_ref, xs_ref, cre_ref, cim_ref, *, tt, n_state, passes, lane_chunk):
    ti = pl.program_id(1)

    @pl.when(ti == 0)
    def _():
        cre_ref[...] = s0re_ref[...]
        cim_ref[...] = s0im_ref[...]

    width = u_ref.shape[-1]
    n_cl, c_in, c_state = bb_ref.shape
    half = c_state // 2
    u = u_ref[...].reshape(tt * SUBLANES, width)
    for c in range(n_cl):
        xc = _mm(u[:, c * c_in:(c + 1) * c_in], bb_ref[c], passes)
        xs_ref[:, c * half:(c + 1) * half] = xc[:, :half]
        xs_ref[:, n_state + c * half:n_state + (c + 1) * half] = xc[:, half:]

    for c0 in range(0, n_state, lane_chunk):
        re_sl = pl.ds(c0, lane_chunk)
        im_sl = pl.ds(n_state + c0, lane_chunk)
        a_re = jnp.broadcast_to(abre_ref[:, re_sl], (SUBLANES, lane_chunk))
        a_im = jnp.broadcast_to(abim_ref[:, re_sl], (SUBLANES, lane_chunk))

        def step(t, carry):
            s_re, s_im = carry
            rows = pl.ds(pl.multiple_of(t * SUBLANES, SUBLANES), SUBLANES)
            n_re = a_re * s_re - a_im * s_im + xs_ref[rows, re_sl]
            n_im = a_re * s_im + a_im * s_re + xs_ref[rows, im_sl]
            xs_ref[rows, re_sl] = n_re
            xs_ref[rows, im_sl] = n_im
            return n_re, n_im

        s_re, s_im = lax.fori_loop(0, tt, step, (cre_ref[:, re_sl], cim_ref[:, re_sl]), unroll=8)
        cre_ref[:, re_sl] = s_re
        cim_ref[:, re_sl] = s_im

    y = jnp.concatenate(
        [_mm(xs_ref[:, c * half:(c + 1) * half], cc_ref[c, :half, :], passes)
         + _mm(xs_ref[:, n_state + c * half:n_state + (c + 1) * half], cc_ref[c, half:, :], passes)
         for c in range(n_cl)], axis=1) + d_ref[...] * u
    y = _gelu_tanh(y)
    y = y * jax.nn.sigmoid(_mm(y, wglu_ref[...]))
    y_ref[...] = y.reshape(tt, SUBLANES, width)
    sre_ref[...] = cre_ref[...]
    sim_ref[...] = cim_ref[...]


def s5_mixer(u_tm, s0_re, s0_im, ab_re, ab_im, bb, cc, d_skip, w_glu, tt, passes):
    t, b, width = u_tm.shape
    n_state = ab_re.shape[-1]
    assert t % tt == 0 and b % SUBLANES == 0
    fixed = lambda bi, ti: (0, 0)
    st = lambda bi, ti: (bi, 0)
    return pl.pallas_call(
        functools.partial(_s5_kernel, tt=tt, n_state=n_state, passes=passes, lane_chunk=512),
        grid=(b // SUBLANES, t // tt),
        in_specs=[pl.BlockSpec((tt, SUBLANES, width), lambda bi, ti: (ti, bi, 0)),
                  pl.BlockSpec((SUBLANES, n_state), st), pl.BlockSpec((SUBLANES, n_state), st),
                  pl.BlockSpec((1, n_state), fixed), pl.BlockSpec((1, n_state), fixed),
                  pl.BlockSpec(bb.shape, lambda bi, ti: (0, 0, 0)), pl.BlockSpec(cc.shape, lambda bi, ti: (0, 0, 0)),
                  pl.BlockSpec((1, width), fixed), pl.BlockSpec((width, width), fixed)],
        out_specs=[pl.BlockSpec((tt, SUBLANES, width), lambda bi, ti: (ti, bi, 0)),
                   pl.BlockSpec((SUBLANES, n_state), st), pl.BlockSpec((SUBLANES, n_state), st)],
        out_shape=[jax.ShapeDtypeStruct((t, b, width), F32),
                   jax.ShapeDtypeStruct((b, n_state), F32), jax.ShapeDtypeStruct((b, n_state), F32)],
        scratch_shapes=[pltpu.VMEM((tt * SUBLANES, 2 * n_state), F32),
                        pltpu.VMEM((SUBLANES, n_state), F32), pltpu.VMEM((SUBLANES, n_state), F32)],
        compiler_params=_cparams("parallel", "arbitrary"),
        name="s5_mixer",
    )(u_tm, s0_re, s0_im, ab_re.reshape(1, n_state), ab_im.reshape(1, n_state), bb, cc,
      d_skip.reshape(1, width), w_glu)


def s5_operands(a_re, a_im, log_dt, b_re, b_im, c_re, c_im):
    g, p = a_re.shape
    gc = b_re.shape[-1]
    ab_re, ab_im, bb_re, bb_im = s5_discretize(a_re, a_im, log_dt, b_re.transpose(0, 2, 1),
                                               b_im.transpose(0, 2, 1))
    cl = LANES // gc
    assert g % cl == 0
    eye = jnp.eye(cl, dtype=F32)
    split = lambda m: m.reshape(g // cl, cl, gc, p)
    bd = lambda m: jnp.einsum('kacp,ab->kacbp', split(m), eye).reshape(g // cl, cl * gc, cl * p)
    bb = jnp.concatenate([bd(bb_re), bd(bb_im)], axis=2)
    bdc = lambda m: jnp.einsum('kacp,ab->kbpac', split(m), eye).reshape(g // cl, cl * p, cl * gc)
    cc = jnp.concatenate([bdc(c_re), -bdc(c_im)], axis=1)
    return ab_re.reshape(g * p), ab_im.reshape(g * p), bb, cc


def _softplus(x):
    return jnp.maximum(x, 0.0) + jnp.log(1.0 + jnp.exp(-jnp.abs(x)))


def _log2_one_minus_sigmoid(x):
    nx = -x
    return jnp.minimum(nx, 0.0) - jnp.log2(1.0 + jnp.exp2(jnp.minimum(x, nx)))


LOG2_E = math.log2(math.e)


def _rwkv_kernel(u_ref, shift0_ref, wkv0_ref, mu_ref, w0_ref, wup_ref, a0_ref, aup_ref, gup_ref, kk_ref,
                 ka_ref, rk_ref, lnw_ref, lnb_ref, hsum_ref, y_ref, wkv_ref, shift_ref,
                 prev_ref, s_ref, yt_ref, *, tt, chunk, width, passes, t_valid):
    ti = pl.program_id(1)
    n_heads = width // HEAD_DIM

    @pl.when(ti == 0)
    def _():
        prev_ref[...] = shift0_ref[...]
        s_ref[...] = wkv0_ref[...]

    u = u_ref[...]
    row = lax.broadcasted_iota(jnp.int32, (tt, 1), 0)
    prev = jnp.where(row == 0, prev_ref[...], pltpu.roll(u, 1, axis=0))
    xs = u + mu_ref[...] * (prev - u)
    prev_ref[...] = u[t_valid - 1:t_valid, :]
    shift_ref[...] = u[t_valid - 1:t_valid, :]

    w3 = 3 * width
    o1 = w3 + RWKV_DECAY_LORA
    o2 = o1 + RWKV_ICLR_LORA
    r = xs[:, :width]
    k = xs[:, width:2 * width]
    v = xs[:, 2 * width:w3]
    w_log = -_softplus(-(w0_ref[...] + _mm(jnp.tanh(xs[:, w3:o1]), wup_ref[...], 3))) - 0.5
    log_decay = -jnp.exp(w_log)
    a = jax.nn.sigmoid(a0_ref[...] + _mm(xs[:, o1:o2], aup_ref[...], 3))
    g = _mm(jax.nn.sigmoid(xs[:, o2:]), gup_ref[...], 3)
    hsum = hsum_ref[...]
    kk = k * kk_ref[...]
    kk = kk / jnp.maximum(jnp.sqrt(_mm(kk * kk, hsum, 2)), 1e-12)
    k2 = k * (1.0 + (a - 1.0) * ka_ref[...])
    a_vec = -kk
    b_vec = kk * a
    if t_valid < tt:
        live = row < t_valid
        log_decay = jnp.where(live, log_decay, 0.0)
        a_vec = jnp.where(live, a_vec, 0.0)
        b_vec = jnp.where(live, b_vec, 0.0)
        k2 = jnp.where(live, k2, 0.0)
    vt = v.T

    ri = lax.broadcasted_iota(jnp.int32, (chunk, chunk), 0)
    ci = lax.broadcasted_iota(jnp.int32, (chunk, chunk), 1)
    upper_strict = ri < ci
    upper_incl = ri <= ci
    lower_incl = (ri >= ci).astype(F32)
    n_double = (min(chunk, t_valid) - 1).bit_length()

    heads = range(n_heads)
    hsl = [slice(h * HEAD_DIM, (h + 1) * HEAD_DIM) for h in heads]
    eye = (ri == ci).astype(F32)

    pre = []
    for c in range(tt // chunk):
        rows = slice(c * chunk, (c + 1) * chunk)
        ld = log_decay[rows]
        cl = _mm(lower_incl, ld, 3)
        e_incl = jnp.exp(cl)
        e_inv = jnp.exp(-cl)
        rt = r[rows] * e_incl
        at = a_vec[rows] * jnp.exp(cl - ld)
        bt = b_vec[rows] * e_inv
        kt = k2[rows] * e_inv
        bk = [jnp.concatenate([bt[:, hs], kt[:, hs]], axis=0) for hs in hsl]
        ar = [jnp.concatenate([at[:, hs], rt[:, hs]], axis=0) for hs in hsl]
        gram = [_mm(bk[h], ar[h], passes, _NT) for h in heads]
        n_t = [jnp.where(upper_strict, gm[:chunk, :chunk], 0.0) for gm in gram]
        m2_t = [jnp.where(upper_strict, gm[chunk:, :chunk], 0.0) for gm in gram]
        m34_t = [jnp.concatenate([jnp.where(upper_incl, gm[:chunk, chunk:], 0.0),
                                  jnp.where(upper_incl, gm[chunk:, chunk:], 0.0)], axis=0) for gm in gram]
        vt_c = [vt[hs, rows] for hs in hsl]
        vm = [_mm(vt_c[h], m2_t[h], passes) for h in heads]
        t_inv = [eye + n for n in n_t]
        nk = n_t
        for _ in range(1, n_double):
            nk = [_mm(n, n, passes) for n in nk]
            t_inv = [t_inv[h] + _mm(t_inv[h], nk[h], passes) for h in heads]
        pre.append((rows, bk, ar, m34_t, vt_c, vm, t_inv, e_incl[chunk - 1:chunk, :]))

    state = [s_ref[h] for h in heads]
    for rows, bk, ar, m34_t, vt_c, vm, t_inv, p_end in pre:
        sa = [_mm(state[h], ar[h], passes, _NT) for h in heads]
        x = [_mm(sa[h][:, :chunk] + vm[h], t_inv[h], passes) for h in heads]
        uv = [jnp.concatenate([x[h], vt_c[h]], axis=1) for h in heads]
        for h in heads:
            yt_ref[hsl[h], rows] = sa[h][:, chunk:] + _mm(uv[h], m34_t[h], passes)
        state = [(state[h] + _mm(uv[h], bk[h], passes)) * p_end[:, hsl[h]] for h in heads]
    for h in heads:
        s_ref[h] = state[h]

    y = yt_ref[...].T
    hmean = hsum * (1.0 / HEAD_DIM)
    yc = y - _mm(y, hmean, 2)
    var = _mm(yc * yc, hmean, 2)
    yn = yc * lax.rsqrt(var + RWKV_GN_EPS) * lnw_ref[...] + lnb_ref[...]
    bonus = _mm(r * k2 * rk_ref[...], hsum, 2) * v
    y_ref[...] = (yn + bonus) * g
    wkv_ref[...] = s_ref[...]


def rwkv_mixer(u, shift0, wkv0, p, tt, chunk, passes):
    b, t_valid, n_in = u.shape
    n_heads = wkv0.shape[1]
    width = n_heads * HEAD_DIM
    if t_valid < tt:
        assert t_valid <= RWKV_MAX_CHUNK
        u = _pad_rows(u, tt)
        chunk = tt
    else:
        assert chunk <= RWKV_MAX_CHUNK
        t_valid = tt
    t = u.shape[1]
    assert t % tt == 0 and tt % chunk == 0
    fixed = lambda bi, ti: (0, 0)
    vec = lambda n: pl.BlockSpec((1, n), fixed)
    args = [p['rwkv_mu'].reshape(1, n_in), p['rwkv_w0'].reshape(1, width), p['rwkv_w_up'],
            p['rwkv_a0'].reshape(1, width), p['rwkv_a_up'], p['rwkv_g_up'], p['rwkv_k_k'].reshape(1, width),
            p['rwkv_k_a'].reshape(1, width), p['rwkv_r_k'].reshape(1, width), p['rwkv_ln_w'].reshape(1, width),
            p['rwkv_ln_b'].reshape(1, width), _head_block_matrix(width, 1.0)]
    specs = [vec(n_in), vec(width), pl.BlockSpec(p['rwkv_w_up'].shape, fixed), vec(width),
             pl.BlockSpec(p['rwkv_a_up'].shape, fixed), pl.BlockSpec(p['rwkv_g_up'].shape, fixed),
             vec(width), vec(width), vec(width), vec(width), vec(width), pl.BlockSpec((width, width), fixed)]
    y, wkv, shift = pl.pallas_call(
        functools.partial(_rwkv_kernel, tt=tt, chunk=chunk, width=width, passes=passes, t_valid=t_valid),
        grid=(b, t // tt),
        in_specs=[pl.BlockSpec((None, tt, n_in), lambda bi, ti: (bi, ti, 0)),
                  pl.BlockSpec((None, 1, n_in), lambda bi, ti: (bi, 0, 0)),
                  pl.BlockSpec((None, n_heads, HEAD_DIM, HEAD_DIM), lambda bi, ti: (bi, 0, 0, 0))] + specs,
        out_specs=[pl.BlockSpec((None, tt, width), lambda bi, ti: (bi, ti, 0)),
                   pl.BlockSpec((None, n_heads, HEAD_DIM, HEAD_DIM), lambda bi, ti: (bi, 0, 0, 0)),
                   pl.BlockSpec((None, 1, n_in), lambda bi, ti: (bi, 0, 0))],
        out_shape=[jax.ShapeDtypeStruct((b, t, width), F32),
                   jax.ShapeDtypeStruct((b, n_heads, HEAD_DIM, HEAD_DIM), F32),
                   jax.ShapeDtypeStruct((b, 1, n_in), F32)],
        scratch_shapes=[pltpu.VMEM((1, n_in), F32), pltpu.VMEM((n_heads, HEAD_DIM, HEAD_DIM), F32),
                        pltpu.VMEM((width, tt), F32)],
        compiler_params=_cparams("parallel", "arbitrary"),
        name="rwkv_mixer",
    )(u, shift0.reshape(b, 1, n_in), wkv0, *args)
    return y[:, :t_valid] if t_valid < tt else y, wkv, shift.reshape(b, n_in)


NEG_BIG = -1e30
_TN = (((0,), (0,)), ((), ()))
HEAD_PAIR = 2 * HEAD_DIM


def _proj_qkv_t_kernel(x_ref, g_ref, wt_ref, gq_ref, gk_ref, qt_ref, kt_ref, vt_ref, krow_ref, vtb_ref, kmean_ref,
                       *, width, q_scale, block):
    ti = pl.program_id(1)
    tm = x_ref.shape[0]
    n_heads = width // HEAD_DIM
    h = _rms(x_ref[...], g_ref[...]).astype(BF16)
    ut = _dg(wt_ref[...], h, _NT)

    def head_rms_t(xt, g_col):
        x3 = xt.reshape(n_heads, HEAD_DIM, tm)
        ms = jnp.mean(x3 * x3, axis=1, keepdims=True)
        return (x3 * lax.rsqrt(ms + EPS) * g_col[None, :, :]).reshape(width, tm)

    qt_ref[...] = head_rms_t(ut[:width], gq_ref[...]) * q_scale
    kt = head_rms_t(ut[width:2 * width], gk_ref[...])
    kt_ref[...] = kt
    vt = ut[2 * width:]
    vt_ref[...] = vt
    vtb_ref[...] = vt.astype(BF16)
    k_rows = kt.T
    krow_ref[...] = k_rows.astype(BF16)
    @pl.when(ti == 0)
    def _():
        kmean_ref[...] = jnp.zeros_like(kmean_ref)

    per_tile = tm // block
    blk = lax.broadcasted_iota(jnp.int32, kmean_ref.shape, 0)
    kmean = kmean_ref[...]
    for n in range(per_tile):
        mean = jnp.sum(k_rows[n * block:(n + 1) * block], axis=0, keepdims=True) * (1.0 / block)
        kmean = jnp.where(blk == ti * per_tile + n, mean, kmean)
    kmean_ref[...] = kmean


def proj_qkv_t(x, g, w_t, g_q, g_k, tm, q_scale, layer, depth, prev_kv):
    b, t, d = x.shape
    width = w_t.shape[0] // 3
    nb = t // MOBA_BLOCK
    assert t % tm == 0 and tm % MOBA_BLOCK == 0
    fixed = lambda bi, ti: (0, 0)
    tr = pl.BlockSpec((None, width, tm), lambda bi, ti: (bi, 0, ti))
    slab = pl.BlockSpec((None, None, width, tm), lambda bi, ti: (layer, bi, 0, ti))
    kernel_fn = functools.partial(_proj_qkv_t_kernel, width=width, q_scale=q_scale, block=MOBA_BLOCK)
    in_specs = [pl.BlockSpec((None, tm, d), lambda bi, ti: (bi, ti, 0)), pl.BlockSpec((1, d), fixed),
                pl.BlockSpec((3 * width, d), fixed), pl.BlockSpec((HEAD_DIM, 1), fixed),
                pl.BlockSpec((HEAD_DIM, 1), fixed)]
    args = [x, g.reshape(1, d), w_t, g_q.reshape(HEAD_DIM, 1), g_k.reshape(HEAD_DIM, 1)]
    aliases = {}
    if prev_kv is not None:
        n_in = len(args)
        args += list(prev_kv)
        in_specs += [pl.BlockSpec(memory_space=pl.ANY)] * 2
        aliases = {n_in: 1, n_in + 1: 2}
        body = kernel_fn
        kernel_fn = lambda *refs: body(*refs[:n_in], *refs[n_in + 2:])
    return pl.pallas_call(
        kernel_fn,
        grid=(b, t // tm),
        in_specs=in_specs,
        out_specs=[tr, slab, slab, pl.BlockSpec((None, tm, width), lambda bi, ti: (bi, ti, 0)), tr,
                   pl.BlockSpec((None, nb, width), lambda bi, ti: (bi, 0, 0))],
        out_shape=[jax.ShapeDtypeStruct((b, width, t), F32)] + [jax.ShapeDtypeStruct((depth, b, width, t), F32)] * 2
                  + [jax.ShapeDtypeStruct((b, t, width), BF16), jax.ShapeDtypeStruct((b, width, t), BF16),
                     jax.ShapeDtypeStruct((b, nb, width), F32)],
        input_output_aliases=aliases,
        compiler_params=_cparams("parallel", "arbitrary"),
        name="proj_qkv_t",
    )(*args)


def _head_pair_queries(qt_ref, h, tq):
    p0 = (h // 2) * HEAD_PAIR
    rows = lax.broadcasted_iota(jnp.int32, (HEAD_PAIR, tq), 0) // HEAD_DIM
    return jnp.where(rows == h % 2, qt_ref[p0:p0 + HEAD_PAIR, :], 0.0)


def _sb_prompt_kernel(qt_ref, k_ref, vt_ref, o_ref, qm_ref, acc_ref, run_ref, *, tq, width):
    qi = pl.program_id(1)
    n_heads = width // HEAD_DIM
    krow = lax.broadcasted_iota(jnp.int32, (tq, tq), 0)
    qcol = lax.broadcasted_iota(jnp.int32, (tq, tq), 1)
    from_here = jnp.where(qcol >= krow, 1.0, 0.0).astype(BF16)
    from_here = jnp.concatenate([from_here, from_here], axis=1)
    strict = krow < qcol
    for h in range(n_heads):
        qm_ref[h] = _head_pair_queries(qt_ref, h, tq).astype(BF16)
    acc_ref[...] = jnp.zeros_like(acc_ref)
    run_ref[...] = jnp.zeros_like(run_ref)

    heads = range(n_heads)
    hsl = [slice(h * HEAD_DIM, (h + 1) * HEAD_DIM) for h in heads]

    def block(j, diag):
        ks = pl.ds(pl.multiple_of(j * tq, tq), tq)
        run = run_ref[...]
        z = [_dg(k_ref[ks, (h // 2) * HEAD_PAIR:(h // 2 + 1) * HEAD_PAIR], qm_ref[h], _NN) for h in heads]
        l1m = [_log2_one_minus_sigmoid(zh) for zh in z]
        if diag:
            l1m = [jnp.where(strict, x, 0.0) for x in l1m]
        parts = [jnp.concatenate(_split_bf16(x), axis=0) for x in l1m]
        rest = [_dg(from_here, parts[h], _NN) + run[h:h + 1, :] for h in heads]
        w = [jnp.exp2(z[h] + rest[h]) for h in heads]
        if diag:
            w = [jnp.where(strict, x, 0.0) for x in w]
        pv = [_dg(vt_ref[hsl[h], ks], w[h].astype(BF16), _NN) for h in heads]
        for h in heads:
            acc_ref[hsl[h], :] += pv[h]
        run_ref[...] = jnp.concatenate([rest[h][0:1, :] for h in heads], axis=0)

    block(qi, True)

    def body(jj, carry):
        block(qi - 1 - jj, False)
        return carry

    lax.fori_loop(0, qi, body, 0)
    o_ref[...] = acc_ref[...]


def _attn_prompt_call(kernel_fn, name, qt, k_rows, vt_b, extra, extra_specs, scratch, tq):
    b, width, t = qt.shape
    n_heads = width // HEAD_DIM
    tile = pl.BlockSpec((None, width, tq), lambda bi, qi: (bi, 0, qi))
    return pl.pallas_call(
        kernel_fn,
        grid=(b, t // tq),
        in_specs=[tile, pl.BlockSpec((None, t, width), lambda bi, qi: (bi, 0, 0)),
                  pl.BlockSpec((None, width, t), lambda bi, qi: (bi, 0, 0))] + extra_specs,
        out_specs=tile,
        out_shape=jax.ShapeDtypeStruct((b, width, t), F32),
        scratch_shapes=[pltpu.VMEM((n_heads, HEAD_PAIR, tq), BF16), pltpu.VMEM((width, tq), F32)] + scratch,
        compiler_params=_cparams("parallel", "arbitrary"),
        name=name,
    )(qt, k_rows, vt_b, *extra)


def sb_prompt(qt, k_rows, vt_b, tq):
    b, width, t = qt.shape
    assert t % tq == 0
    n_heads = width // HEAD_DIM
    return _attn_prompt_call(functools.partial(_sb_prompt_kernel, tq=tq, width=width), "sb_prompt", qt, k_rows,
                             vt_b, [], [], [pltpu.VMEM((n_heads, tq), F32)], tq)


def _topk_block_mask(gate, n_valid, topk, axis):
    nb = gate.shape[axis]
    idx = lax.broadcasted_iota(jnp.int32, gate.shape, axis)
    rank = jnp.zeros(gate.shape, jnp.int32)
    for m in range(nb):
        gm = gate[:, m:m + 1] if axis == 1 else gate[m:m + 1, :]
        beats = (gm > gate) | ((gm == gate) & (m < idx))
        rank = rank + jnp.where(beats & (m < n_valid), 1, 0)
    return (idx < n_valid) & (rank < topk)


def _moba_prompt_kernel(qt_ref, k_ref, vt_ref, kmean_ref, o_ref, qm_ref, acc_ref, sel_ref, m_ref, l_ref,
                        *, tq, width):
    qi = pl.program_id(1)
    n_heads = width // HEAD_DIM
    krow = lax.broadcasted_iota(jnp.int32, (tq, tq), 0)
    qcol = lax.broadcasted_iota(jnp.int32, (tq, tq), 1)
    causal = krow <= qcol
    for h in range(n_heads):
        p0 = (h // 2) * HEAD_PAIR
        q_pair = _head_pair_queries(qt_ref, h, tq)
        qm_ref[h] = q_pair.astype(BF16)
        gate = _mm(kmean_ref[:, p0:p0 + HEAD_PAIR], q_pair, 3)
        sel_ref[h] = jnp.where(_topk_block_mask(gate, qi, MOBA_TOPK, 0), 0.0, NEG_BIG)
    acc_ref[...] = jnp.zeros_like(acc_ref)
    l_ref[...] = jnp.zeros_like(l_ref)
    m_ref[...] = jnp.full(m_ref.shape, NEG_BIG, F32)

    heads = range(n_heads)
    hsl = [slice(h * HEAD_DIM, (h + 1) * HEAD_DIM) for h in heads]

    def block(n, diag):
        ks = pl.ds(pl.multiple_of(n * tq, tq), tq)
        m_old = m_ref[...]
        l_old = l_ref[...]
        s = [_dg(k_ref[ks, (h // 2) * HEAD_PAIR:(h // 2 + 1) * HEAD_PAIR], qm_ref[h], _NN) for h in heads]
        if diag:
            s = [jnp.where(causal, x, NEG_BIG) for x in s]
        else:
            s = [s[h] + sel_ref[h, pl.ds(n, 1), :] for h in heads]
        m_new = [jnp.maximum(m_old[h:h + 1, :], jnp.max(s[h], axis=0, keepdims=True)) for h in heads]
        p = [jnp.exp(s[h] - m_new[h]) for h in heads]
        alpha = [jnp.exp(m_old[h:h + 1, :] - m_new[h]) for h in heads]
        l_new = [alpha[h] * l_old[h:h + 1, :] + jnp.sum(p[h], axis=0, keepdims=True) for h in heads]
        pv = [_dg(vt_ref[hsl[h], ks], p[h].astype(BF16), _NN) for h in heads]
        for h in heads:
            acc_ref[hsl[h], :] = alpha[h] * acc_ref[hsl[h], :] + pv[h]
        m_ref[...] = jnp.concatenate(m_new, axis=0)
        l_ref[...] = jnp.concatenate(l_new, axis=0)

    block(qi, True)

    def body(n, carry):
        block(n, False)
        return carry

    lax.fori_loop(0, qi, body, 0)
    for h in range(n_heads):
        hs = slice(h * HEAD_DIM, (h + 1) * HEAD_DIM)
        o_ref[hs, :] = acc_ref[hs, :] / l_ref[h:h + 1, :]


def moba_prompt(qt, k_rows, vt_b, kmean):
    b, width, t = qt.shape
    tq = MOBA_BLOCK
    nb = t // tq
    assert t % tq == 0
    n_heads = width // HEAD_DIM
    return _attn_prompt_call(functools.partial(_moba_prompt_kernel, tq=tq, width=width), "moba_prompt", qt, k_rows,
                             vt_b, [kmean], [pl.BlockSpec((None, nb, width), lambda bi, qi: (bi, 0, 0))],
                             [pltpu.VMEM((n_heads, nb, tq), F32), pltpu.VMEM((n_heads, tq), F32),
                              pltpu.VMEM((n_heads, tq), F32)], tq)


def _stack_heads(q, n_heads):
    t, width = q.shape
    rows = lax.broadcasted_iota(jnp.int32, (n_heads * t, width), 0) // t
    lanes = lax.broadcasted_iota(jnp.int32, (n_heads * t, width), 1) // HEAD_DIM
    return jnp.where(rows == lanes, jnp.concatenate([q] * n_heads, axis=0), 0.0)


def _unstack_heads(acc, n_heads):
    rows_total, width = acc.shape
    t = rows_total // n_heads
    rows = lax.broadcasted_iota(jnp.int32, (rows_total, width), 0) // t
    lanes = lax.broadcasted_iota(jnp.int32, (rows_total, width), 1) // HEAD_DIM
    kept = jnp.where(rows == lanes, acc, 0.0).reshape(n_heads, t, width)
    return jnp.sum(kept, axis=0)


def _from_here_matrix(tk):
    kj = lax.broadcasted_iota(jnp.int32, (2 * tk, tk), 0) % tk
    ks = lax.broadcasted_iota(jnp.int32, (2 * tk, tk), 1)
    return jnp.where(kj >= ks, 1.0, 0.0).astype(BF16)


def _sum_from_here(x, from_here):
    return _dg(jnp.concatenate(_split_bf16(x), axis=1), from_here, _NN)


def _sb_sample_kernel(pt_ref, q_ref, knew_ref, vnew_ref, *refs, n_pages_step, t_new, n_heads):
    k_refs = refs[:n_pages_step]
    v_refs = refs[n_pages_step:2 * n_pages_step]
    o_ref, qs_ref, acc_ref, run_ref = refs[2 * n_pages_step:]
    step = pl.program_id(1)
    tk = knew_ref.shape[0]
    from_here = _from_here_matrix(tk)

    def tile(k_tile, v_tile, mask, paged):
        z = _dg(qs_ref[...], k_tile.astype(BF16), _NN if paged else _NT)
        l1m = _log2_one_minus_sigmoid(z)
        if mask is not None:
            l1m = jnp.where(mask, l1m, 0.0)
        rest = _sum_from_here(l1m, from_here) + run_ref[...]
        w = jnp.exp2(z + rest)
        if mask is not None:
            w = jnp.where(mask, w, 0.0)
        acc_ref[...] += _mm(w, v_tile, 1, _NT if paged else _NN)
        run_ref[...] = rest[:, 0:1]

    @pl.when(step == 0)
    def _():
        qs_ref[...] = _stack_heads(q_ref[...], n_heads).astype(BF16)
        acc_ref[...] = jnp.zeros_like(acc_ref)
        run_ref[...] = jnp.zeros_like(run_ref)
        rows = lax.broadcasted_iota(jnp.int32, (n_heads * t_new, tk), 0) % t_new
        cols = lax.broadcasted_iota(jnp.int32, (n_heads * t_new, tk), 1)
        tile(knew_ref[...], vnew_ref[...], cols < rows, False)

    qs = qs_ref[...]
    z = [_dg(qs, k_refs[i][...].astype(BF16), _NN) for i in range(n_pages_step)]
    cum = [_sum_from_here(_log2_one_minus_sigmoid(x), from_here) for x in z]
    run = run_ref[...]
    acc = acc_ref[...]
    for i in range(n_pages_step):
        w = jnp.exp2(z[i] + cum[i] + run)
        acc = acc + _mm(w, v_refs[i][...], 1, _NT)
        run = run + cum[i][:, 0:1]
    acc_ref[...] = acc
    run_ref[...] = run

    @pl.when(step == pl.num_programs(1) - 1)
    def _():
        o_ref[...] = _unstack_heads(acc_ref[...], n_heads)


def _page_specs(n_pages, n_pages_step, width, page_size, reverse):
    def spec(i):
        def index(bi, si, pt):
            p = si * n_pages_step + i
            return (pt[bi, n_pages - 1 - p if reverse else p], 0, 0)
        return pl.BlockSpec((None, width, page_size), index)
    return [spec(i) for i in range(n_pages_step)]


def _pad_rows(x, rows):
    return jnp.pad(x, ((0, 0), (0, rows - x.shape[1]), (0, 0)))


def sb_sample(q, k_new, v_new, k_pool, v_pool, pages, n_pages_step):
    b, t_new, width = q.shape
    n_heads = width // HEAD_DIM
    page_size = k_pool.shape[2]
    n_pages = pages.shape[1]
    assert n_pages % n_pages_step == 0 and t_new <= page_size
    new = pl.BlockSpec((None, t_new, width), lambda bi, si, pt: (bi, 0, 0))
    new_pad = pl.BlockSpec((None, page_size, width), lambda bi, si, pt: (bi, 0, 0))
    page_specs = _page_specs(n_pages, n_pages_step, width, page_size, reverse=True)
    return pl.pallas_call(
        functools.partial(_sb_sample_kernel, n_pages_step=n_pages_step, t_new=t_new, n_heads=n_heads),
        grid_spec=pltpu.PrefetchScalarGridSpec(
            num_scalar_prefetch=1, grid=(b, n_pages // n_pages_step),
            in_specs=[new, new_pad, new_pad] + page_specs + page_specs,
            out_specs=new,
            scratch_shapes=[pltpu.VMEM((n_heads * t_new, width), BF16),
                            pltpu.VMEM((n_heads * t_new, width), F32),
                            pltpu.VMEM((n_heads * t_new, 1), F32)]),
        out_shape=jax.ShapeDtypeStruct((b, t_new, width), F32),
        compiler_params=_cparams("parallel", "arbitrary"),
        name="sb_sample",
    )(pages, q, _pad_rows(k_new, page_size), _pad_rows(v_new, page_size),
      *([k_pool] * n_pages_step), *([v_pool] * n_pages_step))


def _moba_sample_kernel(pt_ref, q_ref, knew_ref, vnew_ref, *refs, n_pages_step, pages_per_block, t_new, n_heads,
                        key_steps):
    k_refs = refs[:n_pages_step]
    v_refs = refs[n_pages_step:2 * n_pages_step]
    o_ref, qs_ref, kb_ref, kmean_ref, sel_ref, acc_ref, m_ref, l_ref = refs[2 * n_pages_step:]
    step = pl.program_id(1)
    nb = kmean_ref.shape[1]
    rows_total = n_heads * t_new
    page_size = k_refs[0].shape[1]

    def tile(k_tile, v_tile, mask, paged):
        s = jnp.where(mask, _dg(qs_ref[...], k_tile.astype(BF16), _NN if paged else _NT), NEG_BIG)
        m_new = jnp.maximum(m_ref[...], jnp.max(s, axis=1, keepdims=True))
        p = jnp.exp(s - m_new)
        alpha = jnp.exp(m_ref[...] - m_new)
        l_ref[...] = alpha * l_ref[...] + jnp.sum(p, axis=1, keepdims=True)
        acc_ref[...] = alpha * acc_ref[...] + _mm(p, v_tile, 1, _NT if paged else _NN)
        m_ref[...] = m_new

    @pl.when(step == 0)
    def _():
        qs_ref[...] = _stack_heads(q_ref[...], n_heads).astype(BF16)
        kmean_ref[...] = jnp.zeros_like(kmean_ref)
        acc_ref[...] = jnp.zeros_like(acc_ref)
        l_ref[...] = jnp.zeros_like(l_ref)
        m_ref[...] = jnp.full(m_ref.shape, NEG_BIG, F32)
        tk = knew_ref.shape[0]
        rows = lax.broadcasted_iota(jnp.int32, (rows_total, tk), 0) % t_new
        cols = lax.broadcasted_iota(jnp.int32, (rows_total, tk), 1)
        tile(knew_ref[...], vnew_ref[...], cols <= rows, False)

    @pl.when(step < key_steps)
    def _():
        blk_lane = lax.broadcasted_iota(jnp.int32, kmean_ref.shape, 1)
        kmean = kmean_ref[...]
        for n in range(n_pages_step // pages_per_block):
            tot = None
            for i in range(pages_per_block):
                page = n * pages_per_block + i
                k_page = k_refs[page][...]
                kb_ref[step * n_pages_step + page] = k_page.astype(BF16)
                tot = k_page if tot is None else tot + k_page
            mean = jnp.sum(tot, axis=1, keepdims=True) * (1.0 / MOBA_BLOCK)
            kmean = jnp.where(blk_lane == step * (n_pages_step // pages_per_block) + n, mean, kmean)
        kmean_ref[...] = kmean

    @pl.when(step == key_steps)
    def _():
        gate = _mm(_stack_heads(q_ref[...], n_heads), kmean_ref[...], 3)
        sel_ref[...] = jnp.where(_topk_block_mask(gate, nb, MOBA_TOPK, 1), 1.0, 0.0)

    @pl.when(step >= key_steps)
    def _():
        first = (step - key_steps) * n_pages_step
        qs = qs_ref[...]
        n_keys = n_pages_step * page_size
        key_blk = (first * page_size + lax.broadcasted_iota(jnp.int32, (nb, n_keys), 1)) // MOBA_BLOCK
        expand = jnp.where(key_blk == lax.broadcasted_iota(jnp.int32, (nb, n_keys), 0), 1.0, 0.0).astype(BF16)
        picked = _dg(sel_ref[...].astype(BF16), expand, _NN) > 0.5
        mask = [picked[:, i * page_size:(i + 1) * page_size] for i in range(n_pages_step)]
        s = [jnp.where(mask[i], _dg(qs, kb_ref[first + i], _NN), NEG_BIG) for i in range(n_pages_step)]
        s_max = s[0]
        for x in s[1:]:
            s_max = jnp.maximum(s_max, x)
        m_old = m_ref[...]
        m_new = jnp.maximum(m_old, jnp.max(s_max, axis=1, keepdims=True))
        p = [jnp.exp(x - m_new) for x in s]
        alpha = jnp.exp(m_old - m_new)
        p_sum = p[0]
        acc = alpha * acc_ref[...] + _mm(p[0], v_refs[0][...], 1, _NT)
        for i in range(1, n_pages_step):
            p_sum = p_sum + p[i]
            acc = acc + _mm(p[i], v_refs[i][...], 1, _NT)
        l_ref[...] = alpha * l_ref[...] + jnp.sum(p_sum, axis=1, keepdims=True)
        acc_ref[...] = acc
        m_ref[...] = m_new

    @pl.when(step == pl.num_programs(1) - 1)
    def _():
        o_ref[...] = _unstack_heads(acc_ref[...] / l_ref[...], n_heads)


def moba_sample(q, k_new, v_new, k_pool, v_pool, pages, n_pages_step):
    b, t_new, width = q.shape
    n_heads = width // HEAD_DIM
    page_size = k_pool.shape[2]
    n_pages = pages.shape[1]
    pages_per_block = MOBA_BLOCK // page_size
    nb = n_pages // pages_per_block
    key_steps = n_pages // n_pages_step
    assert n_pages % n_pages_step == 0 and n_pages_step % pages_per_block == 0 and t_new <= page_size
    assert (n_pages * page_size) % MOBA_BLOCK == 0 and t_new <= MOBA_BLOCK
    new = pl.BlockSpec((None, t_new, width), lambda bi, si, pt: (bi, 0, 0))
    new_pad = pl.BlockSpec((None, page_size, width), lambda bi, si, pt: (bi, 0, 0))

    def page_spec(i, keys):
        def index(bi, si, pt):
            s = jnp.minimum(si, key_steps - 1) if keys else jnp.maximum(si - key_steps, 0)
            return (pt[bi, s * n_pages_step + i], 0, 0)
        return pl.BlockSpec((None, width, page_size), index)

    rows_total = n_heads * t_new
    return pl.pallas_call(
        functools.partial(_moba_sample_kernel, n_pages_step=n_pages_step, pages_per_block=pages_per_block,
                          t_new=t_new, n_heads=n_heads, key_steps=key_steps),
        grid_spec=pltpu.PrefetchScalarGridSpec(
            num_scalar_prefetch=1, grid=(b, 2 * key_steps),
            in_specs=[new, new_pad, new_pad] + [page_spec(i, True) for i in range(n_pages_step)]
                     + [page_spec(i, False) for i in range(n_pages_step)],
            out_specs=new,
            scratch_shapes=[pltpu.VMEM((rows_total, width), BF16), pltpu.VMEM((n_pages, width, page_size), BF16),
                            pltpu.VMEM((width, nb), F32), pltpu.VMEM((rows_total, nb), F32),
                            pltpu.VMEM((rows_total, width), F32), pltpu.VMEM((rows_total, 1), F32),
                            pltpu.VMEM((rows_total, 1), F32)]),
        out_shape=jax.ShapeDtypeStruct((b, t_new, width), F32),
        compiler_params=_cparams("parallel", "arbitrary"),
        name="moba_sample",
    )(pages, q, _pad_rows(k_new, page_size), _pad_rows(v_new, page_size),
      *([k_pool] * n_pages_step), *([v_pool] * n_pages_step))


def _merge_kernel(x_ref, g_ref, wg_ref, y0_ref, y1_ref, y2_ref, y3_ref, wb_ref, wo_ref, o_ref, *, d_model, layouts):
    x = x_ref[...]
    h = _rms(x, g_ref[...]).astype(BF16)
    merged = None
    for n, (y_ref, layout) in enumerate(zip((y0_ref, y1_ref, y2_ref, y3_ref), layouts)):
        gate = jax.nn.sigmoid(_dg(h, wg_ref[:, n * d_model:(n + 1) * d_model], _NN))
        term = gate * _mm(y_ref[...], wb_ref[n], 1, _TN if layout == "transposed" else _NN)
        merged = term if merged is None else merged + term
    o_ref[...] = x + _mm(merged, wo_ref[...])


def merge(x, g, w_gate, ys, layouts, w_branch, w_out, tm):
    b, t, d = x.shape
    width = w_branch.shape[1]
    tm = min(tm, t)
    assert t % tm == 0
    fixed = lambda bi, ti: (0, 0)
    seq = lambda bi, ti: (bi, ti, 0)
    y_spec = {"rows": pl.BlockSpec((None, tm, width), seq),
              "time_major": pl.BlockSpec((tm, width), lambda bi, ti: (ti, bi)),
              "transposed": pl.BlockSpec((None, width, tm), lambda bi, ti: (bi, 0, ti))}
    return pl.pallas_call(
        functools.partial(_merge_kernel, d_model=d, layouts=tuple(layouts)),
        grid=(b, t // tm),
        in_specs=[pl.BlockSpec((None, tm, d), seq), pl.BlockSpec((1, d), fixed), _resident((d, N_BRANCH * d))]
                 + [y_spec[k] for k in layouts]
                 + [_resident((N_BRANCH, width, d)), _resident((d, d))],
        out_specs=pl.BlockSpec((None, tm, d), seq),
        out_shape=jax.ShapeDtypeStruct((b, t, d), F32),
        compiler_params=_cparams("parallel", "parallel"),
        name="merge",
    )(x, g.reshape(1, d), w_gate, *ys, w_branch, w_out)


def _ffn_kernel(x_ref, g_ref, wup_ref, cw_ref, cb_ref, wdn_ref, conv0_ref, o_ref, conv_ref, st_ref,
                *, nseq, tt, d_ff, col_chunk):
    ti = pl.program_id(1)

    @pl.when(ti == 0)
    def _():
        st_ref[...] = conv0_ref[...]

    d_model = x_ref.shape[-1]
    x = x_ref[...].reshape(nseq * tt, d_model)
    h = _rms(x, g_ref[...]).astype(BF16)
    t_idx = lax.broadcasted_iota(jnp.int32, (1, tt, 1), 1)

    def conv_cols(c0):
        cols = slice(c0, c0 + col_chunk)
        up = _dg(h, wup_ref[:, cols], _NN).reshape(nseq, tt, col_chunk)
        st = st_ref[:, :, cols]
        p1 = jnp.where(t_idx == 0, st[:, 1:2, :], pltpu.roll(up, 1, axis=1))
        p2 = jnp.where(t_idx == 0, st[:, 0:1, :], jnp.where(t_idx == 1, st[:, 1:2, :], pltpu.roll(up, 2, axis=1)))
        st_ref[:, :, cols] = up[:, tt - 2:, :]
        cw = cw_ref[:, cols]
        c = cb_ref[:, cols] + cw[2:3, :] * up + cw[1:2, :] * p1 + cw[0:1, :] * p2
        return c.reshape(nseq * tt, col_chunk)

    acc = x
    for c0 in range(0, d_ff, col_chunk):
        a = conv_cols(c0)
        b = conv_cols(d_ff + c0)
        acc = acc + _mm(a * jax.nn.sigmoid(a) * b, wdn_ref[c0:c0 + col_chunk, :])
    o_ref[...] = acc.reshape(nseq, tt, d_model)
    conv_ref[...] = st_ref[...]


def conv_ffn(x, g, w_up, conv_w, conv_b, w_down, conv0, nseq, tt, col_chunk):
    b, t, d = x.shape
    d_ff = w_down.shape[0]
    assert b % nseq == 0 and t % tt == 0 and d_ff % col_chunk == 0 and tt >= CONV_W - 1
    fixed = lambda bi, ti: (0, 0)
    seq = lambda bi, ti: (bi, ti, 0)
    st = lambda bi, ti: (bi, 0, 0)
    return pl.pallas_call(
        functools.partial(_ffn_kernel, nseq=nseq, tt=tt, d_ff=d_ff, col_chunk=col_chunk),
        grid=(b // nseq, t // tt),
        in_specs=[pl.BlockSpec((nseq, tt, d), seq), pl.BlockSpec((1, d), fixed),
                  _resident((d, 2 * d_ff)), pl.BlockSpec((CONV_W, 2 * d_ff), fixed),
                  pl.BlockSpec((1, 2 * d_ff), fixed), _resident((d_ff, d)),
                  pl.BlockSpec((nseq, CONV_W - 1, 2 * d_ff), st)],
        out_specs=[pl.BlockSpec((nseq, tt, d), seq), pl.BlockSpec((nseq, CONV_W - 1, 2 * d_ff), st)],
        out_shape=[jax.ShapeDtypeStruct((b, t, d), F32), jax.ShapeDtypeStruct((b, CONV_W - 1, 2 * d_ff), F32)],
        scratch_shapes=[pltpu.VMEM((nseq, CONV_W - 1, 2 * d_ff), F32)],
        compiler_params=_cparams("parallel", "arbitrary"),
        name="conv_ffn",
    )(x, g.reshape(1, d), w_up, conv_w, conv_b.reshape(1, 2 * d_ff), w_down, conv0)


MOBA_Q_SCALE = HEAD_DIM ** -0.5
SB_Q_SCALE = HEAD_DIM ** -0.5 * LOG2_E

PROMPT_TILES = dict(proj_tm=1024, qkv_tm=512, s5_tt=128, s5_passes=1, rwkv_tt=256, rwkv_chunk=128, rwkv_passes=1,
                    sb_tq=256, merge_tm=512, ffn_tt=512, ffn_cols=1408)
SAMPLE_TILES = dict(proj_tm=256, s5_tt=8, s5_passes=3, rwkv_tt=128, rwkv_chunk=64, rwkv_passes=1,
                    merge_tm=256, ffn_cols=1408, sb_pages=16, moba_pages=16)


def _layer(x, st, past, p, tiles):
    b, t, d = x.shape
    width = d // 2
    n_heads = width // HEAD_DIM
    n_rwkv = 3 * width + RWKV_DECAY_LORA + RWKV_ICLR_LORA + RWKV_GATE_LORA
    prompt = past is None
    n_state = p['s5_ab_re'].shape[0]
    s5_args = (st['s5_re'].reshape(b, n_state), st['s5_im'].reshape(b, n_state), p['s5_ab_re'], p['s5_ab_im'],
               p['s5_bb'][tiles['s5_passes']], p['s5_cc'][tiles['s5_passes']], p['s5_d'], p['s5_w_glu'],
               tiles['s5_tt'], tiles['s5_passes'])

    if prompt:
        u_s5, u_rw = proj_plain(x, p['norm1_g'], p['w_mix'], (width, n_rwkv), tiles['proj_tm'], True)
        y_s5, s5_re, s5_im = s5_mixer(u_s5.reshape(t, b, width), *s5_args)
        y_s5 = y_s5.reshape(t, b * width)
        qt, k_sb, v_sb, k_rows, vt_b, _ = proj_qkv_t(x, p['norm1_g'], p['w_sb'].T, p['sb_q_g'], p['sb_k_g'],
                                                     tiles['qkv_tm'], SB_Q_SCALE, p['layer'], p['depth'],
                                                     st['kv_sb'])
        y_sb = sb_prompt(qt, k_rows, vt_b, tiles['sb_tq'])
        qt, k_mb, v_mb, k_rows, vt_b, kmean = proj_qkv_t(x, p['norm1_g'], p['w_mb'].T, p['moba_q_g'],
                                                         p['moba_k_g'], tiles['qkv_tm'], MOBA_Q_SCALE,
                                                         p['layer'], p['depth'], st['kv_mb'])
        y_mb = moba_prompt(qt, k_rows, vt_b, kmean)
        layouts = ("time_major", "rows", "transposed", "transposed")
        x_m = x
        kv = lambda a: a
    else:
        x_m = x.reshape(1, b * t, d)
        u_s5, u_rw = proj_plain(x_m, p['norm1_g'], p['w_mix'], (width, n_rwkv), b * t, False)
        y_s5, s5_re, s5_im = s5_mixer(u_s5.reshape(b, t, width).transpose(1, 0, 2), *s5_args)
        y_s5 = y_s5.transpose(1, 0, 2).reshape(1, b * t, width)
        x2 = x.reshape(b * t, d)
        seq = lambda a: a.reshape(b, t, width)
        q_sb, k_sb, v_sb = proj_qkv(x2, p['norm1_g'], p['w_sb'], p['sb_q_g'], p['sb_k_g'], b * t,
                                    SB_Q_SCALE)
        q_mb, k_mb, v_mb = proj_qkv(x2, p['norm1_g'], p['w_mb'], p['moba_q_g'], p['moba_k_g'], b * t,
                                    MOBA_Q_SCALE)
        y_sb = sb_sample(seq(q_sb), seq(k_sb), seq(v_sb), past['sb_k'], past['sb_v'], past['pages'],
                         tiles['sb_pages'])
        y_mb = moba_sample(seq(q_mb), seq(k_mb), seq(v_mb), past['moba_k'], past['moba_v'], past['pages'],
                           tiles['moba_pages'])
        y_sb, y_mb = y_sb.reshape(1, b * t, width), y_mb.reshape(1, b * t, width)
        layouts = ("rows",) * N_BRANCH
        kv = lambda a: a.reshape(b, t, n_heads, HEAD_DIM)

    y_rw, wkv, shift = rwkv_mixer(u_rw.reshape(b, t, n_rwkv), st['shift'], st['wkv'], p, tiles['rwkv_tt'],
                                  tiles['rwkv_chunk'], tiles['rwkv_passes'])
    ys = [y_s5, y_rw.reshape(x_m.shape[0], x_m.shape[1], width), y_sb, y_mb]
    x1 = merge(x_m, p['norm1_g'], p['w_gate'], ys, layouts, p['w_branch'], p['w_out'], tiles['merge_tm'])
    nseq, tt = (1, tiles['ffn_tt']) if prompt else (b, t)
    x_out, conv = conv_ffn(x1.reshape(b, t, d), p['norm2_g'], p['ffn_w_up'], p['ffn_conv_w'], p['ffn_conv_b'],
                           p['ffn_w_down'], st['conv'], nseq, tt, tiles['ffn_cols'])
    new_st = (s5_re.reshape(st['s5_re'].shape), s5_im.reshape(st['s5_im'].shape), wkv, shift, conv,
              kv(k_sb), kv(v_sb), kv(k_mb), kv(v_mb))
    return x_out, new_st


def kernel(x_prompt, x_sample, state_s5_re, state_s5_im, state_rwkv_wkv, state_rwkv_shift, state_ffn_conv, cache_sb_k, cache_sb_v, cache_moba_k, cache_moba_v, page_table, norm1_g, w_in, s5_a_re, s5_a_im, s5_log_dt, s5_b_re, s5_b_im, s5_c_re, s5_c_im, s5_d, s5_w_glu, rwkv_mu, rwkv_w0, rwkv_w_up, rwkv_a0, rwkv_a_up, rwkv_g_up, rwkv_k_k, rwkv_k_a, rwkv_r_k, rwkv_ln_w, rwkv_ln_b, sb_q_g, sb_k_g, moba_q_g, moba_k_g, w_branch, w_out, norm2_g, ffn_w_up, ffn_conv_w, ffn_conv_b, ffn_w_down):
    depth = w_in.shape[0]
    bp = x_prompt.shape[0]
    n_pool, page_size, n_heads, head_dim = cache_sb_k.shape[1:]
    width = n_heads * head_dim
    pool = lambda c: c.transpose(0, 1, 3, 4, 2).reshape(depth * n_pool, width, page_size)
    pools = dict(sb_k=pool(cache_sb_k), sb_v=pool(cache_sb_v), moba_k=pool(cache_moba_k), moba_v=pool(cache_moba_v))
    g_s5, p_s5 = s5_a_re.shape[1:]
    n_rwkv = state_rwkv_shift.shape[-1]
    d_ff2 = state_ffn_conv.shape[-1]
    zero_st = dict(s5_re=jnp.zeros((bp, g_s5, p_s5), F32), s5_im=jnp.zeros((bp, g_s5, p_s5), F32),
                   shift=jnp.zeros((bp, n_rwkv), F32), wkv=jnp.zeros((bp, n_heads, head_dim, head_dim), F32),
                   conv=jnp.zeros((bp, CONV_W - 1, d_ff2), F32))
    yp, ys = x_prompt, x_sample
    states_p, states_s = [], []
    kv_sb = kv_mb = None
    for l in range(depth):
        ab_re, ab_im, bb, cc = s5_operands(s5_a_re[l], s5_a_im[l], s5_log_dt[l], s5_b_re[l], s5_b_im[l],
                                           s5_c_re[l], s5_c_im[l])
        width_l = w_in.shape[1] // 2
        off_sb = width_l + state_rwkv_shift.shape[-1]
        off_mb, off_gate = off_sb + 3 * width_l, off_sb + 6 * width_l
        w_piece = lambda a, b: w_in[l, :, a:b].astype(BF16)
        p = dict(layer=l, depth=depth, norm1_g=norm1_g[l], w_mix=w_piece(0, off_sb), w_sb=w_piece(off_sb, off_mb),
                 w_mb=w_piece(off_mb, off_gate), w_gate=w_piece(off_gate, w_in.shape[2]),
                 s5_ab_re=ab_re, s5_ab_im=ab_im,
                 s5_bb={1: bb.astype(BF16), 3: bb}, s5_cc={1: cc.astype(BF16), 3: cc}, s5_d=s5_d[l],
                 s5_w_glu=s5_w_glu[l].astype(BF16), rwkv_mu=rwkv_mu[l], rwkv_w0=rwkv_w0[l],
                 rwkv_w_up=rwkv_w_up[l], rwkv_a0=rwkv_a0[l], rwkv_a_up=rwkv_a_up[l], rwkv_g_up=rwkv_g_up[l],
                 rwkv_k_k=rwkv_k_k[l], rwkv_k_a=rwkv_k_a[l], rwkv_r_k=rwkv_r_k[l].reshape(-1),
                 rwkv_ln_w=rwkv_ln_w[l], rwkv_ln_b=rwkv_ln_b[l], sb_q_g=sb_q_g[l], sb_k_g=sb_k_g[l],
                 moba_q_g=moba_q_g[l], moba_k_g=moba_k_g[l], w_branch=w_branch[l].astype(BF16),
                 w_out=w_out[l].astype(BF16), norm2_g=norm2_g[l], ffn_w_up=ffn_w_up[l].astype(BF16),
                 ffn_conv_w=ffn_conv_w[l], ffn_conv_b=ffn_conv_b[l], ffn_w_down=ffn_w_down[l].astype(BF16))
        yp, st_p = _layer(yp, dict(zero_st, kv_sb=kv_sb, kv_mb=kv_mb), None, p, PROMPT_TILES)
        kv_sb, kv_mb = st_p[5:7], st_p[7:9]
        st_s = dict(s5_re=state_s5_re[l], s5_im=state_s5_im[l], shift=state_rwkv_shift[l], wkv=state_rwkv_wkv[l],
                    conv=state_ffn_conv[l])
        past = dict(pools, pages=page_table + l * n_pool)
        ys, st_s = _layer(ys, st_s, past, p, SAMPLE_TILES)
        states_p.append(st_p[:5])
        states_s.append(st_s)
    t_p = x_prompt.shape[1]
    kv_out = lambda a: a.reshape(depth, bp, n_heads, head_dim, t_p).transpose(0, 1, 4, 2, 3)
    stacked_p = [jnp.stack(z, axis=0) for z in zip(*states_p)] + [kv_out(a) for a in (*kv_sb, *kv_mb)]
    stacked_s = [jnp.stack(z, axis=0) for z in zip(*states_s)]
    out = [yp, ys]
    for a, c in zip(stacked_p, stacked_s):
        out += [a, c]
    return tuple(out)
```

```python
import functools
import math

import jax
import jax.numpy as jnp
from jax import lax
from jax.experimental import pallas as pl
from jax.experimental.pallas import tpu as pltpu

F32 = jnp.float32
BF16 = jnp.bfloat16

HEAD_DIM = 64
S5_GROUP = 16
S5_STATE = 64
RWKV_DECAY_LORA = 64
RWKV_ICLR_LORA = 64
RWKV_GATE_LORA = 128
N_BRANCH = 4
MOBA_BLOCK = 256
MOBA_TOPK = 3
CONV_W = 3
EPS = 1e-6
RWKV_GN_EPS = 64e-5
RWKV_MAX_CHUNK = 128

V7X_VMEM_LIMIT_BYTES = 56 * 1024 * 1024
SUBLANES = 8
LANES = 128


def _resident(shape):
    zeros = (0,) * len(shape)
    return pl.BlockSpec(shape, lambda *_: zeros, pipeline_mode=pl.Buffered(1))


def _cparams(*sem):
    return pltpu.CompilerParams(dimension_semantics=sem, vmem_limit_bytes=V7X_VMEM_LIMIT_BYTES)


def _split_bf16(x):
    hi = x.astype(BF16)
    lo = (x - hi.astype(F32)).astype(BF16)
    return hi, lo


_NN = (((1,), (0,)), ((), ()))
_NT = (((1,), (1,)), ((), ()))


def _dg(a, b, dims):
    return lax.dot_general(a, b, dims, preferred_element_type=F32)


def _mm(a, b, passes=1, dims=_NN):
    if passes == 1:
        return _dg(a.astype(BF16), b.astype(BF16), dims)
    ah, al = _split_bf16(a)
    if b.dtype == BF16:
        return _dg(ah, b, dims) + _dg(al, b, dims)
    bh, bl = _split_bf16(b)
    if passes == 2:
        return _dg(ah, bh, dims) + _dg(al, bh, dims)
    return _dg(ah, bh, dims) + (_dg(al, bh, dims) + _dg(ah, bl, dims))


def _head_block_matrix(width, value):
    r = jnp.arange(width) // HEAD_DIM
    return jnp.where(r[:, None] == r[None, :], value, 0.0).astype(BF16)


def _rms(x, g):
    return x * lax.rsqrt(jnp.mean(x * x, axis=-1, keepdims=True) + EPS) * g


def _proj_plain_kernel(x_ref, g_ref, w_ref, *out_refs, widths):
    h = _rms(x_ref[...], g_ref[...]).astype(BF16)
    u = _dg(h, w_ref[...], _NN)
    off = 0
    for o_ref, n in zip(out_refs, widths):
        o_ref[...] = u[:, off:off + n]
        off += n


def proj_plain(x, g, w, widths, tm, first_time_major):
    b, t, d = x.shape
    n = w.shape[1]
    assert sum(widths) == n and t % tm == 0
    fixed = lambda bi, ti: (0, 0)
    seq = lambda bi, ti: (bi, ti, 0)
    out_specs = [pl.BlockSpec((None, tm, k), seq) for k in widths]
    out_shape = [jax.ShapeDtypeStruct((b, t, k), F32) for k in widths]
    if first_time_major:
        out_specs[0] = pl.BlockSpec((tm, widths[0]), lambda bi, ti: (ti, bi))
        out_shape[0] = jax.ShapeDtypeStruct((t, b * widths[0]), F32)
    return pl.pallas_call(
        functools.partial(_proj_plain_kernel, widths=widths),
        grid=(b, t // tm),
        in_specs=[pl.BlockSpec((None, tm, d), seq), pl.BlockSpec((1, d), fixed), _resident((d, n))],
        out_specs=out_specs,
        out_shape=out_shape,
        compiler_params=_cparams("parallel", "parallel"),
        name="proj_plain",
    )(x, g.reshape(1, d), w)


def _head_rms(x, hm, g):
    ms = _mm(x * x, hm, passes=2)
    return x * lax.rsqrt(ms + EPS) * g


def _proj_qkv_kernel(x_ref, g_ref, w_ref, hm_ref, gq_ref, gk_ref, q_ref, k_ref, v_ref, *, width, q_scale):
    h = _rms(x_ref[...], g_ref[...]).astype(BF16)
    u = _dg(h, w_ref[...], _NN)
    hm = hm_ref[...]
    q_ref[...] = _head_rms(u[:, :width], hm, gq_ref[...]) * q_scale
    k_ref[...] = _head_rms(u[:, width:2 * width], hm, gk_ref[...])
    v_ref[...] = u[:, 2 * width:]


def proj_qkv(x2d, g, w, g_q, g_k, tm, q_scale):
    m, d = x2d.shape
    width = w.shape[1] // 3
    n_heads = width // HEAD_DIM
    hm = _head_block_matrix(width, 1.0 / HEAD_DIM)
    row = lambda i: (i, 0)
    fixed = lambda i: (0, 0)
    return pl.pallas_call(
        functools.partial(_proj_qkv_kernel, width=width, q_scale=q_scale),
        grid=(m // tm,),
        in_specs=[pl.BlockSpec((tm, d), row), pl.BlockSpec((1, d), fixed),
                  pl.BlockSpec((d, 3 * width), fixed), pl.BlockSpec((width, width), fixed),
                  pl.BlockSpec((1, width), fixed), pl.BlockSpec((1, width), fixed)],
        out_specs=[pl.BlockSpec((tm, width), row)] * 3,
        out_shape=[jax.ShapeDtypeStruct((m, width), F32)] * 3,
        compiler_params=_cparams("parallel"),
        name="proj_qkv",
    )(x2d, g.reshape(1, d), w, hm, jnp.tile(g_q, n_heads).reshape(1, width),
      jnp.tile(g_k, n_heads).reshape(1, width))


def _s5_disc_kernel(are_ref, aim_ref, ldt_ref, bre_ref, bim_ref, abre_ref, abim_ref, bbre_ref, bbim_ref):
    lr = are_ref[...]
    li = aim_ref[...]
    dt = jnp.exp(ldt_ref[...])
    mag = jnp.exp(lr * dt)
    ab_re = mag * jnp.cos(li * dt)
    ab_im = mag * jnp.sin(li * dt)
    den = lr * lr + li * li
    zr = ((ab_re - 1.0) * lr + ab_im * li) / den
    zi = (ab_im * lr - (ab_re - 1.0) * li) / den
    abre_ref[...] = ab_re
    abim_ref[...] = ab_im
    br = bre_ref[...]
    bi = bim_ref[...]
    bbre_ref[...] = zr[:, None, :] * br - zi[:, None, :] * bi
    bbim_ref[...] = zr[:, None, :] * bi + zi[:, None, :] * br


def s5_discretize(a_re, a_im, log_dt, b_re, b_im):
    g, p = a_re.shape
    gc = b_re.shape[1]
    return pl.pallas_call(
        _s5_disc_kernel,
        out_shape=[jax.ShapeDtypeStruct((g, p), F32)] * 2 + [jax.ShapeDtypeStruct((g, gc, p), F32)] * 2,
        name="s5_discretize",
    )(a_re, a_im, log_dt.reshape(g, 1), b_re, b_im)


def _gelu_tanh(x):
    c = math.sqrt(2.0 / math.pi)
    return 0.5 * x * (1.0 + jnp.tanh(c * (x + 0.044715 * (x * x * x))))


def _s5_kernel(u_ref, s0re_ref, s0im_ref, abre_ref, abim_ref, bb_ref, cc_ref, d_ref, wglu_ref,
               y_ref, sre_ref, sim_ref, xs_ref, cre_ref, cim_ref, *, tt, n_state, passes, lane_chunk):
    ti = pl.program_id(1)

    @pl.when(ti == 0)
    def _():
        cre_ref[...] = s0re_ref[...]
        cim_ref[...] = s0im_ref[...]

    width = u_ref.shape[-1]
    n_cl, c_in, c_state = bb_ref.shape
    half = c_state // 2
    u = u_ref[...].reshape(tt * SUBLANES, width)
    for c in range(n_cl):
        xc = _mm(u[:, c * c_in:(c + 1) * c_in], bb_ref[c], passes)
        xs_ref[:, c * half:(c + 1) * half] = xc[:, :half]
        xs_ref[:, n_state + c * half:n_state + (c + 1) * half] = xc[:, half:]

    for c0 in range(0, n_state, lane_chunk):
        re_sl = pl.ds(c0, lane_chunk)
        im_sl = pl.ds(n_state + c0, lane_chunk)
        a_re = jnp.broadcast_to(abre_ref[:, re_sl], (SUBLANES, lane_chunk))
        a_im = jnp.broadcast_to(abim_ref[:, re_sl], (SUBLANES, lane_chunk))

        def step(t, carry):
            s_re, s_im = carry
            rows = pl.ds(pl.multiple_of(t * SUBLANES, SUBLANES), SUBLANES)
            n_re = a_re * s_re - a_im * s_im + xs_ref[rows, re_sl]
            n_im = a_re * s_im + a_im * s_re + xs_ref[rows, im_sl]
            xs_ref[rows, re_sl] = n_re
            xs_ref[rows, im_sl] = n_im
            return n_re, n_im

        s_re, s_im = lax.fori_loop(0, tt, step, (cre_ref[:, re_sl], cim_ref[:, re_sl]), unroll=8)
        cre_ref[:, re_sl] = s_re
        cim_ref[:, re_sl] = s_im

    y = jnp.concatenate(
        [_mm(xs_ref[:, c * half:(c + 1) * half], cc_ref[c, :half, :], passes)
         + _mm(xs_ref[:, n_state + c * half:n_state + (c + 1) * half], cc_ref[c, half:, :], passes)
         for c in range(n_cl)], axis=1) + d_ref[...] * u
    y = _gelu_tanh(y)
    y = y * jax.nn.sigmoid(_mm(y, wglu_ref[...]))
    y_ref[...] = y.reshape(tt, SUBLANES, width)
    sre_ref[...] = cre_ref[...]
    sim_ref[...] = cim_ref[...]


def s5_mixer(u_tm, s0_re, s0_im, ab_re, ab_im, bb, cc, d_skip, w_glu, tt, passes):
    t, b, width = u_tm.shape
    n_state = ab_re.shape[-1]
    assert t % tt == 0 and b % SUBLANES == 0
    fixed = lambda bi, ti: (0, 0)
    st = lambda bi, ti: (bi, 0)
    return pl.pallas_call(
        functools.partial(_s5_kernel, tt=tt, n_state=n_state, passes=passes, lane_chunk=512),
        grid=(b // SUBLANES, t // tt),
        in_specs=[pl.BlockSpec((tt, SUBLANES, width), lambda bi, ti: (ti, bi, 0)),
                  pl.BlockSpec((SUBLANES, n_state), st), pl.BlockSpec((SUBLANES, n_state), st),
                  pl.BlockSpec((1, n_state), fixed), pl.BlockSpec((1, n_state), fixed),
                  pl.BlockSpec(bb.shape, lambda bi, ti: (0, 0, 0)), pl.BlockSpec(cc.shape, lambda bi, ti: (0, 0, 0)),
                  pl.BlockSpec((1, width), fixed), pl.BlockSpec((width, width), fixed)],
        out_specs=[pl.BlockSpec((tt, SUBLANES, width), lambda bi, ti: (ti, bi, 0)),
                   pl.BlockSpec((SUBLANES, n_state), st), pl.BlockSpec((SUBLANES, n_state), st)],
        out_shape=[jax.ShapeDtypeStruct((t, b, width), F32),
                   jax.ShapeDtypeStruct((b, n_state), F32), jax.ShapeDtypeStruct((b, n_state), F32)],
        scratch_shapes=[pltpu.VMEM((tt * SUBLANES, 2 * n_state), F32),
                        pltpu.VMEM((SUBLANES, n_state), F32), pltpu.VMEM((SUBLANES, n_state), F32)],
        compiler_params=_cparams("parallel", "arbitrary"),
        name="s5_mixer",
    )(u_tm, s0_re, s0_im, ab_re.reshape(1, n_state), ab_im.reshape(1, n_state), bb, cc,
      d_skip.reshape(1, width), w_glu)


def s5_operands(a_re, a_im, log_dt, b_re, b_im, c_re, c_im):
    g, p = a_re.shape
    gc = b_re.shape[-1]
    ab_re, ab_im, bb_re, bb_im = s5_discretize(a_re, a_im, log_dt, b_re.transpose(0, 2, 1),
                                               b_im.transpose(0, 2, 1))
    cl = LANES // gc
    assert g % cl == 0
    eye = jnp.eye(cl, dtype=F32)
    split = lambda m: m.reshape(g // cl, cl, gc, p)
    bd = lambda m: jnp.einsum('kacp,ab->kacbp', split(m), eye).reshape(g // cl, cl * gc, cl * p)
    bb = jnp.concatenate([bd(bb_re), bd(bb_im)], axis=2)
    bdc = lambda m: jnp.einsum('kacp,ab->kbpac', split(m), eye).reshape(g // cl, cl * p, cl * gc)
    cc = jnp.concatenate([bdc(c_re), -bdc(c_im)], axis=1)
    return ab_re.reshape(g * p), ab_im.reshape(g * p), bb, cc


def _softplus(x):
    return jnp.maximum(x, 0.0) + jnp.log(1.0 + jnp.exp(-jnp.abs(x)))


def _log2_one_minus_sigmoid(x):
    nx = -x
    return jnp.minimum(nx, 0.0) - jnp.log2(1.0 + jnp.exp2(jnp.minimum(x, nx)))


LOG2_E = math.log2(math.e)


def _rwkv_kernel(u_ref, shift0_ref, wkv0_ref, mu_ref, w0_ref, wup_ref, a0_ref, aup_ref, gup_ref, kk_ref,
                 ka_ref, rk_ref, lnw_ref, lnb_ref, hsum_ref, y_ref, wkv_ref, shift_ref,
                 prev_ref, s_ref, yt_ref, *, tt, chunk, width, passes, t_valid):
    ti = pl.program_id(1)
    n_heads = width // HEAD_DIM

    @pl.when(ti == 0)
    def _():
        prev_ref[...] = shift0_ref[...]
        s_ref[...] = wkv0_ref[...]

    u = u_ref[...]
    row = lax.broadcasted_iota(jnp.int32, (tt, 1), 0)
    prev = jnp.where(row == 0, prev_ref[...], pltpu.roll(u, 1, axis=0))
    xs = u + mu_ref[...] * (prev - u)
    prev_ref[...] = u[t_valid - 1:t_valid, :]
    shift_ref[...] = u[t_valid - 1:t_valid, :]

    w3 = 3 * width
    o1 = w3 + RWKV_DECAY_LORA
    o2 = o1 + RWKV_ICLR_LORA
    r = xs[:, :width]
    k = xs[:, width:2 * width]
    v = xs[:, 2 * width:w3]
    w_log = -_softplus(-(w0_ref[...] + _mm(jnp.tanh(xs[:, w3:o1]), wup_ref[...], 3))) - 0.5
    log_decay = -jnp.exp(w_log)
    a = jax.nn.sigmoid(a0_ref[...] + _mm(xs[:, o1:o2], aup_ref[...], 3))
    g = _mm(jax.nn.sigmoid(xs[:, o2:]), gup_ref[...], 3)
    hsum = hsum_ref[...]
    kk = k * kk_ref[...]
    kk = kk / jnp.maximum(jnp.sqrt(_mm(kk * kk, hsum, 2)), 1e-12)
    k2 = k * (1.0 + (a - 1.0) * ka_ref[...])
    a_vec = -kk
    b_vec = kk * a
    if t_valid < tt:
        live = row < t_valid
        log_decay = jnp.where(live, log_decay, 0.0)
        a_vec = jnp.where(live, a_vec, 0.0)
        b_vec = jnp.where(live, b_vec, 0.0)
        k2 = jnp.where(live, k2, 0.0)
    vt = v.T

    ri = lax.broadcasted_iota(jnp.int32, (chunk, chunk), 0)
    ci = lax.broadcasted_iota(jnp.int32, (chunk, chunk), 1)
    upper_strict = ri < ci
    upper_incl = ri <= ci
    lower_incl = (ri >= ci).astype(F32)
    n_double = (min(chunk, t_valid) - 1).bit_length()

    heads = range(n_heads)
    hsl = [slice(h * HEAD_DIM, (h + 1) * HEAD_DIM) for h in heads]
    eye = (ri == ci).astype(F32)

    pre = []
    for c in range(tt // chunk):
        rows = slice(c * chunk, (c + 1) * chunk)
        ld = log_decay[rows]
        cl = _mm(lower_incl, ld, 3)
        e_incl = jnp.exp(cl)
        e_inv = jnp.exp(-cl)
        rt = r[rows] * e_incl
        at = a_vec[rows] * jnp.exp(cl - ld)
        bt = b_vec[rows] * e_inv
        kt = k2[rows] * e_inv
        bk = [jnp.concatenate([bt[:, hs], kt[:, hs]], axis=0) for hs in hsl]
        ar = [jnp.concatenate([at[:, hs], rt[:, hs]], axis=0) for hs in hsl]
        gram = [_mm(bk[h], ar[h], passes, _NT) for h in heads]
        n_t = [jnp.where(upper_strict, gm[:chunk, :chunk], 0.0) for gm in gram]
        m2_t = [jnp.where(upper_strict, gm[chunk:, :chunk], 0.0) for gm in gram]
        m34_t = [jnp.concatenate([jnp.where(upper_incl, gm[:chunk, chunk:], 0.0),
                                  jnp.where(upper_incl, gm[chunk:, chunk:], 0.0)], axis=0) for gm in gram]
        vt_c = [vt[hs, rows] for hs in hsl]
        vm = [_mm(vt_c[h], m2_t[h], passes) for h in heads]
        t_inv = [eye + n for n in n_t]
        nk = n_t
        for _ in range(1, n_double):
            nk = [_mm(n, n, passes) for n in nk]
            t_inv = [t_inv[h] + _mm(t_inv[h], nk[h], passes) for h in heads]
        pre.append((rows, bk, ar, m34_t, vt_c, vm, t_inv, e_incl[chunk - 1:chunk, :]))

    state = [s_ref[h] for h in heads]
    for rows, bk, ar, m34_t, vt_c, vm, t_inv, p_end in pre:
        sa = [_mm(state[h], ar[h], passes, _NT) for h in heads]
        x = [_mm(sa[h][:, :chunk] + vm[h], t_inv[h], passes) for h in heads]
        uv = [jnp.concatenate([x[h], vt_c[h]], axis=1) for h in heads]
        for h in heads:
            yt_ref[hsl[h], rows] = sa[h][:, chunk:] + _mm(uv[h], m34_t[h], passes)
        state = [(state[h] + _mm(uv[h], bk[h], passes)) * p_end[:, hsl[h]] for h in heads]
    for h in heads:
        s_ref[h] = state[h]

    y = yt_ref[...].T
    hmean = hsum * (1.0 / HEAD_DIM)
    yc = y - _mm(y, hmean, 2)
    var = _mm(yc * yc, hmean, 2)
    yn = yc * lax.rsqrt(var + RWKV_GN_EPS) * lnw_ref[...] + lnb_ref[...]
    bonus = _mm(r * k2 * rk_ref[...], hsum, 2) * v
    y_ref[...] = (yn + bonus) * g
    wkv_ref[...] = s_ref[...]


def rwkv_mixer(u, shift0, wkv0, p, tt, chunk, passes):
    b, t_valid, n_in = u.shape
    n_heads = wkv0.shape[1]
    width = n_heads * HEAD_DIM
    if t_valid < tt:
        assert t_valid <= RWKV_MAX_CHUNK
        u = _pad_rows(u, tt)
        chunk = tt
    else:
        assert chunk <= RWKV_MAX_CHUNK
        t_valid = tt
    t = u.shape[1]
    assert t % tt == 0 and tt % chunk == 0
    fixed = lambda bi, ti: (0, 0)
    vec = lambda n: pl.BlockSpec((1, n), fixed)
    args = [p['rwkv_mu'].reshape(1, n_in), p['rwkv_w0'].reshape(1, width), p['rwkv_w_up'],
            p['rwkv_a0'].reshape(1, width), p['rwkv_a_up'], p['rwkv_g_up'], p['rwkv_k_k'].reshape(1, width),
            p['rwkv_k_a'].reshape(1, width), p['rwkv_r_k'].reshape(1, width), p['rwkv_ln_w'].reshape(1, width),
            p['rwkv_ln_b'].reshape(1, width), _head_block_matrix(width, 1.0)]
    specs = [vec(n_in), vec(width), pl.BlockSpec(p['rwkv_w_up'].shape, fixed), vec(width),
             pl.BlockSpec(p['rwkv_a_up'].shape, fixed), pl.BlockSpec(p['rwkv_g_up'].shape, fixed),
             vec(width), vec(width), vec(width), vec(width), vec(width), pl.BlockSpec((width, width), fixed)]
    y, wkv, shift = pl.pallas_call(
        functools.partial(_rwkv_kernel, tt=tt, chunk=chunk, width=width, passes=passes, t_valid=t_valid),
        grid=(b, t // tt),
        in_specs=[pl.BlockSpec((None, tt, n_in), lambda bi, ti: (bi, ti, 0)),
                  pl.BlockSpec((None, 1, n_in), lambda bi, ti: (bi, 0, 0)),
                  pl.BlockSpec((None, n_heads, HEAD_DIM, HEAD_DIM), lambda bi, ti: (bi, 0, 0, 0))] + specs,
        out_specs=[pl.BlockSpec((None, tt, width), lambda bi, ti: (bi, ti, 0)),
                   pl.BlockSpec((None, n_heads, HEAD_DIM, HEAD_DIM), lambda bi, ti: (bi, 0, 0, 0)),
                   pl.BlockSpec((None, 1, n_in), lambda bi, ti: (bi, 0, 0))],
        out_shape=[jax.ShapeDtypeStruct((b, t, width), F32),
                   jax.ShapeDtypeStruct((b, n_heads, HEAD_DIM, HEAD_DIM), F32),
                   jax.ShapeDtypeStruct((b, 1, n_in), F32)],
        scratch_shapes=[pltpu.VMEM((1, n_in), F32), pltpu.VMEM((n_heads, HEAD_DIM, HEAD_DIM), F32),
                        pltpu.VMEM((width, tt), F32)],
        compiler_params=_cparams("parallel", "arbitrary"),
        name="rwkv_mixer",
    )(u, shift0.reshape(b, 1, n_in), wkv0, *args)
    return y[:, :t_valid] if t_valid < tt else y, wkv, shift.reshape(b, n_in)


NEG_BIG = -1e30
_TN = (((0,), (0,)), ((), ()))
HEAD_PAIR = 2 * HEAD_DIM


def _proj_qkv_t_kernel(x_ref, g_ref, wt_ref, gq_ref, gk_ref, qt_ref, kt_ref, vt_ref, krow_ref, vtb_ref, kmean_ref,
                       *, width, q_scale, block):
    ti = pl.program_id(1)
    tm = x_ref.shape[0]
    n_heads = width // HEAD_DIM
    h = _rms(x_ref[...], g_ref[...]).astype(BF16)
    ut = _dg(wt_ref[...], h, _NT)

    def head_rms_t(xt, g_col):
        x3 = xt.reshape(n_heads, HEAD_DIM, tm)
        ms = jnp.mean(x3 * x3, axis=1, keepdims=True)
        return (x3 * lax.rsqrt(ms + EPS) * g_col[None, :, :]).reshape(width, tm)

    qt_ref[...] = head_rms_t(ut[:width], gq_ref[...]) * q_scale
    kt = head_rms_t(ut[width:2 * width], gk_ref[...])
    kt_ref[...] = kt
    vt = ut[2 * width:]
    vt_ref[...] = vt
    vtb_ref[...] = vt.astype(BF16)
    k_rows = kt.T
    krow_ref[...] = k_rows.astype(BF16)
    @pl.when(ti == 0)
    def _():
        kmean_ref[...] = jnp.zeros_like(kmean_ref)

    per_tile = tm // block
    blk = lax.broadcasted_iota(jnp.int32, kmean_ref.shape, 0)
    kmean = kmean_ref[...]
    for n in range(per_tile):
        mean = jnp.sum(k_rows[n * block:(n + 1) * block], axis=0, keepdims=True) * (1.0 / block)
        kmean = jnp.where(blk == ti * per_tile + n, mean, kmean)
    kmean_ref[...] = kmean


def proj_qkv_t(x, g, w_t, g_q, g_k, tm, q_scale, layer, depth, prev_kv):
    b, t, d = x.shape
    width = w_t.shape[0] // 3
    nb = t // MOBA_BLOCK
    assert t % tm == 0 and tm % MOBA_BLOCK == 0
    fixed = lambda bi, ti: (0, 0)
    tr = pl.BlockSpec((None, width, tm), lambda bi, ti: (bi, 0, ti))
    slab = pl.BlockSpec((None, None, width, tm), lambda bi, ti: (layer, bi, 0, ti))
    kernel_fn = functools.partial(_proj_qkv_t_kernel, width=width, q_scale=q_scale, block=MOBA_BLOCK)
    in_specs = [pl.BlockSpec((None, tm, d), lambda bi, ti: (bi, ti, 0)), pl.BlockSpec((1, d), fixed),
                pl.BlockSpec((3 * width, d), fixed), pl.BlockSpec((HEAD_DIM, 1), fixed),
                pl.BlockSpec((HEAD_DIM, 1), fixed)]
    args = [x, g.reshape(1, d), w_t, g_q.reshape(HEAD_DIM, 1), g_k.reshape(HEAD_DIM, 1)]
    aliases = {}
    if prev_kv is not None:
        n_in = len(args)
        args += list(prev_kv)
        in_specs += [pl.BlockSpec(memory_space=pl.ANY)] * 2
        aliases = {n_in: 1, n_in + 1: 2}
        body = kernel_fn
        kernel_fn = lambda *refs: body(*refs[:n_in], *refs[n_in + 2:])
    return pl.pallas_call(
        kernel_fn,
        grid=(b, t // tm),
        in_specs=in_specs,
        out_specs=[tr, slab, slab, pl.BlockSpec((None, tm, width), lambda bi, ti: (bi, ti, 0)), tr,
                   pl.BlockSpec((None, nb, width), lambda bi, ti: (bi, 0, 0))],
        out_shape=[jax.ShapeDtypeStruct((b, width, t), F32)] + [jax.ShapeDtypeStruct((depth, b, width, t), F32)] * 2
                  + [jax.ShapeDtypeStruct((b, t, width), BF16), jax.ShapeDtypeStruct((b, width, t), BF16),
                     jax.ShapeDtypeStruct((b, nb, width), F32)],
        input_output_aliases=aliases,
        compiler_params=_cparams("parallel", "arbitrary"),
        name="proj_qkv_t",
    )(*args)


def _head_pair_queries(qt_ref, h, tq):
    p0 = (h // 2) * HEAD_PAIR
    rows = lax.broadcasted_iota(jnp.int32, (HEAD_PAIR, tq), 0) // HEAD_DIM
    return jnp.where(rows == h % 2, qt_ref[p0:p0 + HEAD_PAIR, :], 0.0)


def _sb_prompt_kernel(qt_ref, k_ref, vt_ref, o_ref, qm_ref, acc_ref, run_ref, *, tq, width):
    qi = pl.program_id(1)
    n_heads = width // HEAD_DIM
    krow = lax.broadcasted_iota(jnp.int32, (tq, tq), 0)
    qcol = lax.broadcasted_iota(jnp.int32, (tq, tq), 1)
    from_here = jnp.where(qcol >= krow, 1.0, 0.0).astype(BF16)
    from_here = jnp.concatenate([from_here, from_here], axis=1)
    strict = krow < qcol
    for h in range(n_heads):
        qm_ref[h] = _head_pair_queries(qt_ref, h, tq).astype(BF16)
    acc_ref[...] = jnp.zeros_like(acc_ref)
    run_ref[...] = jnp.zeros_like(run_ref)

    heads = range(n_heads)
    hsl = [slice(h * HEAD_DIM, (h + 1) * HEAD_DIM) for h in heads]

    def block(j, diag):
        ks = pl.ds(pl.multiple_of(j * tq, tq), tq)
        run = run_ref[...]
        z = [_dg(k_ref[ks, (h // 2) * HEAD_PAIR:(h // 2 + 1) * HEAD_PAIR], qm_ref[h], _NN) for h in heads]
        l1m = [_log2_one_minus_sigmoid(zh) for zh in z]
        if diag:
            l1m = [jnp.where(strict, x, 0.0) for x in l1m]
        parts = [jnp.concatenate(_split_bf16(x), axis=0) for x in l1m]
        rest = [_dg(from_here, parts[h], _NN) + run[h:h + 1, :] for h in heads]
        w = [jnp.exp2(z[h] + rest[h]) for h in heads]
        if diag:
            w = [jnp.where(strict, x, 0.0) for x in w]
        pv = [_dg(vt_ref[hsl[h], ks], w[h].astype(BF16), _NN) for h in heads]
        for h in heads:
            acc_ref[hsl[h], :] += pv[h]
        run_ref[...] = jnp.concatenate([rest[h][0:1, :] for h in heads], axis=0)

    block(qi, True)

    def body(jj, carry):
        block(qi - 1 - jj, False)
        return carry

    lax.fori_loop(0, qi, body, 0)
    o_ref[...] = acc_ref[...]


def _attn_prompt_call(kernel_fn, name, qt, k_rows, vt_b, extra, extra_specs, scratch, tq):
    b, width, t = qt.shape
    n_heads = width // HEAD_DIM
    tile = pl.BlockSpec((None, width, tq), lambda bi, qi: (bi, 0, qi))
    return pl.pallas_call(
        kernel_fn,
        grid=(b, t // tq),
        in_specs=[tile, pl.BlockSpec((None, t, width), lambda bi, qi: (bi, 0, 0)),
                  pl.BlockSpec((None, width, t), lambda bi, qi: (bi, 0, 0))] + extra_specs,
        out_specs=tile,
        out_shape=jax.ShapeDtypeStruct((b, width, t), F32),
        scratch_shapes=[pltpu.VMEM((n_heads, HEAD_PAIR, tq), BF16), pltpu.VMEM((width, tq), F32)] + scratch,
        compiler_params=_cparams("parallel", "arbitrary"),
        name=name,
    )(qt, k_rows, vt_b, *extra)


def sb_prompt(qt, k_rows, vt_b, tq):
    b, width, t = qt.shape
    assert t % tq == 0
    n_heads = width // HEAD_DIM
    return _attn_prompt_call(functools.partial(_sb_prompt_kernel, tq=tq, width=width), "sb_prompt", qt, k_rows,
                             vt_b, [], [], [pltpu.VMEM((n_heads, tq), F32)], tq)


def _topk_block_mask(gate, n_valid, topk, axis):
    nb = gate.shape[axis]
    idx = lax.broadcasted_iota(jnp.int32, gate.shape, axis)
    rank = jnp.zeros(gate.shape, jnp.int32)
    for m in range(nb):
        gm = gate[:, m:m + 1] if axis == 1 else gate[m:m + 1, :]
        beats = (gm > gate) | ((gm == gate) & (m < idx))
        rank = rank + jnp.where(beats & (m < n_valid), 1, 0)
    return (idx < n_valid) & (rank < topk)


def _moba_prompt_kernel(qt_ref, k_ref, vt_ref, kmean_ref, o_ref, qm_ref, acc_ref, sel_ref, m_ref, l_ref,
                        *, tq, width):
    qi = pl.program_id(1)
    n_heads = width // HEAD_DIM
    krow = lax.broadcasted_iota(jnp.int32, (tq, tq), 0)
    qcol = lax.broadcasted_iota(jnp.int32, (tq, tq), 1)
    causal = krow <= qcol
    for h in range(n_heads):
        p0 = (h // 2) * HEAD_PAIR
        q_pair = _head_pair_queries(qt_ref, h, tq)
        qm_ref[h] = q_pair.astype(BF16)
        gate = _mm(kmean_ref[:, p0:p0 + HEAD_PAIR], q_pair, 3)
        sel_ref[h] = jnp.where(_topk_block_mask(gate, qi, MOBA_TOPK, 0), 0.0, NEG_BIG)
    acc_ref[...] = jnp.zeros_like(acc_ref)
    l_ref[...] = jnp.zeros_like(l_ref)
    m_ref[...] = jnp.full(m_ref.shape, NEG_BIG, F32)

    heads = range(n_heads)
    hsl = [slice(h * HEAD_DIM, (h + 1) * HEAD_DIM) for h in heads]

    def block(n, diag):
        ks = pl.ds(pl.multiple_of(n * tq, tq), tq)
        m_old = m_ref[...]
        l_old = l_ref[...]
        s = [_dg(k_ref[ks, (h // 2) * HEAD_PAIR:(h // 2 + 1) * HEAD_PAIR], qm_ref[h], _NN) for h in heads]
        if diag:
            s = [jnp.where(causal, x, NEG_BIG) for x in s]
        else:
            s = [s[h] + sel_ref[h, pl.ds(n, 1), :] for h in heads]
        m_new = [jnp.maximum(m_old[h:h + 1, :], jnp.max(s[h], axis=0, keepdims=True)) for h in heads]
        p = [jnp.exp(s[h] - m_new[h]) for h in heads]
        alpha = [jnp.exp(m_old[h:h + 1, :] - m_new[h]) for h in heads]
        l_new = [alpha[h] * l_old[h:h + 1, :] + jnp.sum(p[h], axis=0, keepdims=True) for h in heads]
        pv = [_dg(vt_ref[hsl[h], ks], p[h].astype(BF16), _NN) for h in heads]
        for h in heads:
            acc_ref[hsl[h], :] = alpha[h] * acc_ref[hsl[h], :] + pv[h]
        m_ref[...] = jnp.concatenate(m_new, axis=0)
        l_ref[...] = jnp.concatenate(l_new, axis=0)

    block(qi, True)

    def body(n, carry):
        block(n, False)
        return carry

    lax.fori_loop(0, qi, body, 0)
    for h in range(n_heads):
        hs = slice(h * HEAD_DIM, (h + 1) * HEAD_DIM)
        o_ref[hs, :] = acc_ref[hs, :] / l_ref[h:h + 1, :]


def moba_prompt(qt, k_rows, vt_b, kmean):
    b, width, t = qt.shape
    tq = MOBA_BLOCK
    nb = t // tq
    assert t % tq == 0
    n_heads = width // HEAD_DIM
    return _attn_prompt_call(functools.partial(_moba_prompt_kernel, tq=tq, width=width), "moba_prompt", qt, k_rows,
                             vt_b, [kmean], [pl.BlockSpec((None, nb, width), lambda bi, qi: (bi, 0, 0))],
                             [pltpu.VMEM((n_heads, nb, tq), F32), pltpu.VMEM((n_heads, tq), F32),
                              pltpu.VMEM((n_heads, tq), F32)], tq)


def _stack_heads(q, n_heads):
    t, width = q.shape
    rows = lax.broadcasted_iota(jnp.int32, (n_heads * t, width), 0) // t
    lanes = lax.broadcasted_iota(jnp.int32, (n_heads * t, width), 1) // HEAD_DIM
    return jnp.where(rows == lanes, jnp.concatenate([q] * n_heads, axis=0), 0.0)


def _unstack_heads(acc, n_heads):
    rows_total, width = acc.shape
    t = rows_total // n_heads
    rows = lax.broadcasted_iota(jnp.int32, (rows_total, width), 0) // t
    lanes = lax.broadcasted_iota(jnp.int32, (rows_total, width), 1) // HEAD_DIM
    kept = jnp.where(rows == lanes, acc, 0.0).reshape(n_heads, t, width)
    return jnp.sum(kept, axis=0)


def _from_here_matrix(tk):
    kj = lax.broadcasted_iota(jnp.int32, (2 * tk, tk), 0) % tk
    ks = lax.broadcasted_iota(jnp.int32, (2 * tk, tk), 1)
    return jnp.where(kj >= ks, 1.0, 0.0).astype(BF16)


def _sum_from_here(x, from_here):
    return _dg(jnp.concatenate(_split_bf16(x), axis=1), from_here, _NN)


def _sb_sample_kernel(pt_ref, q_ref, knew_ref, vnew_ref, *refs, n_pages_step, t_new, n_heads):
    k_refs = refs[:n_pages_step]
    v_refs = refs[n_pages_step:2 * n_pages_step]
    o_ref, qs_ref, acc_ref, run_ref = refs[2 * n_pages_step:]
    step = pl.program_id(1)
    tk = knew_ref.shape[0]
    from_here = _from_here_matrix(tk)

    def tile(k_tile, v_tile, mask, paged):
        z = _dg(qs_ref[...], k_tile.astype(BF16), _NN if paged else _NT)
        l1m = _log2_one_minus_sigmoid(z)
        if mask is not None:
            l1m = jnp.where(mask, l1m, 0.0)
        rest = _sum_from_here(l1m, from_here) + run_ref[...]
        w = jnp.exp2(z + rest)
        if mask is not None:
            w = jnp.where(mask, w, 0.0)
        acc_ref[...] += _mm(w, v_tile, 1, _NT if paged else _NN)
        run_ref[...] = rest[:, 0:1]

    @pl.when(step == 0)
    def _():
        qs_ref[...] = _stack_heads(q_ref[...], n_heads).astype(BF16)
        acc_ref[...] = jnp.zeros_like(acc_ref)
        run_ref[...] = jnp.zeros_like(run_ref)
        rows = lax.broadcasted_iota(jnp.int32, (n_heads * t_new, tk), 0) % t_new
        cols = lax.broadcasted_iota(jnp.int32, (n_heads * t_new, tk), 1)
        tile(knew_ref[...], vnew_ref[...], cols < rows, False)

    qs = qs_ref[...]
    z = [_dg(qs, k_refs[i][...].astype(BF16), _NN) for i in range(n_pages_step)]
    cum = [_sum_from_here(_log2_one_minus_sigmoid(x), from_here) for x in z]
    run = run_ref[...]
    acc = acc_ref[...]
    for i in range(n_pages_step):
        w = jnp.exp2(z[i] + cum[i] + run)
        acc = acc + _mm(w, v_refs[i][...], 1, _NT)
        run = run + cum[i][:, 0:1]
    acc_ref[...] = acc
    run_ref[...] = run

    @pl.when(step == pl.num_programs(1) - 1)
    def _():
        o_ref[...] = _unstack_heads(acc_ref[...], n_heads)


def _page_specs(n_pages, n_pages_step, width, page_size, reverse):
    def spec(i):
        def index(bi, si, pt):
            p = si * n_pages_step + i
            return (pt[bi, n_pages - 1 - p if reverse else p], 0, 0)
        return pl.BlockSpec((None, width, page_size), index)
    return [spec(i) for i in range(n_pages_step)]


def _pad_rows(x, rows):
    return jnp.pad(x, ((0, 0), (0, rows - x.shape[1]), (0, 0)))


def sb_sample(q, k_new, v_new, k_pool, v_pool, pages, n_pages_step):
    b, t_new, width = q.shape
    n_heads = width // HEAD_DIM
    page_size = k_pool.shape[2]
    n_pages = pages.shape[1]
    assert n_pages % n_pages_step == 0 and t_new <= page_size
    new = pl.BlockSpec((None, t_new, width), lambda bi, si, pt: (bi, 0, 0))
    new_pad = pl.BlockSpec((None, page_size, width), lambda bi, si, pt: (bi, 0, 0))
    page_specs = _page_specs(n_pages, n_pages_step, width, page_size, reverse=True)
    return pl.pallas_call(
        functools.partial(_sb_sample_kernel, n_pages_step=n_pages_step, t_new=t_new, n_heads=n_heads),
        grid_spec=pltpu.PrefetchScalarGridSpec(
            num_scalar_prefetch=1, grid=(b, n_pages // n_pages_step),
            in_specs=[new, new_pad, new_pad] + page_specs + page_specs,
            out_specs=new,
            scratch_shapes=[pltpu.VMEM((n_heads * t_new, width), BF16),
                            pltpu.VMEM((n_heads * t_new, width), F32),
                            pltpu.VMEM((n_heads * t_new, 1), F32)]),
        out_shape=jax.ShapeDtypeStruct((b, t_new, width), F32),
        compiler_params=_cparams("parallel", "arbitrary"),
        name="sb_sample",
    )(pages, q, _pad_rows(k_new, page_size), _pad_rows(v_new, page_size),
      *([k_pool] * n_pages_step), *([v_pool] * n_pages_step))


def _moba_sample_kernel(pt_ref, q_ref, knew_ref, vnew_ref, *refs, n_pages_step, pages_per_block, t_new, n_heads,
                        key_steps):
    k_refs = refs[:n_pages_step]
    v_refs = refs[n_pages_step:2 * n_pages_step]
    o_ref, qs_ref, kb_ref, kmean_ref, sel_ref, acc_ref, m_ref, l_ref = refs[2 * n_pages_step:]
    step = pl.program_id(1)
    nb = kmean_ref.shape[1]
    rows_total = n_heads * t_new
    page_size = k_refs[0].shape[1]

    def tile(k_tile, v_tile, mask, paged):
        s = jnp.where(mask, _dg(qs_ref[...], k_tile.astype(BF16), _NN if paged else _NT), NEG_BIG)
        m_new = jnp.maximum(m_ref[...], jnp.max(s, axis=1, keepdims=True))
        p = jnp.exp(s - m_new)
        alpha = jnp.exp(m_ref[...] - m_new)
        l_ref[...] = alpha * l_ref[...] + jnp.sum(p, axis=1, keepdims=True)
        acc_ref[...] = alpha * acc_ref[...] + _mm(p, v_tile, 1, _NT if paged else _NN)
        m_ref[...] = m_new

    @pl.when(step == 0)
    def _():
        qs_ref[...] = _stack_heads(q_ref[...], n_heads).astype(BF16)
        kmean_ref[...] = jnp.zeros_like(kmean_ref)
        acc_ref[...] = jnp.zeros_like(acc_ref)
        l_ref[...] = jnp.zeros_like(l_ref)
        m_ref[...] = jnp.full(m_ref.shape, NEG_BIG, F32)
        tk = knew_ref.shape[0]
        rows = lax.broadcasted_iota(jnp.int32, (rows_total, tk), 0) % t_new
        cols = lax.broadcasted_iota(jnp.int32, (rows_total, tk), 1)
        tile(knew_ref[...], vnew_ref[...], cols <= rows, False)

    @pl.when(step < key_steps)
    def _():
        blk_lane = lax.broadcasted_iota(jnp.int32, kmean_ref.shape, 1)
        kmean = kmean_ref[...]
        for n in range(n_pages_step // pages_per_block):
            tot = None
            for i in range(pages_per_block):
                page = n * pages_per_block + i
                k_page = k_refs[page][...]
                kb_ref[step * n_pages_step + page] = k_page.astype(BF16)
                tot = k_page if tot is None else tot + k_page
            mean = jnp.sum(tot, axis=1, keepdims=True) * (1.0 / MOBA_BLOCK)
            kmean = jnp.where(blk_lane == step * (n_pages_step // pages_per_block) + n, mean, kmean)
        kmean_ref[...] = kmean

    @pl.when(step == key_steps)
    def _():
        gate = _mm(_stack_heads(q_ref[...], n_heads), kmean_ref[...], 3)
        sel_ref[...] = jnp.where(_topk_block_mask(gate, nb, MOBA_TOPK, 1), 1.0, 0.0)

    @pl.when(step >= key_steps)
    def _():
        first = (step - key_steps) * n_pages_step
        qs = qs_ref[...]
        n_keys = n_pages_step * page_size
        key_blk = (first * page_size + lax.broadcasted_iota(jnp.int32, (nb, n_keys), 1)) // MOBA_BLOCK
        expand = jnp.where(key_blk == lax.broadcasted_iota(jnp.int32, (nb, n_keys), 0), 1.0, 0.0).astype(BF16)
        picked = _dg(sel_ref[...].astype(BF16), expand, _NN) > 0.5
        mask = [picked[:, i * page_size:(i + 1) * page_size] for i in range(n_pages_step)]
        s = [jnp.where(mask[i], _dg(qs, kb_ref[first + i], _NN), NEG_BIG) for i in range(n_pages_step)]
        s_max = s[0]
        for x in s[1:]:
            s_max = jnp.maximum(s_max, x)
        m_old = m_ref[...]
        m_new = jnp.maximum(m_old, jnp.max(s_max, axis=1, keepdims=True))
        p = [jnp.exp(x - m_new) for x in s]
        alpha = jnp.exp(m_old - m_new)
        p_sum = p[0]
        acc = alpha * acc_ref[...] + _mm(p[0], v_refs[0][...], 1, _NT)
        for i in range(1, n_pages_step):
            p_sum = p_sum + p[i]
            acc = acc + _mm(p[i], v_refs[i][...], 1, _NT)
        l_ref[...] = alpha * l_ref[...] + jnp.sum(p_sum, axis=1, keepdims=True)
        acc_ref[...] = acc
        m_ref[...] = m_new

    @pl.when(step == pl.num_programs(1) - 1)
    def _():
        o_ref[...] = _unstack_heads(acc_ref[...] / l_ref[...], n_heads)


def moba_sample(q, k_new, v_new, k_pool, v_pool, pages, n_pages_step):
    b, t_new, width = q.shape
    n_heads = width // HEAD_DIM
    page_size = k_pool.shape[2]
    n_pages = pages.shape[1]
    pages_per_block = MOBA_BLOCK // page_size
    nb = n_pages // pages_per_block
    key_steps = n_pages // n_pages_step
    assert n_pages % n_pages_step == 0 and n_pages_step % pages_per_block == 0 and t_new <= page_size
    assert (n_pages * page_size) % MOBA_BLOCK == 0 and t_new <= MOBA_BLOCK
    new = pl.BlockSpec((None, t_new, width), lambda bi, si, pt: (bi, 0, 0))
    new_pad = pl.BlockSpec((None, page_size, width), lambda bi, si, pt: (bi, 0, 0))

    def page_spec(i, keys):
        def index(bi, si, pt):
            s = jnp.minimum(si, key_steps - 1) if keys else jnp.maximum(si - key_steps, 0)
            return (pt[bi, s * n_pages_step + i], 0, 0)
        return pl.BlockSpec((None, width, page_size), index)

    rows_total = n_heads * t_new
    return pl.pallas_call(
        functools.partial(_moba_sample_kernel, n_pages_step=n_pages_step, pages_per_block=pages_per_block,
                          t_new=t_new, n_heads=n_heads, key_steps=key_steps),
        grid_spec=pltpu.PrefetchScalarGridSpec(
            num_scalar_prefetch=1, grid=(b, 2 * key_steps),
            in_specs=[new, new_pad, new_pad] + [page_spec(i, True) for i in range(n_pages_step)]
                     + [page_spec(i, False) for i in range(n_pages_step)],
            out_specs=new,
            scratch_shapes=[pltpu.VMEM((rows_total, width), BF16), pltpu.VMEM((n_pages, width, page_size), BF16),
                            pltpu.VMEM((width, nb), F32), pltpu.VMEM((rows_total, nb), F32),
                            pltpu.VMEM((rows_total, width), F32), pltpu.VMEM((rows_total, 1), F32),
                            pltpu.VMEM((rows_total, 1), F32)]),
        out_shape=jax.ShapeDtypeStruct((b, t_new, width), F32),
        compiler_params=_cparams("parallel", "arbitrary"),
        name="moba_sample",
    )(pages, q, _pad_rows(k_new, page_size), _pad_rows(v_new, page_size),
      *([k_pool] * n_pages_step), *([v_pool] * n_pages_step))


def _merge_kernel(x_ref, g_ref, wg_ref, y0_ref, y1_ref, y2_ref, y3_ref, wb_ref, wo_ref, o_ref, *, d_model, layouts):
    x = x_ref[...]
    h = _rms(x, g_ref[...]).astype(BF16)
    merged = None
    for n, (y_ref, layout) in enumerate(zip((y0_ref, y1_ref, y2_ref, y3_ref), layouts)):
        gate = jax.nn.sigmoid(_dg(h, wg_ref[:, n * d_model:(n + 1) * d_model], _NN))
        term = gate * _mm(y_ref[...], wb_ref[n], 1, _TN if layout == "transposed" else _NN)
        merged = term if merged is None else merged + term
    o_ref[...] = x + _mm(merged, wo_ref[...])


def merge(x, g, w_gate, ys, layouts, w_branch, w_out, tm):
    b, t, d = x.shape
    width = w_branch.shape[1]
    tm = min(tm, t)
    assert t % tm == 0
    fixed = lambda bi, ti: (0, 0)
    seq = lambda bi, ti: (bi, ti, 0)
    y_spec = {"rows": pl.BlockSpec((None, tm, width), seq),
              "time_major": pl.BlockSpec((tm, width), lambda bi, ti: (ti, bi)),
              "transposed": pl.BlockSpec((None, width, tm), lambda bi, ti: (bi, 0, ti))}
    return pl.pallas_call(
        functools.partial(_merge_kernel, d_model=d, layouts=tuple(layouts)),
        grid=(b, t // tm),
        in_specs=[pl.BlockSpec((None, tm, d), seq), pl.BlockSpec((1, d), fixed), _resident((d, N_BRANCH * d))]
                 + [y_spec[k] for k in layouts]
                 + [_resident((N_BRANCH, width, d)), _resident((d, d))],
        out_specs=pl.BlockSpec((None, tm, d), seq),
        out_shape=jax.ShapeDtypeStruct((b, t, d), F32),
        compiler_params=_cparams("parallel", "parallel"),
        name="merge",
    )(x, g.reshape(1, d), w_gate, *ys, w_branch, w_out)


def _ffn_kernel(x_ref, g_ref, wup_ref, cw_ref, cb_ref, wdn_ref, conv0_ref, o_ref, conv_ref, st_ref,
                *, nseq, tt, d_ff, col_chunk):
    ti = pl.program_id(1)

    @pl.when(ti == 0)
    def _():
        st_ref[...] = conv0_ref[...]

    d_model = x_ref.shape[-1]
    x = x_ref[...].reshape(nseq * tt, d_model)
    h = _rms(x, g_ref[...]).astype(BF16)
    t_idx = lax.broadcasted_iota(jnp.int32, (1, tt, 1), 1)

    def conv_cols(c0):
        cols = slice(c0, c0 + col_chunk)
        up = _dg(h, wup_ref[:, cols], _NN).reshape(nseq, tt, col_chunk)
        st = st_ref[:, :, cols]
        p1 = jnp.where(t_idx == 0, st[:, 1:2, :], pltpu.roll(up, 1, axis=1))
        p2 = jnp.where(t_idx == 0, st[:, 0:1, :], jnp.where(t_idx == 1, st[:, 1:2, :], pltpu.roll(up, 2, axis=1)))
        st_ref[:, :, cols] = up[:, tt - 2:, :]
        cw = cw_ref[:, cols]
        c = cb_ref[:, cols] + cw[2:3, :] * up + cw[1:2, :] * p1 + cw[0:1, :] * p2
        return c.reshape(nseq * tt, col_chunk)

    acc = x
    for c0 in range(0, d_ff, col_chunk):
        a = conv_cols(c0)
        b = conv_cols(d_ff + c0)
        acc = acc + _mm(a * jax.nn.sigmoid(a) * b, wdn_ref[c0:c0 + col_chunk, :])
    o_ref[...] = acc.reshape(nseq, tt, d_model)
    conv_ref[...] = st_ref[...]


def conv_ffn(x, g, w_up, conv_w, conv_b, w_down, conv0, nseq, tt, col_chunk):
    b, t, d = x.shape
    d_ff = w_down.shape[0]
    assert b % nseq == 0 and t % tt == 0 and d_ff % col_chunk == 0 and tt >= CONV_W - 1
    fixed = lambda bi, ti: (0, 0)
    seq = lambda bi, ti: (bi, ti, 0)
    st = lambda bi, ti: (bi, 0, 0)
    return pl.pallas_call(
        functools.partial(_ffn_kernel, nseq=nseq, tt=tt, d_ff=d_ff, col_chunk=col_chunk),
        grid=(b // nseq, t // tt),
        in_specs=[pl.BlockSpec((nseq, tt, d), seq), pl.BlockSpec((1, d), fixed),
                  _resident((d, 2 * d_ff)), pl.BlockSpec((CONV_W, 2 * d_ff), fixed),
                  pl.BlockSpec((1, 2 * d_ff), fixed), _resident((d_ff, d)),
                  pl.BlockSpec((nseq, CONV_W - 1, 2 * d_ff), st)],
        out_specs=[pl.BlockSpec((nseq, tt, d), seq), pl.BlockSpec((nseq, CONV_W - 1, 2 * d_ff), st)],
        out_shape=[jax.ShapeDtypeStruct((b, t, d), F32), jax.ShapeDtypeStruct((b, CONV_W - 1, 2 * d_ff), F32)],
        scratch_shapes=[pltpu.VMEM((nseq, CONV_W - 1, 2 * d_ff), F32)],
        compiler_params=_cparams("parallel", "arbitrary"),
        name="conv_ffn",
    )(x, g.reshape(1, d), w_up, conv_w, conv_b.reshape(1, 2 * d_ff), w_down, conv0)


MOBA_Q_SCALE = HEAD_DIM ** -0.5
SB_Q_SCALE = HEAD_DIM ** -0.5 * LOG2_E

PROMPT_TILES = dict(proj_tm=1024, qkv_tm=512, s5_tt=128, s5_passes=1, rwkv_tt=256, rwkv_chunk=128, rwkv_passes=1,
                    sb_tq=256, merge_tm=512, ffn_tt=512, ffn_cols=1408)
SAMPLE_TILES = dict(proj_tm=256, s5_tt=8, s5_passes=3, rwkv_tt=128, rwkv_chunk=64, rwkv_passes=1,
                    merge_tm=256, ffn_cols=1408, sb_pages=32, moba_pages=32)


def _layer(x, st, past, p, tiles):
    b, t, d = x.shape
    width = d // 2
    n_heads = width // HEAD_DIM
    n_rwkv = 3 * width + RWKV_DECAY_LORA + RWKV_ICLR_LORA + RWKV_GATE_LORA
    prompt = past is None
    n_state = p['s5_ab_re'].shape[0]
    s5_args = (st['s5_re'].reshape(b, n_state), st['s5_im'].reshape(b, n_state), p['s5_ab_re'], p['s5_ab_im'],
               p['s5_bb'][tiles['s5_passes']], p['s5_cc'][tiles['s5_passes']], p['s5_d'], p['s5_w_glu'],
               tiles['s5_tt'], tiles['s5_passes'])

    if prompt:
        u_s5, u_rw = proj_plain(x, p['norm1_g'], p['w_mix'], (width, n_rwkv), tiles['proj_tm'], True)
        y_s5, s5_re, s5_im = s5_mixer(u_s5.reshape(t, b, width), *s5_args)
        y_s5 = y_s5.reshape(t, b * width)
        qt, k_sb, v_sb, k_rows, vt_b, _ = proj_qkv_t(x, p['norm1_g'], p['w_sb'].T, p['sb_q_g'], p['sb_k_g'],
                                                     tiles['qkv_tm'], SB_Q_SCALE, p['layer'], p['depth'],
                                                     st['kv_sb'])
        y_sb = sb_prompt(qt, k_rows, vt_b, tiles['sb_tq'])
        qt, k_mb, v_mb, k_rows, vt_b, kmean = proj_qkv_t(x, p['norm1_g'], p['w_mb'].T, p['moba_q_g'],
                                                         p['moba_k_g'], tiles['qkv_tm'], MOBA_Q_SCALE,
                                                         p['layer'], p['depth'], st['kv_mb'])
        y_mb = moba_prompt(qt, k_rows, vt_b, kmean)
        layouts = ("time_major", "rows", "transposed", "transposed")
        x_m = x
        kv = lambda a: a
    else:
        x_m = x.reshape(1, b * t, d)
        u_s5, u_rw = proj_plain(x_m, p['norm1_g'], p['w_mix'], (width, n_rwkv), b * t, False)
        y_s5, s5_re, s5_im = s5_mixer(u_s5.reshape(b, t, width).transpose(1, 0, 2), *s5_args)
        y_s5 = y_s5.transpose(1, 0, 2).reshape(1, b * t, width)
        x2 = x.reshape(b * t, d)
        seq = lambda a: a.reshape(b, t, width)
        q_sb, k_sb, v_sb = proj_qkv(x2, p['norm1_g'], p['w_sb'], p['sb_q_g'], p['sb_k_g'], b * t,
                                    SB_Q_SCALE)
        q_mb, k_mb, v_mb = proj_qkv(x2, p['norm1_g'], p['w_mb'], p['moba_q_g'], p['moba_k_g'], b * t,
                                    MOBA_Q_SCALE)
        y_sb = sb_sample(seq(q_sb), seq(k_sb), seq(v_sb), past['sb_k'], past['sb_v'], past['pages'],
                         tiles['sb_pages'])
        y_mb = moba_sample(seq(q_mb), seq(k_mb), seq(v_mb), past['moba_k'], past['moba_v'], past['pages'],
                           tiles['moba_pages'])
        y_sb, y_mb = y_sb.reshape(1, b * t, width), y_mb.reshape(1, b * t, width)
        layouts = ("rows",) * N_BRANCH
        kv = lambda a: a.reshape(b, t, n_heads, HEAD_DIM)

    y_rw, wkv, shift = rwkv_mixer(u_rw.reshape(b, t, n_rwkv), st['shift'], st['wkv'], p, tiles['rwkv_tt'],
                                  tiles['rwkv_chunk'], tiles['rwkv_passes'])
    ys = [y_s5, y_rw.reshape(x_m.shape[0], x_m.shape[1], width), y_sb, y_mb]
    x1 = merge(x_m, p['norm1_g'], p['w_gate'], ys, layouts, p['w_branch'], p['w_out'], tiles['merge_tm'])
    nseq, tt = (1, tiles['ffn_tt']) if prompt else (b, t)
    x_out, conv = conv_ffn(x1.reshape(b, t, d), p['norm2_g'], p['ffn_w_up'], p['ffn_conv_w'], p['ffn_conv_b'],
                           p['ffn_w_down'], st['conv'], nseq, tt, tiles['ffn_cols'])
    new_st = (s5_re.reshape(st['s5_re'].shape), s5_im.reshape(st['s5_im'].shape), wkv, shift, conv,
              kv(k_sb), kv(v_sb), kv(k_mb), kv(v_mb))
    return x_out, new_st


def kernel(x_prompt, x_sample, state_s5_re, state_s5_im, state_rwkv_wkv, state_rwkv_shift, state_ffn_conv, cache_sb_k, cache_sb_v, cache_moba_k, cache_moba_v, page_table, norm1_g, w_in, s5_a_re, s5_a_im, s5_log_dt, s5_b_re, s5_b_im, s5_c_re, s5_c_im, s5_d, s5_w_glu, rwkv_mu, rwkv_w0, rwkv_w_up, rwkv_a0, rwkv_a_up, rwkv_g_up, rwkv_k_k, rwkv_k_a, rwkv_r_k, rwkv_ln_w, rwkv_ln_b, sb_q_g, sb_k_g, moba_q_g, moba_k_g, w_branch, w_out, norm2_g, ffn_w_up, ffn_conv_w, ffn_conv_b, ffn_w_down):
    depth = w_in.shape[0]
    bp = x_prompt.shape[0]
    n_pool, page_size, n_heads, head_dim = cache_sb_k.shape[1:]
    width = n_heads * head_dim
    pool = lambda c: c.transpose(0, 1, 3, 4, 2).reshape(depth * n_pool, width, page_size)
    pools = dict(sb_k=pool(cache_sb_k), sb_v=pool(cache_sb_v), moba_k=pool(cache_moba_k), moba_v=pool(cache_moba_v))
    g_s5, p_s5 = s5_a_re.shape[1:]
    n_rwkv = state_rwkv_shift.shape[-1]
    d_ff2 = state_ffn_conv.shape[-1]
    zero_st = dict(s5_re=jnp.zeros((bp, g_s5, p_s5), F32), s5_im=jnp.zeros((bp, g_s5, p_s5), F32),
                   shift=jnp.zeros((bp, n_rwkv), F32), wkv=jnp.zeros((bp, n_heads, head_dim, head_dim), F32),
                   conv=jnp.zeros((bp, CONV_W - 1, d_ff2), F32))
    yp, ys = x_prompt, x_sample
    states_p, states_s = [], []
    kv_sb = kv_mb = None
    for l in range(depth):
        ab_re, ab_im, bb, cc = s5_operands(s5_a_re[l], s5_a_im[l], s5_log_dt[l], s5_b_re[l], s5_b_im[l],
                                           s5_c_re[l], s5_c_im[l])
        width_l = w_in.shape[1] // 2
        off_sb = width_l + state_rwkv_shift.shape[-1]
        off_mb, off_gate = off_sb + 3 * width_l, off_sb + 6 * width_l
        w_piece = lambda a, b: w_in[l, :, a:b].astype(BF16)
        p = dict(layer=l, depth=depth, norm1_g=norm1_g[l], w_mix=w_piece(0, off_sb), w_sb=w_piece(off_sb, off_mb),
                 w_mb=w_piece(off_mb, off_gate), w_gate=w_piece(off_gate, w_in.shape[2]),
                 s5_ab_re=ab_re, s5_ab_im=ab_im,
                 s5_bb={1: bb.astype(BF16), 3: bb}, s5_cc={1: cc.astype(BF16), 3: cc}, s5_d=s5_d[l],
                 s5_w_glu=s5_w_glu[l].astype(BF16), rwkv_mu=rwkv_mu[l], rwkv_w0=rwkv_w0[l],
                 rwkv_w_up=rwkv_w_up[l], rwkv_a0=rwkv_a0[l], rwkv_a_up=rwkv_a_up[l], rwkv_g_up=rwkv_g_up[l],
                 rwkv_k_k=rwkv_k_k[l], rwkv_k_a=rwkv_k_a[l], rwkv_r_k=rwkv_r_k[l].reshape(-1),
                 rwkv_ln_w=rwkv_ln_w[l], rwkv_ln_b=rwkv_ln_b[l], sb_q_g=sb_q_g[l], sb_k_g=sb_k_g[l],
                 moba_q_g=moba_q_g[l], moba_k_g=moba_k_g[l], w_branch=w_branch[l].astype(BF16),
                 w_out=w_out[l].astype(BF16), norm2_g=norm2_g[l], ffn_w_up=ffn_w_up[l].astype(BF16),
                 ffn_conv_w=ffn_conv_w[l], ffn_conv_b=ffn_conv_b[l], ffn_w_down=ffn_w_down[l].astype(BF16))
        yp, st_p = _layer(yp, dict(zero_st, kv_sb=kv_sb, kv_mb=kv_mb), None, p, PROMPT_TILES)
        kv_sb, kv_mb = st_p[5:7], st_p[7:9]
        st_s = dict(s5_re=state_s5_re[l], s5_im=state_s5_im[l], shift=state_rwkv_shift[l], wkv=state_rwkv_wkv[l],
                    conv=state_ffn_conv[l])
        past = dict(pools, pages=page_table + l * n_pool)
        ys, st_s = _layer(ys, st_s, past, p, SAMPLE_TILES)
        states_p.append(st_p[:5])
        states_s.append(st_s)
    t_p = x_prompt.shape[1]
    kv_out = lambda a: a.reshape(depth, bp, n_heads, head_dim, t_p).transpose(0, 1, 4, 2, 3)
    stacked_p = [jnp.stack(z, axis=0) for z in zip(*states_p)] + [kv_out(a) for a in (*kv_sb, *kv_mb)]
    stacked_s = [jnp.stack(z, axis=0) for z in zip(*states_s)]
    out = [yp, ys]
    for a, c in zip(stacked_p, stacked_s):
        out += [a, c]
    return tuple(out)
```

```python
import functools
import math

import jax
import jax.numpy as jnp
from jax import lax
from jax.experimental import pallas as pl
from jax.experimental.pallas import tpu as pltpu

F32 = jnp.float32
BF16 = jnp.bfloat16

HEAD_DIM = 64
S5_GROUP = 16
S5_STATE = 64
RWKV_DECAY_LORA = 64
RWKV_ICLR_LORA = 64
RWKV_GATE_LORA = 128
N_BRANCH = 4
MOBA_BLOCK = 256
MOBA_TOPK = 3
CONV_W = 3
EPS = 1e-6
RWKV_GN_EPS = 64e-5
RWKV_MAX_CHUNK = 128

V7X_VMEM_LIMIT_BYTES = 56 * 1024 * 1024
SUBLANES = 8
LANES = 128


def _resident(shape):
    zeros = (0,) * len(shape)
    return pl.BlockSpec(shape, lambda *_: zeros, pipeline_mode=pl.Buffered(1))


def _cparams(*sem):
    return pltpu.CompilerParams(dimension_semantics=sem, vmem_limit_bytes=V7X_VMEM_LIMIT_BYTES)


def _split_bf16(x):
    hi = x.astype(BF16)
    lo = (x - hi.astype(F32)).astype(BF16)
    return hi, lo


_NN = (((1,), (0,)), ((), ()))
_NT = (((1,), (1,)), ((), ()))


def _dg(a, b, dims):
    return lax.dot_general(a, b, dims, preferred_element_type=F32)


def _mm(a, b, passes=1, dims=_NN):
    if passes == 1:
        return _dg(a.astype(BF16), b.astype(BF16), dims)
    ah, al = _split_bf16(a)
    if b.dtype == BF16:
        return _dg(ah, b, dims) + _dg(al, b, dims)
    bh, bl = _split_bf16(b)
    if passes == 2:
        return _dg(ah, bh, dims) + _dg(al, bh, dims)
    return _dg(ah, bh, dims) + (_dg(al, bh, dims) + _dg(ah, bl, dims))


def _head_block_matrix(width, value):
    r = jnp.arange(width) // HEAD_DIM
    return jnp.where(r[:, None] == r[None, :], value, 0.0).astype(BF16)


def _rms(x, g):
    return x * lax.rsqrt(jnp.mean(x * x, axis=-1, keepdims=True) + EPS) * g


def _proj_plain_kernel(x_ref, g_ref, w_ref, *out_refs, widths):
    h = _rms(x_ref[...], g_ref[...]).astype(BF16)
    u = _dg(h, w_ref[...], _NN)
    off = 0
    for o_ref, n in zip(out_refs, widths):
        o_ref[...] = u[:, off:off + n]
        off += n


def proj_plain(x, g, w, widths, tm, first_time_major):
    b, t, d = x.shape
    n = w.shape[1]
    assert sum(widths) == n and t % tm == 0
    fixed = lambda bi, ti: (0, 0)
    seq = lambda bi, ti: (bi, ti, 0)
    out_specs = [pl.BlockSpec((None, tm, k), seq) for k in widths]
    out_shape = [jax.ShapeDtypeStruct((b, t, k), F32) for k in widths]
    if first_time_major:
        out_specs[0] = pl.BlockSpec((tm, widths[0]), lambda bi, ti: (ti, bi))
        out_shape[0] = jax.ShapeDtypeStruct((t, b * widths[0]), F32)
    return pl.pallas_call(
        functools.partial(_proj_plain_kernel, widths=widths),
        grid=(b, t // tm),
        in_specs=[pl.BlockSpec((None, tm, d), seq), pl.BlockSpec((1, d), fixed), _resident((d, n))],
        out_specs=out_specs,
        out_shape=out_shape,
        compiler_params=_cparams("parallel", "parallel"),
        name="proj_plain",
    )(x, g.reshape(1, d), w)


def _head_rms(x, hm, g):
    ms = _mm(x * x, hm, passes=2)
    return x * lax.rsqrt(ms + EPS) * g


def _proj_qkv_kernel(x_ref, g_ref, w_ref, hm_ref, gq_ref, gk_ref, q_ref, k_ref, v_ref, *, width, q_scale):
    h = _rms(x_ref[...], g_ref[...]).astype(BF16)
    u = _dg(h, w_ref[...], _NN)
    hm = hm_ref[...]
    q_ref[...] = _head_rms(u[:, :width], hm, gq_ref[...]) * q_scale
    k_ref[...] = _head_rms(u[:, width:2 * width], hm, gk_ref[...])
    v_ref[...] = u[:, 2 * width:]


def proj_qkv(x2d, g, w, g_q, g_k, tm, q_scale):
    m, d = x2d.shape
    width = w.shape[1] // 3
    n_heads = width // HEAD_DIM
    hm = _head_block_matrix(width, 1.0 / HEAD_DIM)
    row = lambda i: (i, 0)
    fixed = lambda i: (0, 0)
    return pl.pallas_call(
        functools.partial(_proj_qkv_kernel, width=width, q_scale=q_scale),
        grid=(m // tm,),
        in_specs=[pl.BlockSpec((tm, d), row), pl.BlockSpec((1, d), fixed),
                  pl.BlockSpec((d, 3 * width), fixed), pl.BlockSpec((width, width), fixed),
                  pl.BlockSpec((1, width), fixed), pl.BlockSpec((1, width), fixed)],
        out_specs=[pl.BlockSpec((tm, width), row)] * 3,
        out_shape=[jax.ShapeDtypeStruct((m, width), F32)] * 3,
        compiler_params=_cparams("parallel"),
        name="proj_qkv",
    )(x2d, g.reshape(1, d), w, hm, jnp.tile(g_q, n_heads).reshape(1, width),
      jnp.tile(g_k, n_heads).reshape(1, width))


def _s5_disc_kernel(are_ref, aim_ref, ldt_ref, bre_ref, bim_ref, abre_ref, abim_ref, bbre_ref, bbim_ref):
    lr = are_ref[...]
    li = aim_ref[...]
    dt = jnp.exp(ldt_ref[...])
    mag = jnp.exp(lr * dt)
    ab_re = mag * jnp.cos(li * dt)
    ab_im = mag * jnp.sin(li * dt)
    den = lr * lr + li * li
    zr = ((ab_re - 1.0) * lr + ab_im * li) / den
    zi = (ab_im * lr - (ab_re - 1.0) * li) / den
    abre_ref[...] = ab_re
    abim_ref[...] = ab_im
    br = bre_ref[...]
    bi = bim_ref[...]
    bbre_ref[...] = zr[:, None, :] * br - zi[:, None, :] * bi
    bbim_ref[...] = zr[:, None, :] * bi + zi[:, None, :] * br


def s5_discretize(a_re, a_im, log_dt, b_re, b_im):
    g, p = a_re.shape
    gc = b_re.shape[1]
    return pl.pallas_call(
        _s5_disc_kernel,
        out_shape=[jax.ShapeDtypeStruct((g, p), F32)] * 2 + [jax.ShapeDtypeStruct((g, gc, p), F32)] * 2,
        name="s5_discretize",
    )(a_re, a_im, log_dt.reshape(g, 1), b_re, b_im)


def _gelu_tanh(x):
    c = math.sqrt(2.0 / math.pi)
    return 0.5 * x * (1.0 + jnp.tanh(c * (x + 0.044715 * (x * x * x))))


def _s5_kernel(u_ref, s0re_ref, s0im_ref, abre_ref, abim_ref, bb_ref, cc_ref, d_ref, wglu_ref,
               y_ref, sre_ref, sim_ref, xs_ref, cre_ref, cim_ref, *, tt, n_state, passes, lane_chunk):
    ti = pl.program_id(1)

    @pl.when(ti == 0)
    def _():
        cre_ref[...] = s0re_ref[...]
        cim_ref[...] = s0im_ref[...]

    width = u_ref.shape[-1]
    n_cl, c_in, c_state = bb_ref.shape
    half = c_state // 2
    u = u_ref[...].reshape(tt * SUBLANES, width)
    for c in range(n_cl):
        xc = _mm(u[:, c * c_in:(c + 1) * c_in], bb_ref[c], passes)
        xs_ref[:, c * half:(c + 1) * half] = xc[:, :half]
        xs_ref[:, n_state + c * half:n_state + (c + 1) * half] = xc[:, half:]

    for c0 in range(0, n_state, lane_chunk):
        re_sl = pl.ds(c0, lane_chunk)
        im_sl = pl.ds(n_state + c0, lane_chunk)
        a_re = jnp.broadcast_to(abre_ref[:, re_sl], (SUBLANES, lane_chunk))
        a_im = jnp.broadcast_to(abim_ref[:, re_sl], (SUBLANES, lane_chunk))

        def step(t, carry):
            s_re, s_im = carry
            rows = pl.ds(pl.multiple_of(t * SUBLANES, SUBLANES), SUBLANES)
            n_re = a_re * s_re - a_im * s_im + xs_ref[rows, re_sl]
            n_im = a_re * s_im + a_im * s_re + xs_ref[rows, im_sl]
            xs_ref[rows, re_sl] = n_re
            xs_ref[rows, im_sl] = n_im
            return n_re, n_im

        s_re, s_im = lax.fori_loop(0, tt, step, (cre_ref[:, re_sl], cim_ref[:, re_sl]), unroll=8)
        cre_ref[:, re_sl] = s_re
        cim_ref[:, re_sl] = s_im

    y = jnp.concatenate(
        [_mm(xs_ref[:, c * half:(c + 1) * half], cc_ref[c, :half, :], passes)
         + _mm(xs_ref[:, n_state + c * half:n_state + (c + 1) * half], cc_ref[c, half:, :], passes)
         for c in range(n_cl)], axis=1) + d_ref[...] * u
    y = _gelu_tanh(y)
    y = y * jax.nn.sigmoid(_mm(y, wglu_ref[...]))
    y_ref[...] = y.reshape(tt, SUBLANES, width)
    sre_ref[...] = cre_ref[...]
    sim_ref[...] = cim_ref[...]


def s5_mixer(u_tm, s0_re, s0_im, ab_re, ab_im, bb, cc, d_skip, w_glu, tt, passes):
    t, b, width = u_tm.shape
    n_state = ab_re.shape[-1]
    assert t % tt == 0 and b % SUBLANES == 0
    fixed = lambda bi, ti: (0, 0)
    st = lambda bi, ti: (bi, 0)
    return pl.pallas_call(
        functools.partial(_s5_kernel, tt=tt, n_state=n_state, passes=passes, lane_chunk=512),
        grid=(b // SUBLANES, t // tt),
        in_specs=[pl.BlockSpec((tt, SUBLANES, width), lambda bi, ti: (ti, bi, 0)),
                  pl.BlockSpec((SUBLANES, n_state), st), pl.BlockSpec((SUBLANES, n_state), st),
                  pl.BlockSpec((1, n_state), fixed), pl.BlockSpec((1, n_state), fixed),
                  pl.BlockSpec(bb.shape, lambda bi, ti: (0, 0, 0)), pl.BlockSpec(cc.shape, lambda bi, ti: (0, 0, 0)),
                  pl.BlockSpec((1, width), fixed), pl.BlockSpec((width, width), fixed)],
        out_specs=[pl.BlockSpec((tt, SUBLANES, width), lambda bi, ti: (ti, bi, 0)),
                   pl.BlockSpec((SUBLANES, n_state), st), pl.BlockSpec((SUBLANES, n_state), st)],
        out_shape=[jax.ShapeDtypeStruct((t, b, width), F32),
                   jax.ShapeDtypeStruct((b, n_state), F32), jax.ShapeDtypeStruct((b, n_state), F32)],
        scratch_shapes=[pltpu.VMEM((tt * SUBLANES, 2 * n_state), F32),
                        pltpu.VMEM((SUBLANES, n_state), F32), pltpu.VMEM((SUBLANES, n_state), F32)],
        compiler_params=_cparams("parallel", "arbitrary"),
        name="s5_mixer",
    )(u_tm, s0_re, s0_im, ab_re.reshape(1, n_state), ab_im.reshape(1, n_state), bb, cc,
      d_skip.reshape(1, width), w_glu)


def s5_operands(a_re, a_im, log_dt, b_re, b_im, c_re, c_im):
    g, p = a_re.shape
    gc = b_re.shape[-1]
    ab_re, ab_im, bb_re, bb_im = s5_discretize(a_re, a_im, log_dt, b_re.transpose(0, 2, 1),
                                               b_im.transpose(0, 2, 1))
    cl = LANES // gc
    assert g % cl == 0
    eye = jnp.eye(cl, dtype=F32)
    split = lambda m: m.reshape(g // cl, cl, gc, p)
    bd = lambda m: jnp.einsum('kacp,ab->kacbp', split(m), eye).reshape(g // cl, cl * gc, cl * p)
    bb = jnp.concatenate([bd(bb_re), bd(bb_im)], axis=2)
    bdc = lambda m: jnp.einsum('kacp,ab->kbpac', split(m), eye).reshape(g // cl, cl * p, cl * gc)
    cc = jnp.concatenate([bdc(c_re), -bdc(c_im)], axis=1)
    return ab_re.reshape(g * p), ab_im.reshape(g * p), bb, cc


def _softplus(x):
    return jnp.maximum(x, 0.0) + jnp.log(1.0 + jnp.exp(-jnp.abs(x)))


def _log2_one_minus_sigmoid(x):
    nx = -x
    return jnp.minimum(nx, 0.0) - jnp.log2(1.0 + jnp.exp2(jnp.minimum(x, nx)))


LOG2_E = math.log2(math.e)


def _rwkv_kernel(u_ref, shift0_ref, wkv0_ref, mu_ref, w0_ref, wup_ref, a0_ref, aup_ref, gup_ref, kk_ref,
                 ka_ref, rk_ref, lnw_ref, lnb_ref, hsum_ref, y_ref, wkv_ref, shift_ref,
                 prev_ref, s_ref, yt_ref, *, tt, chunk, width, passes, t_valid):
    ti = pl.program_id(1)
    n_heads = width // HEAD_DIM

    @pl.when(ti == 0)
    def _():
        prev_ref[...] = shift0_ref[...]
        s_ref[...] = wkv0_ref[...]

    u = u_ref[...]
    row = lax.broadcasted_iota(jnp.int32, (tt, 1), 0)
    prev = jnp.where(row == 0, prev_ref[...], pltpu.roll(u, 1, axis=0))
    xs = u + mu_ref[...] * (prev - u)
    prev_ref[...] = u[t_valid - 1:t_valid, :]
    shift_ref[...] = u[t_valid - 1:t_valid, :]

    w3 = 3 * width
    o1 = w3 + RWKV_DECAY_LORA
    o2 = o1 + RWKV_ICLR_LORA
    r = xs[:, :width]
    k = xs[:, width:2 * width]
    v = xs[:, 2 * width:w3]
    w_log = -_softplus(-(w0_ref[...] + _mm(jnp.tanh(xs[:, w3:o1]), wup_ref[...], 3))) - 0.5
    log_decay = -jnp.exp(w_log)
    a = jax.nn.sigmoid(a0_ref[...] + _mm(xs[:, o1:o2], aup_ref[...], 3))
    g = _mm(jax.nn.sigmoid(xs[:, o2:]), gup_ref[...], 3)
    hsum = hsum_ref[...]
    kk = k * kk_ref[...]
    kk = kk / jnp.maximum(jnp.sqrt(_mm(kk * kk, hsum, 2)), 1e-12)
    k2 = k * (1.0 + (a - 1.0) * ka_ref[...])
    a_vec = -kk
    b_vec = kk * a
    if t_valid < tt:
        live = row < t_valid
        log_decay = jnp.where(live, log_decay, 0.0)
        a_vec = jnp.where(live, a_vec, 0.0)
        b_vec = jnp.where(live, b_vec, 0.0)
        k2 = jnp.where(live, k2, 0.0)
    vt = v.T

    ri = lax.broadcasted_iota(jnp.int32, (chunk, chunk), 0)
    ci = lax.broadcasted_iota(jnp.int32, (chunk, chunk), 1)
    upper_strict = ri < ci
    upper_incl = ri <= ci
    lower_incl = (ri >= ci).astype(F32)
    n_double = (min(chunk, t_valid) - 1).bit_length()

    heads = range(n_heads)
    hsl = [slice(h * HEAD_DIM, (h + 1) * HEAD_DIM) for h in heads]
    eye = (ri == ci).astype(F32)

    pre = []
    for c in range(tt // chunk):
        rows = slice(c * chunk, (c + 1) * chunk)
        ld = log_decay[rows]
        cl = _mm(lower_incl, ld, 3)
        e_incl = jnp.exp(cl)
        e_inv = jnp.exp(-cl)
        rt = r[rows] * e_incl
        at = a_vec[rows] * jnp.exp(cl - ld)
        bt = b_vec[rows] * e_inv
        kt = k2[rows] * e_inv
        bk = [jnp.concatenate([bt[:, hs], kt[:, hs]], axis=0) for hs in hsl]
        ar = [jnp.concatenate([at[:, hs], rt[:, hs]], axis=0) for hs in hsl]
        gram = [_mm(bk[h], ar[h], passes, _NT) for h in heads]
        n_t = [jnp.where(upper_strict, gm[:chunk, :chunk], 0.0) for gm in gram]
        m2_t = [jnp.where(upper_strict, gm[chunk:, :chunk], 0.0) for gm in gram]
        m34_t = [jnp.concatenate([jnp.where(upper_incl, gm[:chunk, chunk:], 0.0),
                                  jnp.where(upper_incl, gm[chunk:, chunk:], 0.0)], axis=0) for gm in gram]
        vt_c = [vt[hs, rows] for hs in hsl]
        vm = [_mm(vt_c[h], m2_t[h], passes) for h in heads]
        t_inv = [eye + n for n in n_t]
        nk = n_t
        for _ in range(1, n_double):
            nk = [_mm(n, n, passes) for n in nk]
            t_inv = [t_inv[h] + _mm(t_inv[h], nk[h], passes) for h in heads]
        pre.append((rows, bk, ar, m34_t, vt_c, vm, t_inv, e_incl[chunk - 1:chunk, :]))

    state = [s_ref[h] for h in heads]
    for rows, bk, ar, m34_t, vt_c, vm, t_inv, p_end in pre:
        sa = [_mm(state[h], ar[h], passes, _NT) for h in heads]
        x = [_mm(sa[h][:, :chunk] + vm[h], t_inv[h], passes) for h in heads]
        uv = [jnp.concatenate([x[h], vt_c[h]], axis=1) for h in heads]
        for h in heads:
            yt_ref[hsl[h], rows] = sa[h][:, chunk:] + _mm(uv[h], m34_t[h], passes)
        state = [(state[h] + _mm(uv[h], bk[h], passes)) * p_end[:, hsl[h]] for h in heads]
    for h in heads:
        s_ref[h] = state[h]

    y = yt_ref[...].T
    hmean = hsum * (1.0 / HEAD_DIM)
    yc = y - _mm(y, hmean, 2)
    var = _mm(yc * yc, hmean, 2)
    yn = yc * lax.rsqrt(var + RWKV_GN_EPS) * lnw_ref[...] + lnb_ref[...]
    bonus = _mm(r * k2 * rk_ref[...], hsum, 2) * v
    y_ref[...] = (yn + bonus) * g
    wkv_ref[...] = s_ref[...]


def rwkv_mixer(u, shift0, wkv0, p, tt, chunk, passes):
    b, t_valid, n_in = u.shape
    n_heads = wkv0.shape[1]
    width = n_heads * HEAD_DIM
    if t_valid < tt:
        assert t_valid <= RWKV_MAX_CHUNK
        u = _pad_rows(u, tt)
        chunk = tt
    else:
        assert chunk <= RWKV_MAX_CHUNK
        t_valid = tt
    t = u.shape[1]
    assert t % tt == 0 and tt % chunk == 0
    fixed = lambda bi, ti: (0, 0)
    vec = lambda n: pl.BlockSpec((1, n), fixed)
    args = [p['rwkv_mu'].reshape(1, n_in), p['rwkv_w0'].reshape(1, width), p['rwkv_w_up'],
            p['rwkv_a0'].reshape(1, width), p['rwkv_a_up'], p['rwkv_g_up'], p['rwkv_k_k'].reshape(1, width),
            p['rwkv_k_a'].reshape(1, width), p['rwkv_r_k'].reshape(1, width), p['rwkv_ln_w'].reshape(1, width),
            p['rwkv_ln_b'].reshape(1, width), _head_block_matrix(width, 1.0)]
    specs = [vec(n_in), vec(width), pl.BlockSpec(p['rwkv_w_up'].shape, fixed), vec(width),
             pl.BlockSpec(p['rwkv_a_up'].shape, fixed), pl.BlockSpec(p['rwkv_g_up'].shape, fixed),
             vec(width), vec(width), vec(width), vec(width), vec(width), pl.BlockSpec((width, width), fixed)]
    y, wkv, shift = pl.pallas_call(
        functools.partial(_rwkv_kernel, tt=tt, chunk=chunk, width=width, passes=passes, t_valid=t_valid),
        grid=(b, t // tt),
        in_specs=[pl.BlockSpec((None, tt, n_in), lambda bi, ti: (bi, ti, 0)),
                  pl.BlockSpec((None, 1, n_in), lambda bi, ti: (bi, 0, 0)),
                  pl.BlockSpec((None, n_heads, HEAD_DIM, HEAD_DIM), lambda bi, ti: (bi, 0, 0, 0))] + specs,
        out_specs=[pl.BlockSpec((None, tt, width), lambda bi, ti: (bi, ti, 0)),
                   pl.BlockSpec((None, n_heads, HEAD_DIM, HEAD_DIM), lambda bi, ti: (bi, 0, 0, 0)),
                   pl.BlockSpec((None, 1, n_in), lambda bi, ti: (bi, 0, 0))],
        out_shape=[jax.ShapeDtypeStruct((b, t, width), F32),
                   jax.ShapeDtypeStruct((b, n_heads, HEAD_DIM, HEAD_DIM), F32),
                   jax.ShapeDtypeStruct((b, 1, n_in), F32)],
        scratch_shapes=[pltpu.VMEM((1, n_in), F32), pltpu.VMEM((n_heads, HEAD_DIM, HEAD_DIM), F32),
                        pltpu.VMEM((width, tt), F32)],
        compiler_params=_cparams("parallel", "arbitrary"),
        name="rwkv_mixer",
    )(u, shift0.reshape(b, 1, n_in), wkv0, *args)
    return y[:, :t_valid] if t_valid < tt else y, wkv, shift.reshape(b, n_in)


NEG_BIG = -1e30
_TN = (((0,), (0,)), ((), ()))
HEAD_PAIR = 2 * HEAD_DIM


def _proj_qkv_t_kernel(x_ref, g_ref, wt_ref, gq_ref, gk_ref, qt_ref, kt_ref, vt_ref, krow_ref, vtb_ref, kmean_ref,
                       *, width, q_scale, block):
    ti = pl.program_id(1)
    tm = x_ref.shape[0]
    n_heads = width // HEAD_DIM
    h = _rms(x_ref[...], g_ref[...]).astype(BF16)
    ut = _dg(wt_ref[...], h, _NT)

    def head_rms_t(xt, g_col):
        x3 = xt.reshape(n_heads, HEAD_DIM, tm)
        ms = jnp.mean(x3 * x3, axis=1, keepdims=True)
        return (x3 * lax.rsqrt(ms + EPS) * g_col[None, :, :]).reshape(width, tm)

    qt_ref[...] = head_rms_t(ut[:width], gq_ref[...]) * q_scale
    kt = head_rms_t(ut[width:2 * width], gk_ref[...])
    kt_ref[...] = kt
    vt = ut[2 * width:]
    vt_ref[...] = vt
    vtb_ref[...] = vt.astype(BF16)
    k_rows = kt.T
    krow_ref[...] = k_rows.astype(BF16)
    @pl.when(ti == 0)
    def _():
        kmean_ref[...] = jnp.zeros_like(kmean_ref)

    per_tile = tm // block
    blk = lax.broadcasted_iota(jnp.int32, kmean_ref.shape, 0)
    kmean = kmean_ref[...]
    for n in range(per_tile):
        mean = jnp.sum(k_rows[n * block:(n + 1) * block], axis=0, keepdims=True) * (1.0 / block)
        kmean = jnp.where(blk == ti * per_tile + n, mean, kmean)
    kmean_ref[...] = kmean


def proj_qkv_t(x, g, w_t, g_q, g_k, tm, q_scale, layer, depth, prev_kv):
    b, t, d = x.shape
    width = w_t.shape[0] // 3
    nb = t // MOBA_BLOCK
    assert t % tm == 0 and tm % MOBA_BLOCK == 0
    fixed = lambda bi, ti: (0, 0)
    tr = pl.BlockSpec((None, width, tm), lambda bi, ti: (bi, 0, ti))
    slab = pl.BlockSpec((None, None, width, tm), lambda bi, ti: (layer, bi, 0, ti))
    kernel_fn = functools.partial(_proj_qkv_t_kernel, width=width, q_scale=q_scale, block=MOBA_BLOCK)
    in_specs = [pl.BlockSpec((None, tm, d), lambda bi, ti: (bi, ti, 0)), pl.BlockSpec((1, d), fixed),
                pl.BlockSpec((3 * width, d), fixed), pl.BlockSpec((HEAD_DIM, 1), fixed),
                pl.BlockSpec((HEAD_DIM, 1), fixed)]
    args = [x, g.reshape(1, d), w_t, g_q.reshape(HEAD_DIM, 1), g_k.reshape(HEAD_DIM, 1)]
    aliases = {}
    if prev_kv is not None:
        n_in = len(args)
        args += list(prev_kv)
        in_specs += [pl.BlockSpec(memory_space=pl.ANY)] * 2
        aliases = {n_in: 1, n_in + 1: 2}
        body = kernel_fn
        kernel_fn = lambda *refs: body(*refs[:n_in], *refs[n_in + 2:])
    return pl.pallas_call(
        kernel_fn,
        grid=(b, t // tm),
        in_specs=in_specs,
        out_specs=[tr, slab, slab, pl.BlockSpec((None, tm, width), lambda bi, ti: (bi, ti, 0)), tr,
                   pl.BlockSpec((None, nb, width), lambda bi, ti: (bi, 0, 0))],
        out_shape=[jax.ShapeDtypeStruct((b, width, t), F32)] + [jax.ShapeDtypeStruct((depth, b, width, t), F32)] * 2
                  + [jax.ShapeDtypeStruct((b, t, width), BF16), jax.ShapeDtypeStruct((b, width, t), BF16),
                     jax.ShapeDtypeStruct((b, nb, width), F32)],
        input_output_aliases=aliases,
        compiler_params=_cparams("parallel", "arbitrary"),
        name="proj_qkv_t",
    )(*args)


def _head_pair_queries(qt_ref, h, tq):
    p0 = (h // 2) * HEAD_PAIR
    rows = lax.broadcasted_iota(jnp.int32, (HEAD_PAIR, tq), 0) // HEAD_DIM
    return jnp.where(rows == h % 2, qt_ref[p0:p0 + HEAD_PAIR, :], 0.0)


def _sb_prompt_kernel(qt_ref, k_ref, vt_ref, o_ref, qm_ref, acc_ref, run_ref, *, tq, width):
    qi = pl.program_id(1)
    n_heads = width // HEAD_DIM
    krow = lax.broadcasted_iota(jnp.int32, (tq, tq), 0)
    qcol = lax.broadcasted_iota(jnp.int32, (tq, tq), 1)
    later = jnp.where(qcol > krow, 1.0, 0.0).astype(BF16)
    strict = krow < qcol
    for h in range(n_heads):
        qm_ref[h] = _head_pair_queries(qt_ref, h, tq).astype(BF16)
    acc_ref[...] = jnp.zeros_like(acc_ref)
    run_ref[...] = jnp.zeros_like(run_ref)

    heads = range(n_heads)
    hsl = [slice(h * HEAD_DIM, (h + 1) * HEAD_DIM) for h in heads]

    def block(j, diag):
        ks = pl.ds(pl.multiple_of(j * tq, tq), tq)
        run = run_ref[...]
        z = [_dg(k_ref[ks, (h // 2) * HEAD_PAIR:(h // 2 + 1) * HEAD_PAIR], qm_ref[h], _NN) for h in heads]
        l1m = [_log2_one_minus_sigmoid(zh) for zh in z]
        if diag:
            l1m = [jnp.where(strict, x, 0.0) for x in l1m]
        rest = [_dg(later, l1m[h].astype(BF16), _NN) + l1m[h] + run[h:h + 1, :] for h in heads]
        w = [jnp.exp2(z[h] + rest[h]) for h in heads]
        if diag:
            w = [jnp.where(strict, x, 0.0) for x in w]
        pv = [_dg(vt_ref[hsl[h], ks], w[h].astype(BF16), _NN) for h in heads]
        for h in heads:
            acc_ref[hsl[h], :] += pv[h]
        run_ref[...] = jnp.concatenate([rest[h][0:1, :] for h in heads], axis=0)

    block(qi, True)

    def body(jj, carry):
        block(qi - 1 - jj, False)
        return carry

    lax.fori_loop(0, qi, body, 0)
    o_ref[...] = acc_ref[...]


def _attn_prompt_call(kernel_fn, name, qt, k_rows, vt_b, extra, extra_specs, scratch, tq):
    b, width, t = qt.shape
    n_heads = width // HEAD_DIM
    tile = pl.BlockSpec((None, width, tq), lambda bi, qi: (bi, 0, qi))
    return pl.pallas_call(
        kernel_fn,
        grid=(b, t // tq),
        in_specs=[tile, pl.BlockSpec((None, t, width), lambda bi, qi: (bi, 0, 0)),
                  pl.BlockSpec((None, width, t), lambda bi, qi: (bi, 0, 0))] + extra_specs,
        out_specs=tile,
        out_shape=jax.ShapeDtypeStruct((b, width, t), F32),
        scratch_shapes=[pltpu.VMEM((n_heads, HEAD_PAIR, tq), BF16), pltpu.VMEM((width, tq), F32)] + scratch,
        compiler_params=_cparams("parallel", "arbitrary"),
        name=name,
    )(qt, k_rows, vt_b, *extra)


def sb_prompt(qt, k_rows, vt_b, tq):
    b, width, t = qt.shape
    assert t % tq == 0
    n_heads = width // HEAD_DIM
    return _attn_prompt_call(functools.partial(_sb_prompt_kernel, tq=tq, width=width), "sb_prompt", qt, k_rows,
                             vt_b, [], [], [pltpu.VMEM((n_heads, tq), F32)], tq)


def _topk_block_mask(gate, n_valid, topk, axis):
    nb = gate.shape[axis]
    idx = lax.broadcasted_iota(jnp.int32, gate.shape, axis)
    rank = jnp.zeros(gate.shape, jnp.int32)
    for m in range(nb):
        gm = gate[:, m:m + 1] if axis == 1 else gate[m:m + 1, :]
        beats = (gm > gate) | ((gm == gate) & (m < idx))
        rank = rank + jnp.where(beats & (m < n_valid), 1, 0)
    return (idx < n_valid) & (rank < topk)


def _moba_prompt_kernel(qt_ref, k_ref, vt_ref, kmean_ref, o_ref, qm_ref, acc_ref, sel_ref, m_ref, l_ref,
                        *, tq, width):
    qi = pl.program_id(1)
    n_heads = width // HEAD_DIM
    krow = lax.broadcasted_iota(jnp.int32, (tq, tq), 0)
    qcol = lax.broadcasted_iota(jnp.int32, (tq, tq), 1)
    causal = krow <= qcol
    for h in range(n_heads):
        p0 = (h // 2) * HEAD_PAIR
        q_pair = _head_pair_queries(qt_ref, h, tq)
        qm_ref[h] = q_pair.astype(BF16)
        gate = _mm(kmean_ref[:, p0:p0 + HEAD_PAIR], q_pair, 3)
        sel_ref[h] = jnp.where(_topk_block_mask(gate, qi, MOBA_TOPK, 0), 0.0, NEG_BIG)
    acc_ref[...] = jnp.zeros_like(acc_ref)
    l_ref[...] = jnp.zeros_like(l_ref)
    m_ref[...] = jnp.full(m_ref.shape, NEG_BIG, F32)

    heads = range(n_heads)
    hsl = [slice(h * HEAD_DIM, (h + 1) * HEAD_DIM) for h in heads]

    def block(n, diag):
        ks = pl.ds(pl.multiple_of(n * tq, tq), tq)
        m_old = m_ref[...]
        l_old = l_ref[...]
        s = [_dg(k_ref[ks, (h // 2) * HEAD_PAIR:(h // 2 + 1) * HEAD_PAIR], qm_ref[h], _NN) for h in heads]
        if diag:
            s = [jnp.where(causal, x, NEG_BIG) for x in s]
        else:
            s = [s[h] + sel_ref[h, pl.ds(n, 1), :] for h in heads]
        m_new = [jnp.maximum(m_old[h:h + 1, :], jnp.max(s[h], axis=0, keepdims=True)) for h in heads]
        p = [jnp.exp(s[h] - m_new[h]) for h in heads]
        alpha = [jnp.exp(m_old[h:h + 1, :] - m_new[h]) for h in heads]
        l_new = [alpha[h] * l_old[h:h + 1, :] + jnp.sum(p[h], axis=0, keepdims=True) for h in heads]
        pv = [_dg(vt_ref[hsl[h], ks], p[h].astype(BF16), _NN) for h in heads]
        for h in heads:
            acc_ref[hsl[h], :] = alpha[h] * acc_ref[hsl[h], :] + pv[h]
        m_ref[...] = jnp.concatenate(m_new, axis=0)
        l_ref[...] = jnp.concatenate(l_new, axis=0)

    block(qi, True)

    def body(n, carry):
        block(n, False)
        return carry

    lax.fori_loop(0, qi, body, 0)
    for h in range(n_heads):
        hs = slice(h * HEAD_DIM, (h + 1) * HEAD_DIM)
        o_ref[hs, :] = acc_ref[hs, :] / l_ref[h:h + 1, :]


def moba_prompt(qt, k_rows, vt_b, kmean):
    b, width, t = qt.shape
    tq = MOBA_BLOCK
    nb = t // tq
    assert t % tq == 0
    n_heads = width // HEAD_DIM
    return _attn_prompt_call(functools.partial(_moba_prompt_kernel, tq=tq, width=width), "moba_prompt", qt, k_rows,
                             vt_b, [kmean], [pl.BlockSpec((None, nb, width), lambda bi, qi: (bi, 0, 0))],
                             [pltpu.VMEM((n_heads, nb, tq), F32), pltpu.VMEM((n_heads, tq), F32),
                              pltpu.VMEM((n_heads, tq), F32)], tq)


def _stack_heads(q, n_heads):
    t, width = q.shape
    rows = lax.broadcasted_iota(jnp.int32, (n_heads * t, width), 0) // t
    lanes = lax.broadcasted_iota(jnp.int32, (n_heads * t, width), 1) // HEAD_DIM
    return jnp.where(rows == lanes, jnp.concatenate([q] * n_heads, axis=0), 0.0)


def _unstack_heads(acc, n_heads):
    rows_total, width = acc.shape
    t = rows_total // n_heads
    rows = lax.broadcasted_iota(jnp.int32, (rows_total, width), 0) // t
    lanes = lax.broadcasted_iota(jnp.int32, (rows_total, width), 1) // HEAD_DIM
    kept = jnp.where(rows == lanes, acc, 0.0).reshape(n_heads, t, width)
    return jnp.sum(kept, axis=0)


def _from_here_matrix(tk):
    kj = lax.broadcasted_iota(jnp.int32, (2 * tk, tk), 0) % tk
    ks = lax.broadcasted_iota(jnp.int32, (2 * tk, tk), 1)
    return jnp.where(kj >= ks, 1.0, 0.0).astype(BF16)


def _sum_from_here(x, from_here):
    return _dg(jnp.concatenate(_split_bf16(x), axis=1), from_here, _NN)


def _sb_sample_kernel(pt_ref, q_ref, knew_ref, vnew_ref, *refs, n_pages_step, t_new, n_heads):
    k_refs = refs[:n_pages_step]
    v_refs = refs[n_pages_step:2 * n_pages_step]
    o_ref, qs_ref, acc_ref, run_ref = refs[2 * n_pages_step:]
    step = pl.program_id(1)
    tk = knew_ref.shape[0]
    from_here = _from_here_matrix(tk)

    def tile(k_tile, v_tile, mask, paged):
        z = _dg(qs_ref[...], k_tile.astype(BF16), _NN if paged else _NT)
        l1m = _log2_one_minus_sigmoid(z)
        if mask is not None:
            l1m = jnp.where(mask, l1m, 0.0)
        rest = _sum_from_here(l1m, from_here) + run_ref[...]
        w = jnp.exp2(z + rest)
        if mask is not None:
            w = jnp.where(mask, w, 0.0)
        acc_ref[...] += _mm(w, v_tile, 1, _NT if paged else _NN)
        run_ref[...] = rest[:, 0:1]

    @pl.when(step == 0)
    def _():
        qs_ref[...] = _stack_heads(q_ref[...], n_heads).astype(BF16)
        acc_ref[...] = jnp.zeros_like(acc_ref)
        run_ref[...] = jnp.zeros_like(run_ref)
        rows = lax.broadcasted_iota(jnp.int32, (n_heads * t_new, tk), 0) % t_new
        cols = lax.broadcasted_iota(jnp.int32, (n_heads * t_new, tk), 1)
        tile(knew_ref[...], vnew_ref[...], cols < rows, False)

    qs = qs_ref[...]
    z = [_dg(qs, k_refs[i][...].astype(BF16), _NN) for i in range(n_pages_step)]
    cum = [_sum_from_here(_log2_one_minus_sigmoid(x), from_here) for x in z]
    run = run_ref[...]
    acc = acc_ref[...]
    for i in range(n_pages_step):
        w = jnp.exp2(z[i] + cum[i] + run)
        acc = acc + _mm(w, v_refs[i][...], 1, _NT)
        run = run + cum[i][:, 0:1]
    acc_ref[...] = acc
    run_ref[...] = run

    @pl.when(step == pl.num_programs(1) - 1)
    def _():
        o_ref[...] = _unstack_heads(acc_ref[...], n_heads)


def _page_specs(n_pages, n_pages_step, width, page_size, reverse):
    def spec(i):
        def index(bi, si, pt):
            p = si * n_pages_step + i
            return (pt[bi, n_pages - 1 - p if reverse else p], 0, 0)
        return pl.BlockSpec((None, width, page_size), index)
    return [spec(i) for i in range(n_pages_step)]


def _pad_rows(x, rows):
    return jnp.pad(x, ((0, 0), (0, rows - x.shape[1]), (0, 0)))


def sb_sample(q, k_new, v_new, k_pool, v_pool, pages, n_pages_step):
    b, t_new, width = q.shape
    n_heads = width // HEAD_DIM
    page_size = k_pool.shape[2]
    n_pages = pages.shape[1]
    assert n_pages % n_pages_step == 0 and t_new <= page_size
    new = pl.BlockSpec((None, t_new, width), lambda bi, si, pt: (bi, 0, 0))
    new_pad = pl.BlockSpec((None, page_size, width), lambda bi, si, pt: (bi, 0, 0))
    page_specs = _page_specs(n_pages, n_pages_step, width, page_size, reverse=True)
    return pl.pallas_call(
        functools.partial(_sb_sample_kernel, n_pages_step=n_pages_step, t_new=t_new, n_heads=n_heads),
        grid_spec=pltpu.PrefetchScalarGridSpec(
            num_scalar_prefetch=1, grid=(b, n_pages // n_pages_step),
            in_specs=[new, new_pad, new_pad] + page_specs + page_specs,
            out_specs=new,
            scratch_shapes=[pltpu.VMEM((n_heads * t_new, width), BF16),
                            pltpu.VMEM((n_heads * t_new, width), F32),
                            pltpu.VMEM((n_heads * t_new, 1), F32)]),
        out_shape=jax.ShapeDtypeStruct((b, t_new, width), F32),
        compiler_params=_cparams("parallel", "arbitrary"),
        name="sb_sample",
    )(pages, q, _pad_rows(k_new, page_size), _pad_rows(v_new, page_size),
      *([k_pool] * n_pages_step), *([v_pool] * n_pages_step))


def _moba_sample_kernel(pt_ref, q_ref, knew_ref, vnew_ref, *refs, n_pages_step, pages_per_block, t_new, n_heads,
                        key_steps):
    k_refs = refs[:n_pages_step]
    v_refs = refs[n_pages_step:2 * n_pages_step]
    o_ref, qs_ref, kb_ref, kmean_ref, sel_ref, acc_ref, m_ref, l_ref = refs[2 * n_pages_step:]
    step = pl.program_id(1)
    nb = kmean_ref.shape[1]
    rows_total = n_heads * t_new
    page_size = k_refs[0].shape[1]

    def tile(k_tile, v_tile, mask, paged):
        s = jnp.where(mask, _dg(qs_ref[...], k_tile.astype(BF16), _NN if paged else _NT), NEG_BIG)
        m_new = jnp.maximum(m_ref[...], jnp.max(s, axis=1, keepdims=True))
        p = jnp.exp(s - m_new)
        alpha = jnp.exp(m_ref[...] - m_new)
        l_ref[...] = alpha * l_ref[...] + jnp.sum(p, axis=1, keepdims=True)
        acc_ref[...] = alpha * acc_ref[...] + _mm(p, v_tile, 1, _NT if paged else _NN)
        m_ref[...] = m_new

    @pl.when(step == 0)
    def _():
        qs_ref[...] = _stack_heads(q_ref[...], n_heads).astype(BF16)
        kmean_ref[...] = jnp.zeros_like(kmean_ref)
        acc_ref[...] = jnp.zeros_like(acc_ref)
        l_ref[...] = jnp.zeros_like(l_ref)
        m_ref[...] = jnp.full(m_ref.shape, NEG_BIG, F32)
        tk = knew_ref.shape[0]
        rows = lax.broadcasted_iota(jnp.int32, (rows_total, tk), 0) % t_new
        cols = lax.broadcasted_iota(jnp.int32, (rows_total, tk), 1)
        tile(knew_ref[...], vnew_ref[...], cols <= rows, False)

    @pl.when(step < key_steps)
    def _():
        blk_lane = lax.broadcasted_iota(jnp.int32, kmean_ref.shape, 1)
        kmean = kmean_ref[...]
        for n in range(n_pages_step // pages_per_block):
            tot = None
            for i in range(pages_per_block):
                page = n * pages_per_block + i
                k_page = k_refs[page][...]
                kb_ref[step * n_pages_step + page] = k_page.astype(BF16)
                tot = k_page if tot is None else tot + k_page
            mean = jnp.sum(tot, axis=1, keepdims=True) * (1.0 / MOBA_BLOCK)
            kmean = jnp.where(blk_lane == step * (n_pages_step // pages_per_block) + n, mean, kmean)
        kmean_ref[...] = kmean

    @pl.when(step == key_steps)
    def _():
        gate = _mm(_stack_heads(q_ref[...], n_heads), kmean_ref[...], 3)
        sel_ref[...] = jnp.where(_topk_block_mask(gate, nb, MOBA_TOPK, 1), 1.0, 0.0)

    @pl.when(step >= key_steps)
    def _():
        first = (step - key_steps) * n_pages_step
        qs = qs_ref[...]
        n_keys = n_pages_step * page_size
        key_blk = (first * page_size + lax.broadcasted_iota(jnp.int32, (nb, n_keys), 1)) // MOBA_BLOCK
        expand = jnp.where(key_blk == lax.broadcasted_iota(jnp.int32, (nb, n_keys), 0), 1.0, 0.0).astype(BF16)
        picked = _dg(sel_ref[...].astype(BF16), expand, _NN) > 0.5
        mask = [picked[:, i * page_size:(i + 1) * page_size] for i in range(n_pages_step)]
        s = [jnp.where(mask[i], _dg(qs, kb_ref[first + i], _NN), NEG_BIG) for i in range(n_pages_step)]
        s_max = s[0]
        for x in s[1:]:
            s_max = jnp.maximum(s_max, x)
        m_old = m_ref[...]
        m_new = jnp.maximum(m_old, jnp.max(s_max, axis=1, keepdims=True))
        p = [jnp.exp(x - m_new) for x in s]
        alpha = jnp.exp(m_old - m_new)
        p_sum = p[0]
        acc = alpha * acc_ref[...] + _mm(p[0], v_refs[0][...], 1, _NT)
        for i in range(1, n_pages_step):
            p_sum = p_sum + p[i]
            acc = acc + _mm(p[i], v_refs[i][...], 1, _NT)
        l_ref[...] = alpha * l_ref[...] + jnp.sum(p_sum, axis=1, keepdims=True)
        acc_ref[...] = acc
        m_ref[...] = m_new

    @pl.when(step == pl.num_programs(1) - 1)
    def _():
        o_ref[...] = _unstack_heads(acc_ref[...] / l_ref[...], n_heads)


def moba_sample(q, k_new, v_new, k_pool, v_pool, pages, n_pages_step):
    b, t_new, width = q.shape
    n_heads = width // HEAD_DIM
    page_size = k_pool.shape[2]
    n_pages = pages.shape[1]
    pages_per_block = MOBA_BLOCK // page_size
    nb = n_pages // pages_per_block
    key_steps = n_pages // n_pages_step
    assert n_pages % n_pages_step == 0 and n_pages_step % pages_per_block == 0 and t_new <= page_size
    assert (n_pages * page_size) % MOBA_BLOCK == 0 and t_new <= MOBA_BLOCK
    new = pl.BlockSpec((None, t_new, width), lambda bi, si, pt: (bi, 0, 0))
    new_pad = pl.BlockSpec((None, page_size, width), lambda bi, si, pt: (bi, 0, 0))

    def page_spec(i, keys):
        def index(bi, si, pt):
            s = jnp.minimum(si, key_steps - 1) if keys else jnp.maximum(si - key_steps, 0)
            return (pt[bi, s * n_pages_step + i], 0, 0)
        return pl.BlockSpec((None, width, page_size), index)

    rows_total = n_heads * t_new
    return pl.pallas_call(
        functools.partial(_moba_sample_kernel, n_pages_step=n_pages_step, pages_per_block=pages_per_block,
                          t_new=t_new, n_heads=n_heads, key_steps=key_steps),
        grid_spec=pltpu.PrefetchScalarGridSpec(
            num_scalar_prefetch=1, grid=(b, 2 * key_steps),
            in_specs=[new, new_pad, new_pad] + [page_spec(i, True) for i in range(n_pages_step)]
                     + [page_spec(i, False) for i in range(n_pages_step)],
            out_specs=new,
            scratch_shapes=[pltpu.VMEM((rows_total, width), BF16), pltpu.VMEM((n_pages, width, page_size), BF16),
                            pltpu.VMEM((width, nb), F32), pltpu.VMEM((rows_total, nb), F32),
                            pltpu.VMEM((rows_total, width), F32), pltpu.VMEM((rows_total, 1), F32),
                            pltpu.VMEM((rows_total, 1), F32)]),
        out_shape=jax.ShapeDtypeStruct((b, t_new, width), F32),
        compiler_params=_cparams("parallel", "arbitrary"),
        name="moba_sample",
    )(pages, q, _pad_rows(k_new, page_size), _pad_rows(v_new, page_size),
      *([k_pool] * n_pages_step), *([v_pool] * n_pages_step))


def _merge_kernel(x_ref, g_ref, wg_ref, y0_ref, y1_ref, y2_ref, y3_ref, wb_ref, wo_ref, o_ref, *, d_model, layouts):
    x = x_ref[...]
    h = _rms(x, g_ref[...]).astype(BF16)
    merged = None
    for n, (y_ref, layout) in enumerate(zip((y0_ref, y1_ref, y2_ref, y3_ref), layouts)):
        gate = jax.nn.sigmoid(_dg(h, wg_ref[:, n * d_model:(n + 1) * d_model], _NN))
        term = gate * _mm(y_ref[...], wb_ref[n], 1, _TN if layout == "transposed" else _NN)
        merged = term if merged is None else merged + term
    o_ref[...] = x + _mm(merged, wo_ref[...])


def merge(x, g, w_gate, ys, layouts, w_branch, w_out, tm):
    b, t, d = x.shape
    width = w_branch.shape[1]
    tm = min(tm, t)
    assert t % tm == 0
    fixed = lambda bi, ti: (0, 0)
    seq = lambda bi, ti: (bi, ti, 0)
    y_spec = {"rows": pl.BlockSpec((None, tm, width), seq),
              "time_major": pl.BlockSpec((tm, width), lambda bi, ti: (ti, bi)),
              "transposed": pl.BlockSpec((None, width, tm), lambda bi, ti: (bi, 0, ti))}
    return pl.pallas_call(
        functools.partial(_merge_kernel, d_model=d, layouts=tuple(layouts)),
        grid=(b, t // tm),
        in_specs=[pl.BlockSpec((None, tm, d), seq), pl.BlockSpec((1, d), fixed), _resident((d, N_BRANCH * d))]
                 + [y_spec[k] for k in layouts]
                 + [_resident((N_BRANCH, width, d)), _resident((d, d))],
        out_specs=pl.BlockSpec((None, tm, d), seq),
        out_shape=jax.ShapeDtypeStruct((b, t, d), F32),
        compiler_params=_cparams("parallel", "parallel"),
        name="merge",
    )(x, g.reshape(1, d), w_gate, *ys, w_branch, w_out)


def _ffn_kernel(x_ref, g_ref, wup_ref, cw_ref, cb_ref, wdn_ref, conv0_ref, o_ref, conv_ref, st_ref,
                *, nseq, tt, d_ff, col_chunk):
    ti = pl.program_id(1)

    @pl.when(ti == 0)
    def _():
        st_ref[...] = conv0_ref[...]

    d_model = x_ref.shape[-1]
    x = x_ref[...].reshape(nseq * tt, d_model)
    h = _rms(x, g_ref[...]).astype(BF16)
    t_idx = lax.broadcasted_iota(jnp.int32, (1, tt, 1), 1)

    def conv_cols(c0):
        cols = slice(c0, c0 + col_chunk)
        up = _dg(h, wup_ref[:, cols], _NN).reshape(nseq, tt, col_chunk)
        st = st_ref[:, :, cols]
        p1 = jnp.where(t_idx == 0, st[:, 1:2, :], pltpu.roll(up, 1, axis=1))
        p2 = jnp.where(t_idx == 0, st[:, 0:1, :], jnp.where(t_idx == 1, st[:, 1:2, :], pltpu.roll(up, 2, axis=1)))
        st_ref[:, :, cols] = up[:, tt - 2:, :]
        cw = cw_ref[:, cols]
        c = cb_ref[:, cols] + cw[2:3, :] * up + cw[1:2, :] * p1 + cw[0:1, :] * p2
        return c.reshape(nseq * tt, col_chunk)

    acc = x
    for c0 in range(0, d_ff, col_chunk):
        a = conv_cols(c0)
        b = conv_cols(d_ff + c0)
        acc = acc + _mm(a * jax.nn.sigmoid(a) * b, wdn_ref[c0:c0 + col_chunk, :])
    o_ref[...] = acc.reshape(nseq, tt, d_model)
    conv_ref[...] = st_ref[...]


def conv_ffn(x, g, w_up, conv_w, conv_b, w_down, conv0, nseq, tt, col_chunk):
    b, t, d = x.shape
    d_ff = w_down.shape[0]
    assert b % nseq == 0 and t % tt == 0 and d_ff % col_chunk == 0 and tt >= CONV_W - 1
    fixed = lambda bi, ti: (0, 0)
    seq = lambda bi, ti: (bi, ti, 0)
    st = lambda bi, ti: (bi, 0, 0)
    return pl.pallas_call(
        functools.partial(_ffn_kernel, nseq=nseq, tt=tt, d_ff=d_ff, col_chunk=col_chunk),
        grid=(b // nseq, t // tt),
        in_specs=[pl.BlockSpec((nseq, tt, d), seq), pl.BlockSpec((1, d), fixed),
                  _resident((d, 2 * d_ff)), pl.BlockSpec((CONV_W, 2 * d_ff), fixed),
                  pl.BlockSpec((1, 2 * d_ff), fixed), _resident((d_ff, d)),
                  pl.BlockSpec((nseq, CONV_W - 1, 2 * d_ff), st)],
        out_specs=[pl.BlockSpec((nseq, tt, d), seq), pl.BlockSpec((nseq, CONV_W - 1, 2 * d_ff), st)],
        out_shape=[jax.ShapeDtypeStruct((b, t, d), F32), jax.ShapeDtypeStruct((b, CONV_W - 1, 2 * d_ff), F32)],
        scratch_shapes=[pltpu.VMEM((nseq, CONV_W - 1, 2 * d_ff), F32)],
        compiler_params=_cparams("parallel", "arbitrary"),
        name="conv_ffn",
    )(x, g.reshape(1, d), w_up, conv_w, conv_b.reshape(1, 2 * d_ff), w_down, conv0)


MOBA_Q_SCALE = HEAD_DIM ** -0.5
SB_Q_SCALE = HEAD_DIM ** -0.5 * LOG2_E

PROMPT_TILES = dict(proj_tm=1024, qkv_tm=512, s5_tt=128, s5_passes=1, rwkv_tt=256, rwkv_chunk=128, rwkv_passes=1,
                    sb_tq=256, merge_tm=512, ffn_tt=512, ffn_cols=1408)
SAMPLE_TILES = dict(proj_tm=256, s5_tt=8, s5_passes=3, rwkv_tt=128, rwkv_chunk=64, rwkv_passes=1,
                    merge_tm=256, ffn_cols=1408, sb_pages=32, moba_pages=32)


def _layer(x, st, past, p, tiles):
    b, t, d = x.shape
    width = d // 2
    n_heads = width // HEAD_DIM
    n_rwkv = 3 * width + RWKV_DECAY_LORA + RWKV_ICLR_LORA + RWKV_GATE_LORA
    prompt = past is None
    n_state = p['s5_ab_re'].shape[0]
    s5_args = (st['s5_re'].reshape(b, n_state), st['s5_im'].reshape(b, n_state), p['s5_ab_re'], p['s5_ab_im'],
               p['s5_bb'][tiles['s5_passes']], p['s5_cc'][tiles['s5_passes']], p['s5_d'], p['s5_w_glu'],
               tiles['s5_tt'], tiles['s5_passes'])

    if prompt:
        u_s5, u_rw = proj_plain(x, p['norm1_g'], p['w_mix'], (width, n_rwkv), tiles['proj_tm'], True)
        y_s5, s5_re, s5_im = s5_mixer(u_s5.reshape(t, b, width), *s5_args)
        y_s5 = y_s5.reshape(t, b * width)
        qt, k_sb, v_sb, k_rows, vt_b, _ = proj_qkv_t(x, p['norm1_g'], p['w_sb'].T, p['sb_q_g'], p['sb_k_g'],
                                                     tiles['qkv_tm'], SB_Q_SCALE, p['layer'], p['depth'],
                                                     st['kv_sb'])
        y_sb = sb_prompt(qt, k_rows, vt_b, tiles['sb_tq'])
        qt, k_mb, v_mb, k_rows, vt_b, kmean = proj_qkv_t(x, p['norm1_g'], p['w_mb'].T, p['moba_q_g'],
                                                         p['moba_k_g'], tiles['qkv_tm'], MOBA_Q_SCALE,
                                                         p['layer'], p['depth'], st['kv_mb'])
        y_mb = moba_prompt(qt, k_rows, vt_b, kmean)
        layouts = ("time_major", "rows", "transposed", "transposed")
        x_m = x
        kv = lambda a: a
    else:
        x_m = x.reshape(1, b * t, d)
        u_s5, u_rw = proj_plain(x_m, p['norm1_g'], p['w_mix'], (width, n_rwkv), b * t, False)
        y_s5, s5_re, s5_im = s5_mixer(u_s5.reshape(b, t, width).transpose(1, 0, 2), *s5_args)
        y_s5 = y_s5.transpose(1, 0, 2).reshape(1, b * t, width)
        x2 = x.reshape(b * t, d)
        seq = lambda a: a.reshape(b, t, width)
        q_sb, k_sb, v_sb = proj_qkv(x2, p['norm1_g'], p['w_sb'], p['sb_q_g'], p['sb_k_g'], b * t,
                                    SB_Q_SCALE)
        q_mb, k_mb, v_mb = proj_qkv(x2, p['norm1_g'], p['w_mb'], p['moba_q_g'], p['moba_k_g'], b * t,
                                    MOBA_Q_SCALE)
        y_sb = sb_sample(seq(q_sb), seq(k_sb), seq(v_sb), past['sb_k'], past['sb_v'], past['pages'],
                         tiles['sb_pages'])
        y_mb = moba_sample(seq(q_mb), seq(k_mb), seq(v_mb), past['moba_k'], past['moba_v'], past['pages'],
                           tiles['moba_pages'])
        y_sb, y_mb = y_sb.reshape(1, b * t, width), y_mb.reshape(1, b * t, width)
        layouts = ("rows",) * N_BRANCH
        kv = lambda a: a.reshape(b, t, n_heads, HEAD_DIM)

    y_rw, wkv, shift = rwkv_mixer(u_rw.reshape(b, t, n_rwkv), st['shift'], st['wkv'], p, tiles['rwkv_tt'],
                                  tiles['rwkv_chunk'], tiles['rwkv_passes'])
    ys = [y_s5, y_rw.reshape(x_m.shape[0], x_m.shape[1], width), y_sb, y_mb]
    x1 = merge(x_m, p['norm1_g'], p['w_gate'], ys, layouts, p['w_branch'], p['w_out'], tiles['merge_tm'])
    nseq, tt = (1, tiles['ffn_tt']) if prompt else (b, t)
    x_out, conv = conv_ffn(x1.reshape(b, t, d), p['norm2_g'], p['ffn_w_up'], p['ffn_conv_w'], p['ffn_conv_b'],
                           p['ffn_w_down'], st['conv'], nseq, tt, tiles['ffn_cols'])
    new_st = (s5_re.reshape(st['s5_re'].shape), s5_im.reshape(st['s5_im'].shape), wkv, shift, conv,
              kv(k_sb), kv(v_sb), kv(k_mb), kv(v_mb))
    return x_out, new_st


def kernel(x_prompt, x_sample, state_s5_re, state_s5_im, state_rwkv_wkv, state_rwkv_shift, state_ffn_conv, cache_sb_k, cache_sb_v, cache_moba_k, cache_moba_v, page_table, norm1_g, w_in, s5_a_re, s5_a_im, s5_log_dt, s5_b_re, s5_b_im, s5_c_re, s5_c_im, s5_d, s5_w_glu, rwkv_mu, rwkv_w0, rwkv_w_up, rwkv_a0, rwkv_a_up, rwkv_g_up, rwkv_k_k, rwkv_k_a, rwkv_r_k, rwkv_ln_w, rwkv_ln_b, sb_q_g, sb_k_g, moba_q_g, moba_k_g, w_branch, w_out, norm2_g, ffn_w_up, ffn_conv_w, ffn_conv_b, ffn_w_down):
    depth = w_in.shape[0]
    bp = x_prompt.shape[0]
    n_pool, page_size, n_heads, head_dim = cache_sb_k.shape[1:]
    width = n_heads * head_dim
    pool = lambda c: c.transpose(0, 1, 3, 4, 2).reshape(depth * n_pool, width, page_size)
    pools = dict(sb_k=pool(cache_sb_k), sb_v=pool(cache_sb_v), moba_k=pool(cache_moba_k), moba_v=pool(cache_moba_v))
    g_s5, p_s5 = s5_a_re.shape[1:]
    n_rwkv = state_rwkv_shift.shape[-1]
    d_ff2 = state_ffn_conv.shape[-1]
    zero_st = dict(s5_re=jnp.zeros((bp, g_s5, p_s5), F32), s5_im=jnp.zeros((bp, g_s5, p_s5), F32),
                   shift=jnp.zeros((bp, n_rwkv), F32), wkv=jnp.zeros((bp, n_heads, head_dim, head_dim), F32),
                   conv=jnp.zeros((bp, CONV_W - 1, d_ff2), F32))
    yp, ys = x_prompt, x_sample
    states_p, states_s = [], []
    kv_sb = kv_mb = None
    for l in range(depth):
        ab_re, ab_im, bb, cc = s5_operands(s5_a_re[l], s5_a_im[l], s5_log_dt[l], s5_b_re[l], s5_b_im[l],
                                           s5_c_re[l], s5_c_im[l])
        width_l = w_in.shape[1] // 2
        off_sb = width_l + state_rwkv_shift.shape[-1]
        off_mb, off_gate = off_sb + 3 * width_l, off_sb + 6 * width_l
        w_piece = lambda a, b: w_in[l, :, a:b].astype(BF16)
        p = dict(layer=l, depth=depth, norm1_g=norm1_g[l], w_mix=w_piece(0, off_sb), w_sb=w_piece(off_sb, off_mb),
                 w_mb=w_piece(off_mb, off_gate), w_gate=w_piece(off_gate, w_in.shape[2]),
                 s5_ab_re=ab_re, s5_ab_im=ab_im,
                 s5_bb={1: bb.astype(BF16), 3: bb}, s5_cc={1: cc.astype(BF16), 3: cc}, s5_d=s5_d[l],
                 s5_w_glu=s5_w_glu[l].astype(BF16), rwkv_mu=rwkv_mu[l], rwkv_w0=rwkv_w0[l],
                 rwkv_w_up=rwkv_w_up[l], rwkv_a0=rwkv_a0[l], rwkv_a_up=rwkv_a_up[l], rwkv_g_up=rwkv_g_up[l],
                 rwkv_k_k=rwkv_k_k[l], rwkv_k_a=rwkv_k_a[l], rwkv_r_k=rwkv_r_k[l].reshape(-1),
                 rwkv_ln_w=rwkv_ln_w[l], rwkv_ln_b=rwkv_ln_b[l], sb_q_g=sb_q_g[l], sb_k_g=sb_k_g[l],
                 moba_q_g=moba_q_g[l], moba_k_g=moba_k_g[l], w_branch=w_branch[l].astype(BF16),
                 w_out=w_out[l].astype(BF16), norm2_g=norm2_g[l], ffn_w_up=ffn_w_up[l].astype(BF16),
                 ffn_conv_w=ffn_conv_w[l], ffn_conv_b=ffn_conv_b[l], ffn_w_down=ffn_w_down[l].astype(BF16))
        yp, st_p = _layer(yp, dict(zero_st, kv_sb=kv_sb, kv_mb=kv_mb), None, p, PROMPT_TILES)
        kv_sb, kv_mb = st_p[5:7], st_p[7:9]
        st_s = dict(s5_re=state_s5_re[l], s5_im=state_s5_im[l], shift=state_rwkv_shift[l], wkv=state_rwkv_wkv[l],
                    conv=state_ffn_conv[l])
        past = dict(pools, pages=page_table + l * n_pool)
        ys, st_s = _layer(ys, st_s, past, p, SAMPLE_TILES)
        states_p.append(st_p[:5])
        states_s.append(st_s)
    t_p = x_prompt.shape[1]
    kv_out = lambda a: a.reshape(depth, bp, n_heads, head_dim, t_p).transpose(0, 1, 4, 2, 3)
    stacked_p = [jnp.stack(z, axis=0) for z in zip(*states_p)] + [kv_out(a) for a in (*kv_sb, *kv_mb)]
    stacked_s = [jnp.stack(z, axis=0) for z in zip(*states_s)]
    out = [yp, ys]
    for a, c in zip(stacked_p, stacked_s):
        out += [a, c]
    return tuple(out)
```
